```python
import jax
import jax.numpy as jnp
from jax import lax
import numpy as np

D_MODEL = 1024
BATCH = 8
SEQ = 8192
DEPTH = 2

D_MIX = D_MODEL
MLSTM_HEADS = 4
MLSTM_DH = (D_MIX // 2) // MLSTM_HEADS
MLSTM_W = MLSTM_HEADS * MLSTM_DH
GLA_HEADS = 4
GLA_DV = (D_MIX // 2) // GLA_HEADS
GLA_DK = GLA_DV // 2
GLA_WV = GLA_HEADS * GLA_DV
GLA_WK = GLA_HEADS * GLA_DK
GLA_GATE_RANK = 16
GLA_GATE_NORM = 16.0
CONV_WIDTH = 4
MLSTM_CHUNK = 128
GLA_CHUNK = 64
D_PROJ = 4 * MLSTM_W + 2 * MLSTM_HEADS + 2 * GLA_WK + 2 * GLA_WV + GLA_GATE_RANK
N_GROUPS = 4
EXPERTS_PER_GROUP = 8
N_EXPERTS = N_GROUPS * EXPERTS_PER_GROUP
TOP_K = 2
D_EXPERT = D_MODEL // 2
MOE_BLOCK = 256
EPS = 1e-6

kernel_name = 'hymba_mlstm_gla_hmoe'


def rmsnorm(x, g):
    xf = x.astype(jnp.float32)
    y = xf * lax.rsqrt(jnp.mean(xf * xf, axis=-1, keepdims=True) + EPS)
    return (y * g.astype(jnp.float32)).astype(x.dtype)


def head_rmsnorm(h):
    h = h * lax.rsqrt(jnp.mean(h * h, axis=-1, keepdims=True) + EPS)
    return h.reshape(h.shape[0], h.shape[1], -1)


def split_proj(p):
    sizes = [MLSTM_W, MLSTM_W, MLSTM_W, MLSTM_W, MLSTM_HEADS, MLSTM_HEADS,
             GLA_WK, GLA_WK, GLA_WV, GLA_WV, GLA_GATE_RANK]
    offs, acc = [], 0
    for s in sizes[:-1]:
        acc += s
        offs.append(acc)
    return jnp.split(p, offs, axis=-1)


def heads(t, n_heads):
    return t.reshape(t.shape[0], t.shape[1], n_heads, -1)


def to_chunks(t, L):
    B, S = t.shape[0], t.shape[1]
    t = t.reshape((B, S // L, L) + t.shape[2:])
    return jnp.swapaxes(jnp.moveaxis(t, 1, 0), 2, 3)


def from_chunks(y):
    y = jnp.moveaxis(jnp.swapaxes(y, 2, 3), 0, 1)
    B, NC, L = y.shape[0], y.shape[1], y.shape[2]
    return y.reshape((B, NC * L) + y.shape[3:])


def causal_dwconv(u, w, b):
    out = lax.conv_general_dilated(u, w[:, None, :], window_strides=(1,),
                                   padding=[(CONV_WIDTH - 1, 0)],
                                   dimension_numbers=('NWC', 'WIO', 'NWC'),
                                   feature_group_count=u.shape[-1])
    return out + b


def mlstm_chunked(q, k, v, ig, lf):
    B, S, H, dk = q.shape
    dv = v.shape[-1]
    L = MLSTM_CHUNK
    tri = jnp.tril(jnp.ones((L, L), dtype=bool))

    def body(carry, inp):
        C, n, m = carry
        qc, kc, vc, ic, fc = inp
        b = jnp.cumsum(fc, axis=-1)
        logD = jnp.where(tri, b[..., :, None] - b[..., None, :] + ic[..., None, :], -jnp.inf)
        m_inter = b + m[..., None]
        m_j = jnp.maximum(m_inter, jnp.max(logD, axis=-1))
        s = jnp.einsum('bhjd,bhsd->bhjs', qc, kc) * jnp.exp(logD - m_j[..., None])
        a = jnp.exp(m_inter - m_j)
        num = jnp.einsum('bhjs,bhsv->bhjv', s, vc) + a[..., None] * jnp.einsum('bhjd,bhdv->bhjv', qc, C)
        den = jnp.sum(s, axis=-1) + a * jnp.einsum('bhjd,bhd->bhj', qc, n)
        h = num / jnp.maximum(jnp.abs(den), jnp.exp(-m_j))[..., None]
        g = b[..., -1]
        logw = g[..., None] - b + ic
        m_new = jnp.maximum(g + m, jnp.max(logw, axis=-1))
        kw = kc * jnp.exp(logw - m_new[..., None])[..., None]
        decay = jnp.exp(g + m - m_new)
        C_new = decay[..., None, None] * C + jnp.einsum('bhsd,bhsv->bhdv', kw, vc)
        n_new = decay[..., None] * n + jnp.sum(kw, axis=2)
        return (C_new, n_new, m_new), h

    init = (jnp.zeros((B, H, dk, dv), jnp.float32), jnp.zeros((B, H, dk), jnp.float32),
            jnp.zeros((B, H), jnp.float32))
    xs = (to_chunks(q, L), to_chunks(k, L), to_chunks(v, L), to_chunks(ig, L), to_chunks(lf, L))
    _, hs = lax.scan(body, init, xs)
    return from_chunks(hs)


def gla_chunked(q, k, v, la):
    B, S, H, dk = q.shape
    dv = v.shape[-1]
    L = GLA_CHUNK
    tri = jnp.tril(jnp.ones((L, L), dtype=bool))[..., None]

    def body(Sst, inp):
        qc, kc, vc, ac = inp
        Bc = jnp.cumsum(ac, axis=2)
        diff = jnp.where(tri, Bc[:, :, :, None, :] - Bc[:, :, None, :, :], -jnp.inf)
        A = jnp.einsum('bhjc,bhsc,bhjsc->bhjs', qc, kc, jnp.exp(diff))
        o = jnp.einsum('bhjs,bhsv->bhjv', A, vc) + jnp.einsum('bhjc,bhcv->bhjv', qc * jnp.exp(Bc), Sst)
        gL = Bc[:, :, -1, :]
        kd = kc * jnp.exp(gL[:, :, None, :] - Bc)
        S_new = jnp.exp(gL)[..., None] * Sst + jnp.einsum('bhsc,bhsv->bhcv', kd, vc)
        return S_new, o

    init = jnp.zeros((B, H, dk, dv), jnp.float32)
    xs = (to_chunks(q, L), to_chunks(k, L), to_chunks(v, L), to_chunks(la, L))
    _, os_ = lax.scan(body, init, xs)
    return from_chunks(os_)


def parallel_mixer(xn, w_in, conv_w, conv_b, b_gate, w_gla_gate, b_gla_gate, head_norm, w_out):
    f32 = jnp.float32
    Bsz, S, _ = xn.shape
    p = xn @ w_in
    q_m, k_m, v_m, o_m, i_m, f_m, q_g, k_g, v_g, z_g, r_g = split_proj(p)
    qk = jax.nn.silu(causal_dwconv(jnp.concatenate([q_m, k_m], axis=-1), conv_w, conv_b))
    q_m, k_m = jnp.split(qk, 2, axis=-1)
    gates = jnp.concatenate([i_m, f_m], axis=-1).astype(f32) + b_gate.astype(f32)
    ig = gates[..., :MLSTM_HEADS]
    lf = jax.nn.log_sigmoid(gates[..., MLSTM_HEADS:])
    h_m = mlstm_chunked(heads(q_m, MLSTM_HEADS).astype(f32),
                        heads(k_m, MLSTM_HEADS).astype(f32) * (MLSTM_DH ** -0.5),
                        heads(v_m, MLSTM_HEADS).astype(f32), ig, lf)
    la = jax.nn.log_sigmoid((r_g @ w_gla_gate + b_gla_gate).astype(f32)) / GLA_GATE_NORM
    h_g = gla_chunked(heads(q_g, GLA_HEADS).astype(f32) * (GLA_DK ** -0.5),
                      heads(k_g, GLA_HEADS).astype(f32),
                      heads(v_g, GLA_HEADS).astype(f32), heads(la, GLA_HEADS))
    hc = jnp.concatenate([head_rmsnorm(h_m), head_rmsnorm(h_g)], axis=-1)
    gate_out = jnp.concatenate([jax.nn.sigmoid(o_m), jax.nn.silu(z_g)], axis=-1).astype(f32)
    y = hc * head_norm.astype(f32) * gate_out
    return y.astype(xn.dtype) @ w_out


def hier_moe(xn, w_group, b_group, w_expert, b_expert, w_gu, w_down):
    f32 = jnp.float32
    Bsz, S, D = xn.shape
    T = Bsz * S
    xt = xn.reshape(T, D)
    g_logit = (xt @ w_group).astype(f32) + b_group.astype(f32)
    g_prob = jax.nn.softmax(g_logit, axis=-1)
    g_sel = jnp.argmax(g_logit, axis=-1)
    g_gate = jnp.take_along_axis(g_prob, g_sel[:, None], axis=-1)
    e_logit = ((xt @ w_expert).astype(f32) + b_expert.astype(f32)).reshape(T, N_GROUPS, EXPERTS_PER_GROUP)
    e_logit = jnp.take_along_axis(e_logit, g_sel[:, None, None], axis=1)[:, 0]
    top_v, top_i = lax.top_k(e_logit, TOP_K)
    wts = jax.nn.softmax(top_v, axis=-1) * g_gate
    eid = (g_sel[:, None] * EXPERTS_PER_GROUP + top_i).reshape(-1).astype(jnp.int32)
    tid = jnp.repeat(jnp.arange(T, dtype=jnp.int32), TOP_K)
    wflat = wts.reshape(-1)
    order = jnp.argsort(eid)
    se, st, sw = eid[order], tid[order], wflat[order]
    counts = jnp.bincount(eid, length=N_EXPERTS).astype(jnp.int32)
    pcounts = (counts + MOE_BLOCK - 1) // MOE_BLOCK * MOE_BLOCK
    pend = jnp.cumsum(pcounts)
    pstart = pend - pcounts
    start = jnp.cumsum(counts) - counts
    n_assign = T * TOP_K
    pos = pstart[se] + jnp.arange(n_assign, dtype=jnp.int32) - start[se]
    n_blocks = (n_assign + MOE_BLOCK - 1) // MOE_BLOCK + N_EXPERTS
    tok_buf = jnp.full((n_blocks * MOE_BLOCK,), T, jnp.int32).at[pos].set(st)
    w_buf = jnp.zeros((n_blocks * MOE_BLOCK,), f32).at[pos].set(sw)
    blk_exp = jnp.minimum(jnp.searchsorted(pend, jnp.arange(n_blocks, dtype=jnp.int32) * MOE_BLOCK,
                                           side='right'), N_EXPERTS - 1)
    xt_pad = jnp.concatenate([xt, jnp.zeros((1, D), xt.dtype)], axis=0)

    def expert_block(args):
        tok, wb, e = args
        xb = xt_pad[tok]
        gate, up = jnp.split(xb @ w_gu[e], 2, axis=-1)
        yb = (jax.nn.silu(gate) * up) @ w_down[e]
        return yb.astype(f32) * wb[:, None]

    ys = lax.map(expert_block, (tok_buf.reshape(n_blocks, MOE_BLOCK),
                                w_buf.reshape(n_blocks, MOE_BLOCK), blk_exp))
    y = jax.ops.segment_sum(ys.reshape(-1, D), tok_buf, num_segments=T + 1)[:T]
    return y.astype(xn.dtype).reshape(Bsz, S, D)


def setup_inputs(seed: int = 0) -> dict:
    key = jax.random.key(seed)
    ks = jax.random.split(key, 20)
    f32 = jnp.float32
    nrm = lambda k, shape, scale: jax.random.normal(k, shape, f32) * scale
    x = jax.random.normal(ks[0], (BATCH, SEQ, D_MODEL), f32)
    norm_mix = 1.0 + nrm(ks[1], (DEPTH, D_MODEL), 0.02)
    w_in = nrm(ks[2], (DEPTH, D_MODEL, D_PROJ), D_MODEL ** -0.5)
    conv_w = nrm(ks[3], (DEPTH, CONV_WIDTH, 2 * MLSTM_W), CONV_WIDTH ** -0.5)
    conv_b = nrm(ks[4], (DEPTH, 2 * MLSTM_W), 0.02)
    i_bias = nrm(ks[5], (DEPTH, MLSTM_HEADS), 0.1)
    f_bias = jnp.linspace(3.0, 6.0, MLSTM_HEADS, dtype=f32)[None, :] + nrm(ks[6], (DEPTH, MLSTM_HEADS), 0.1)
    b_mlstm_gate = jnp.concatenate([i_bias, f_bias], axis=-1)
    w_gla_gate = nrm(ks[7], (DEPTH, GLA_GATE_RANK, GLA_WK), GLA_GATE_RANK ** -0.5)
    b_gla_gate = nrm(ks[8], (DEPTH, GLA_WK), 0.1)
    head_norm = 1.0 + nrm(ks[9], (DEPTH, D_MIX), 0.02)
    w_out = nrm(ks[10], (DEPTH, D_MIX, D_MODEL), D_MIX ** -0.5)
    norm_ffn = 1.0 + nrm(ks[11], (DEPTH, D_MODEL), 0.02)
    w_group = nrm(ks[12], (DEPTH, D_MODEL, N_GROUPS), D_MODEL ** -0.5)
    b_group = nrm(ks[13], (DEPTH, N_GROUPS), 0.01)
    w_expert = nrm(ks[14], (DEPTH, D_MODEL, N_EXPERTS), D_MODEL ** -0.5)
    b_expert = nrm(ks[15], (DEPTH, N_EXPERTS), 0.01)
    w_gu = nrm(ks[16], (DEPTH, N_EXPERTS, D_MODEL, 2 * D_EXPERT), D_MODEL ** -0.5)
    w_down = nrm(ks[17], (DEPTH, N_EXPERTS, D_EXPERT, D_MODEL), D_EXPERT ** -0.5)
    norm_final = 1.0 + nrm(ks[18], (D_MODEL,), 0.02)
    return {'x': x, 'norm_mix': norm_mix, 'w_in': w_in, 'conv_w': conv_w, 'conv_b': conv_b,
            'b_mlstm_gate': b_mlstm_gate, 'w_gla_gate': w_gla_gate, 'b_gla_gate': b_gla_gate,
            'head_norm': head_norm, 'w_out': w_out, 'norm_ffn': norm_ffn, 'w_group': w_group,
            'b_group': b_group, 'w_expert': w_expert, 'b_expert': b_expert, 'w_gu': w_gu,
            'w_down': w_down, 'norm_final': norm_final}


def reference(x, norm_mix, w_in, conv_w, conv_b, b_mlstm_gate, w_gla_gate, b_gla_gate,
              head_norm, w_out, norm_ffn, w_group, b_group, w_expert, b_expert, w_gu,
              w_down, norm_final):
    h = x
    for l in range(DEPTH):
        h = h + parallel_mixer(rmsnorm(h, norm_mix[l]), w_in[l], conv_w[l], conv_b[l],
                               b_mlstm_gate[l], w_gla_gate[l], b_gla_gate[l], head_norm[l], w_out[l])
        h = h + hier_moe(rmsnorm(h, norm_ffn[l]), w_group[l], b_group[l], w_expert[l],
                         b_expert[l], w_gu[l], w_down[l])
    return rmsnorm(h, norm_final)
```

```python
import functools

import jax
import jax.numpy as jnp
from jax import lax
from jax.experimental import pallas as pl
from jax.experimental.pallas import tpu as pltpu

F32 = jnp.float32
BF16 = jnp.bfloat16
I32 = jnp.int32

EPS = 1e-6
LANES = 128
SUBLANES = 8
VMEM_LIMIT = 56 * 1024 * 1024

N_HEADS = 4
DH = 128
GLA_DK = 64
GLA_RANK = 16
GLA_GATE_NORM = 16.0
CONV_W = 4
MLSTM_L = 128
GLA_L = 64
GLA_R = 16
N_GROUPS = 4
EPG = 8
N_EXPERTS = N_GROUPS * EPG
MOE_BLOCK = 256
EXP_LANE0 = N_GROUPS

NEG_INF = float("-inf")


def _cparams(sem):
    return pltpu.CompilerParams(dimension_semantics=sem, vmem_limit_bytes=VMEM_LIMIT)


def _split3(x):
    hi = x.astype(BF16)
    r1 = x - hi.astype(F32)
    mid = r1.astype(BF16)
    lo = (r1 - mid.astype(F32)).astype(BF16)
    return hi, mid, lo


def _cumsum_rows(tri_bf, x):
    hi, mid, lo = _split3(x)
    dot = functools.partial(jnp.dot, preferred_element_type=F32)
    return dot(tri_bf, hi) + dot(tri_bf, mid) + dot(tri_bf, lo)


def _sigmoid(x):
    return 1.0 / (1.0 + jnp.exp(-x))


def _log_sigmoid(x):
    return jnp.minimum(x, 0.0) - jnp.log(1.0 + jnp.exp(-jnp.abs(x)))


def _inproj_body(x_ref, g_ref, wm_ref, wg_ref, ws_ref, om_ref, og_ref, os_ref):
    x = x_ref[...]
    xn = x * lax.rsqrt(jnp.mean(x * x, axis=-1, keepdims=True) + EPS) * g_ref[...]
    xb = xn.astype(BF16)
    om_ref[...] = jnp.dot(xb, wm_ref[...], preferred_element_type=F32).astype(BF16)
    og_ref[...] = jnp.dot(xb, wg_ref[...], preferred_element_type=F32).astype(BF16)
    os_ref[...] = jnp.dot(xb, ws_ref[...], preferred_element_type=F32)


def _inproj(h2d, gain, wm, wg, ws, tm):
    T, D = h2d.shape
    nm, ng, ns = wm.shape[1], wg.shape[1], ws.shape[1]
    const = lambda i: (0, 0)
    row = lambda i: (i, 0)
    return pl.pallas_call(
        _inproj_body,
        grid=(T // tm,),
        in_specs=[pl.BlockSpec((tm, D), row), pl.BlockSpec((1, D), const),
                  pl.BlockSpec((D, nm), const), pl.BlockSpec((D, ng), const),
                  pl.BlockSpec((D, ns), const)],
        out_specs=[pl.BlockSpec((tm, nm), row), pl.BlockSpec((tm, ng), row),
                   pl.BlockSpec((tm, ns), row)],
        out_shape=[jax.ShapeDtypeStruct((T, nm), BF16), jax.ShapeDtypeStruct((T, ng), BF16),
                   jax.ShapeDtypeStruct((T, ns), F32)],
        compiler_params=_cparams(("parallel",)),
    )(h2d, gain, wm, wg, ws)


def _mlstm_body(main_ref, gate_ref, cw_ref, cb_ref, bg_ref, hn_ref, y_ref, ubuf, cst, mst):
    L = MLSTM_L
    W = N_HEADS * DH
    dot = functools.partial(jnp.dot, preferred_element_type=F32)

    @pl.when(pl.program_id(1) == 0)
    def _():
        ubuf[0:SUBLANES, :] = jnp.zeros((SUBLANES, 2 * W), F32)
        cst[...] = jnp.zeros(cst.shape, F32)
        mst[...] = jnp.zeros(mst.shape, F32)

    ubuf[SUBLANES:SUBLANES + L, :] = main_ref[:, 0:2 * W].astype(F32)
    acc = cb_ref[...]
    for i in range(CONV_W):
        off = SUBLANES - (CONV_W - 1) + i
        acc = acc + ubuf[off:off + L, :] * cw_ref[i:i + 1, :]
    ubuf[0:SUBLANES, :] = ubuf[L:L + SUBLANES, :]
    qk = acc * _sigmoid(acc)

    row = lax.broadcasted_iota(I32, (L, L), 0)
    col = lax.broadcasted_iota(I32, (L, L), 1)
    tri = col <= row
    tri_bf = jnp.where(tri, 1.0, 0.0).astype(BF16)
    lane = lax.broadcasted_iota(I32, (L, LANES), 1)
    gpre = gate_ref[...] + bg_ref[...]
    xg = jnp.where(lane < N_HEADS, gpre, _log_sigmoid(gpre))
    bcs = _cumsum_rows(tri_bf, xg)
    xg_t = xg.T
    bcs_t = bcs.T

    ones_col = jnp.where(lane == 0, 1.0, 0.0).astype(BF16)
    for h in range(N_HEADS):
        q = qk[:, h * DH:(h + 1) * DH].astype(BF16)
        k = (qk[:, W + h * DH:W + (h + 1) * DH] * (DH ** -0.5))
        v = main_ref[:, 2 * W + h * DH:2 * W + (h + 1) * DH]
        vext = jnp.concatenate([v, ones_col], axis=1)
        ig_col = xg[:, h:h + 1]
        b_col = bcs[:, N_HEADS + h:N_HEADS + h + 1]
        ig_row = xg_t[h:h + 1, :]
        b_row = bcs_t[N_HEADS + h:N_HEADS + h + 1, :]
        m_prev = mst[h][:, 0:1]
        cext = cst[h]

        logd = jnp.where(tri, b_col - b_row + ig_row, NEG_INF)
        m_inter = b_col + m_prev
        m_j = jnp.maximum(m_inter, jnp.max(logd, axis=-1, keepdims=True))
        s = lax.dot_general(q, k.astype(BF16), (((1,), (1,)), ((), ())),
                            preferred_element_type=F32) * jnp.exp(logd - m_j)
        a = jnp.exp(m_inter - m_j)
        num_ext = dot(s.astype(BF16), vext) + a * dot(q, cext.astype(BF16))
        den = num_ext[:, DH:DH + 1]
        hh = num_ext[:, 0:DH] / jnp.maximum(jnp.abs(den), jnp.exp(-m_j))

        g = b_row[:, L - 1:L]
        m_new = jnp.maximum(g + m_prev, jnp.max(g - b_row + ig_row, axis=-1, keepdims=True))
        kw = (k * jnp.exp(g - b_col + ig_col - m_new)).astype(BF16)
        decay = jnp.exp(g + m_prev - m_new)
        cst[h] = decay * cext + lax.dot_general(kw, vext, (((0,), (0,)), ((), ())),
                                                preferred_element_type=F32)
        mst[h] = jnp.broadcast_to(m_new, (1, LANES))

        hn = hh * lax.rsqrt(jnp.mean(hh * hh, axis=-1, keepdims=True) + EPS)
        o_gate = _sigmoid(main_ref[:, 3 * W + h * DH:3 * W + (h + 1) * DH].astype(F32))
        y_ref[:, h * DH:(h + 1) * DH] = (hn * hn_ref[:, h * DH:(h + 1) * DH] * o_gate).astype(BF16)


def _mlstm(main, gates, conv_w, conv_b, b_gate, head_norm, B, S):
    T = B * S
    L = MLSTM_L
    NC = S // L
    W = N_HEADS * DH
    tok = lambda b, c: (b * NC + c, 0)
    const = lambda b, c: (0, 0)
    return pl.pallas_call(
        _mlstm_body,
        grid=(B, NC),
        in_specs=[pl.BlockSpec((L, 4 * W), tok), pl.BlockSpec((L, LANES), tok),
                  pl.BlockSpec((CONV_W, 2 * W), const), pl.BlockSpec((1, 2 * W), const),
                  pl.BlockSpec((1, LANES), const), pl.BlockSpec((1, W), const)],
        out_specs=pl.BlockSpec((L, W), tok),
        out_shape=jax.ShapeDtypeStruct((T, W), BF16),
        scratch_shapes=[pltpu.VMEM((L + SUBLANES, 2 * W), F32),
                        pltpu.VMEM((N_HEADS, DH, 2 * DH), F32),
                        pltpu.VMEM((N_HEADS, 1, LANES), F32)],
        compiler_params=_cparams(("arbitrary", "arbitrary")),
    )(main, gates, conv_w, conv_b, b_gate, head_norm)


def _gla_chunk(r0, gla_ref, gate_ref, wgg_ref, bgg_ref, hn_ref, y_ref, sst):
    L, R = GLA_L, GLA_R
    WK = N_HEADS * GLA_DK
    WV = N_HEADS * DH
    dot = functools.partial(jnp.dot, preferred_element_type=F32)
    nt = (((1,), (1,)), ((), ()))
    tn = (((0,), (0,)), ((), ()))
    rows = pl.ds(r0, L)

    row = lax.broadcasted_iota(I32, (L, L), 0)
    col = lax.broadcasted_iota(I32, (L, L), 1)
    tri_bf = jnp.where(col <= row, 1.0, 0.0).astype(BF16)
    lane = lax.broadcasted_iota(I32, (L, LANES), 1)
    rowl = lax.broadcasted_iota(I32, (L, LANES), 0)
    lane_r = lax.broadcasted_iota(I32, (R, LANES), 1)
    row_r = lax.broadcasted_iota(I32, (R, LANES), 0)
    wsel = jnp.where(lax.broadcasted_iota(I32, (LANES, 2 * LANES), 0) // GLA_DK
                     == lax.broadcasted_iota(I32, (LANES, 2 * LANES), 1) // LANES, 1.0, 0.0).astype(BF16)

    la = _log_sigmoid(dot(gate_ref[rows, :].astype(BF16), wgg_ref[...]) + bgg_ref[...]) / GLA_GATE_NORM

    for p in range(N_HEADS // 2):
        ls = slice(p * LANES, (p + 1) * LANES)
        bc = _cumsum_rows(tri_bf, la[:, ls])
        q2 = gla_ref[rows, p * LANES:(p + 1) * LANES].astype(F32) * (GLA_DK ** -0.5)
        k2 = gla_ref[rows, WK + p * LANES:WK + (p + 1) * LANES].astype(F32)
        g_last = bc[L - 1:L, :]
        q_in = q2 * jnp.exp(bc)
        kd = k2 * jnp.exp(g_last - bc)
        decay = jnp.exp(g_last)

        k_off = [None]
        q_off = [None]
        for j in range(1, L // R):
            rj = bc[j * R:j * R + 1, :]
            k_off.append((k2 * jnp.exp(jnp.where(rowl < j * R, rj - bc, NEG_INF))).astype(BF16))
            q_off.append(q2[j * R:(j + 1) * R, :] * jnp.exp(bc[j * R:(j + 1) * R, :] - rj))

        diag = []
        for j in range(L // R):
            qb = q2[j * R:(j + 1) * R, :]
            bq = bc[j * R:(j + 1) * R, :]
            terms = []
            for s in range(R):
                krow = k2[j * R + s:j * R + s + 1, :]
                brow = bc[j * R + s:j * R + s + 1, :]
                d = jnp.where(row_r >= s, bq - brow, NEG_INF)
                terms.append((qb * krow * jnp.exp(d)).astype(BF16))
            diag.append(dot(jnp.concatenate(terms, axis=0), wsel))

        for hh in range(2):
            h = 2 * p + hh
            mh = (lane // GLA_DK) == hh
            v = gla_ref[rows, 2 * WK + h * DH:2 * WK + (h + 1) * DH]
            st = sst[h]
            o = lax.dot_general(jnp.where(mh, q_in, 0.0).astype(BF16), st.astype(BF16), nt,
                                preferred_element_type=F32)
            sst[h] = st * decay + lax.dot_general(v, jnp.where(mh, kd, 0.0).astype(BF16), tn,
                                                  preferred_element_type=F32)
            a_rows = []
            for j in range(L // R):
                res = diag[j][:, hh * LANES:(hh + 1) * LANES]
                blk = jnp.zeros((R, LANES), F32)
                for s in range(R):
                    blk = jnp.where(lane_r == j * R + s, res[s * R:(s + 1) * R, :], blk)
                blk = blk[:, 0:L]
                if j > 0:
                    mq = (lane_r // GLA_DK) == hh
                    blk = blk + lax.dot_general(jnp.where(mq, q_off[j], 0.0).astype(BF16), k_off[j],
                                                nt, preferred_element_type=F32)
                a_rows.append(blk)
            a_mat = jnp.concatenate(a_rows, axis=0).astype(BF16)
            o = o + dot(a_mat, v)
            on = o * lax.rsqrt(jnp.mean(o * o, axis=-1, keepdims=True) + EPS)
            z = gla_ref[rows, 2 * WK + WV + h * DH:2 * WK + WV + (h + 1) * DH].astype(F32)
            y_ref[rows, h * DH:(h + 1) * DH] = (
                on * hn_ref[:, h * DH:(h + 1) * DH] * (z * _sigmoid(z))).astype(BF16)


def _gla_body(gla_ref, gate_ref, wgg_ref, bgg_ref, hn_ref, y_ref, sst, *, n_chunks):
    @pl.when(pl.program_id(1) == 0)
    def _():
        sst[...] = jnp.zeros(sst.shape, F32)

    def step(i, carry):
        _gla_chunk(pl.multiple_of(i * GLA_L, GLA_L), gla_ref, gate_ref, wgg_ref, bgg_ref, hn_ref,
                   y_ref, sst)
        return carry

    lax.fori_loop(0, n_chunks, step, 0)


def _gla(gla, gates, wgg, bgg, head_norm, B, S, lt):
    T = B * S
    NT = S // lt
    WK = N_HEADS * GLA_DK
    WV = N_HEADS * DH
    tok = lambda b, c: (b * NT + c, 0)
    const = lambda b, c: (0, 0)
    return pl.pallas_call(
        functools.partial(_gla_body, n_chunks=lt // GLA_L),
        grid=(B, NT),
        in_specs=[pl.BlockSpec((lt, 2 * WK + 2 * WV), tok), pl.BlockSpec((lt, LANES), tok),
                  pl.BlockSpec((LANES, WK), const), pl.BlockSpec((1, WK), const),
                  pl.BlockSpec((1, WV), const)],
        out_specs=pl.BlockSpec((lt, WV), tok),
        out_shape=jax.ShapeDtypeStruct((T, WV), BF16),
        scratch_shapes=[pltpu.VMEM((N_HEADS, DH, LANES), F32)],
        compiler_params=_cparams(("arbitrary", "arbitrary")),
    )(gla, gates, wgg, bgg, head_norm)


def _outproj_router_body(ym_ref, yg_ref, h_ref, wo_ref, g_ref, wr_ref, br_ref,
                         hout_ref, xn_ref, route_ref, cnt_ref, carry):
    tm = h_ref.shape[0]
    W = ym_ref.shape[1]
    dot = functools.partial(jnp.dot, preferred_element_type=F32)

    @pl.when(pl.program_id(0) == 0)
    def _():
        carry[...] = jnp.zeros(carry.shape, F32)

    hnew = h_ref[...] + dot(ym_ref[...], wo_ref[0:W, :]) + dot(yg_ref[...], wo_ref[W:2 * W, :])
    hout_ref[...] = hnew
    xn = hnew * lax.rsqrt(jnp.mean(hnew * hnew, axis=-1, keepdims=True) + EPS) * g_ref[...]
    xn_ref[...] = xn

    xh, xm, xl = _split3(xn)
    wh, wm, wl = wr_ref[0], wr_ref[1], wr_ref[2]
    logits = (dot(xh, wh) + (dot(xh, wm) + dot(xm, wh))
              + (dot(xh, wl) + dot(xm, wm) + dot(xl, wh))) + br_ref[...]

    lane = lax.broadcasted_iota(I32, (tm, LANES), 1)
    lane_f = lane.astype(F32)
    big = float(LANES)
    gl = jnp.where(lane < N_GROUPS, logits, NEG_INF)
    gmax = jnp.max(gl, axis=-1, keepdims=True)
    gsel = jnp.min(jnp.where(gl == gmax, lane_f, big), axis=-1, keepdims=True)
    g_gate = 1.0 / jnp.sum(jnp.where(lane < N_GROUPS, jnp.exp(logits - gmax), 0.0),
                           axis=-1, keepdims=True)
    in_grp = ((lane >= EXP_LANE0) & (lane < EXP_LANE0 + N_EXPERTS)
              & (((lane - EXP_LANE0) // EPG).astype(F32) == gsel))
    el = jnp.where(in_grp, logits, NEG_INF)
    v1 = jnp.max(el, axis=-1, keepdims=True)
    i1 = jnp.min(jnp.where(el == v1, lane_f, big), axis=-1, keepdims=True)
    el2 = jnp.where(lane_f == i1, NEG_INF, el)
    v2 = jnp.max(el2, axis=-1, keepdims=True)
    i2 = jnp.min(jnp.where(el2 == v2, lane_f, big), axis=-1, keepdims=True)
    e2 = jnp.exp(v2 - v1)
    w1 = g_gate / (1.0 + e2)
    w2 = g_gate * e2 / (1.0 + e2)

    oh1 = lane_f == i1
    oh2 = lane_f == i2
    oh = jnp.where(oh1, 1.0, 0.0) + jnp.where(oh2, 1.0, 0.0)
    r_i = lax.broadcasted_iota(I32, (tm, tm), 0)
    c_i = lax.broadcasted_iota(I32, (tm, tm), 1)
    strict = jnp.where(c_i < r_i, 1.0, 0.0).astype(BF16)
    before = dot(strict, oh.astype(BF16)) + carry[0:1, :]
    rank1 = jnp.sum(jnp.where(oh1, before, 0.0), axis=-1, keepdims=True)
    rank2 = jnp.sum(jnp.where(oh2, before, 0.0), axis=-1, keepdims=True)
    total = carry[0:1, :] + jnp.sum(oh, axis=0, keepdims=True)
    carry[...] = jnp.broadcast_to(total, carry.shape)
    cnt_ref[...] = jnp.broadcast_to(total, cnt_ref.shape)

    rec = jnp.where(lane == 0, i1 - EXP_LANE0, 0.0)
    rec = jnp.where(lane == 1, i2 - EXP_LANE0, rec)
    rec = jnp.where(lane == 2, w1, rec)
    rec = jnp.where(lane == 3, w2, rec)
    rec = jnp.where(lane == 4, rank1, rec)
    rec = jnp.where(lane == 5, rank2, rec)
    route_ref[...] = rec


def _outproj_router(ym, yg, h2d, wo, gain, wr3, br, tm):
    T, D = h2d.shape
    W = ym.shape[1]
    row = lambda i: (i, 0)
    const = lambda i: (0, 0)
    return pl.pallas_call(
        _outproj_router_body,
        grid=(T // tm,),
        in_specs=[pl.BlockSpec((tm, W), row), pl.BlockSpec((tm, W), row), pl.BlockSpec((tm, D), row),
                  pl.BlockSpec((2 * W, D), const), pl.BlockSpec((1, D), const),
                  pl.BlockSpec((3, D, LANES), lambda i: (0, 0, 0)), pl.BlockSpec((1, LANES), const)],
        out_specs=[pl.BlockSpec((tm, D), row), pl.BlockSpec((tm, D), row),
                   pl.BlockSpec((tm, LANES), row), pl.BlockSpec((SUBLANES, LANES), const)],
        out_shape=[jax.ShapeDtypeStruct((T, D), F32), jax.ShapeDtypeStruct((T, D), F32),
                   jax.ShapeDtypeStruct((T, LANES), F32),
                   jax.ShapeDtypeStruct((SUBLANES, LANES), F32)],
        scratch_shapes=[pltpu.VMEM((SUBLANES, LANES), F32)],
        compiler_params=_cparams(("arbitrary",)),
    )(ym, yg, h2d, wo, gain, wr3, br)


def _dispatch_body(pos_ref, x_hbm, xs_in_hbm, xs_hbm, sem, *, td):
    del xs_in_hbm
    base = pl.program_id(0) * td

    def row_copy(src_row, dst_row):
        return pltpu.make_async_copy(x_hbm.at[pl.ds(src_row, 1)], xs_hbm.at[pl.ds(dst_row, 1)], sem)

    def issue(j, carry):
        row_copy(base + j, pos_ref[0, 0, 2 * j]).start()
        row_copy(base + j, pos_ref[0, 0, 2 * j + 1]).start()
        return carry

    lax.fori_loop(0, td, issue, 0)

    def drain(j, carry):
        row_copy(0, 0).wait()
        row_copy(0, 0).wait()
        return carry

    lax.fori_loop(0, td, drain, 0)


def _dispatch(pos3, xn, xs_init, td):
    T, D = xn.shape
    n = T // td
    return pl.pallas_call(
        functools.partial(_dispatch_body, td=td),
        grid=(n,),
        in_specs=[pl.BlockSpec((1, 1, 2 * td), lambda i: (i, 0, 0), memory_space=pltpu.SMEM),
                  pl.BlockSpec(memory_space=pl.ANY), pl.BlockSpec(memory_space=pl.ANY)],
        out_specs=pl.BlockSpec(memory_space=pl.ANY),
        out_shape=jax.ShapeDtypeStruct(xs_init.shape, xs_init.dtype),
        scratch_shapes=[pltpu.SemaphoreType.DMA(())],
        input_output_aliases={2: 0},
        compiler_params=_cparams(("arbitrary",)),
    )(pos3, xn, xs_init)


def _ffn_body(blk_exp_ref, nused_ref, xs_ref, wgu_ref, wd_ref, ys_ref):
    del blk_exp_ref
    de = wd_ref.shape[1]
    used = pl.program_id(0) < nused_ref[0]

    @pl.when(used)
    def _():
        gu = jnp.dot(xs_ref[...].astype(BF16), wgu_ref[0], preferred_element_type=F32)
        gate, up = gu[:, 0:de], gu[:, de:2 * de]
        act = (gate * _sigmoid(gate) * up).astype(BF16)
        ys_ref[...] = jnp.dot(act, wd_ref[0], preferred_element_type=F32)

    @pl.when(jnp.logical_not(used))
    def _():
        ys_ref[...] = jnp.zeros(ys_ref.shape, F32)


def _ffn(blk_exp, nused, xs, wgu, wd):
    NR, D = xs.shape
    nb = NR // MOE_BLOCK
    de = wd.shape[1]
    grid_spec = pltpu.PrefetchScalarGridSpec(
        num_scalar_prefetch=2,
        grid=(nb,),
        in_specs=[pl.BlockSpec((MOE_BLOCK, D), lambda b, be, nu: (b, 0)),
                  pl.BlockSpec((1, D, 2 * de), lambda b, be, nu: (be[b], 0, 0)),
                  pl.BlockSpec((1, de, D), lambda b, be, nu: (be[b], 0, 0))],
        out_specs=pl.BlockSpec((MOE_BLOCK, D), lambda b, be, nu: (b, 0)),
    )
    return pl.pallas_call(
        _ffn_body,
        grid_spec=grid_spec,
        out_shape=jax.ShapeDtypeStruct((NR, D), F32),
        compiler_params=_cparams(("arbitrary",)),
    )(blk_exp, nused, xs, wgu, wd)


def _combine_body(pos_ref, ys_hbm, h_ref, route_ref, gfin_ref, out_ref, gbuf, sem, *, tc, final):
    def row_copy(src_row, k, j):
        return pltpu.make_async_copy(ys_hbm.at[pl.ds(src_row, 1)], gbuf.at[k, pl.ds(j, 1)], sem)

    def issue(j, carry):
        row_copy(pos_ref[0, 0, 2 * j], 0, j).start()
        row_copy(pos_ref[0, 0, 2 * j + 1], 1, j).start()
        return carry

    lax.fori_loop(0, tc, issue, 0)

    def drain(j, carry):
        row_copy(0, 0, 0).wait()
        row_copy(0, 1, 0).wait()
        return carry

    lax.fori_loop(0, tc, drain, 0)

    w1 = route_ref[:, 2:3]
    w2 = route_ref[:, 3:4]
    out = h_ref[...] + (w1 * gbuf[0] + w2 * gbuf[1])
    if final:
        out = out * lax.rsqrt(jnp.mean(out * out, axis=-1, keepdims=True) + EPS) * gfin_ref[...]
    out_ref[...] = out


def _combine(pos3, ys, h2d, route, gfin, tc, final):
    T, D = h2d.shape
    row = lambda i: (i, 0)
    return pl.pallas_call(
        functools.partial(_combine_body, tc=tc, final=final),
        grid=(T // tc,),
        in_specs=[pl.BlockSpec((1, 1, 2 * tc), lambda i: (i, 0, 0), memory_space=pltpu.SMEM),
                  pl.BlockSpec(memory_space=pl.ANY),
                  pl.BlockSpec((tc, D), row), pl.BlockSpec((tc, LANES), row),
                  pl.BlockSpec((1, D), lambda i: (0, 0))],
        out_specs=pl.BlockSpec((tc, D), row),
        out_shape=jax.ShapeDtypeStruct((T, D), F32),
        scratch_shapes=[pltpu.VMEM((2, tc, D), F32), pltpu.SemaphoreType.DMA(())],
        compiler_params=_cparams(("arbitrary",)),
    )(pos3, ys, h2d, route, gfin)


def _pick_tile(n, pref):
    t = min(pref, n)
    while n % t:
        t //= 2
    return t


def _prep_layer(w_in, b_mlstm_gate, w_gla_gate, b_gla_gate, w_out, w_group, b_group, w_expert,
                b_expert, w_gu, w_down):
    W = N_HEADS * DH
    WK = N_HEADS * GLA_DK
    D = w_in.shape[0]
    o_gates = 4 * W
    o_gla = o_gates + 2 * N_HEADS
    o_r = o_gla + 2 * WK + 2 * W
    wm = w_in[:, 0:o_gates].astype(BF16)
    wg = w_in[:, o_gla:o_r].astype(BF16)
    ws = jnp.zeros((D, LANES), F32)
    ws = ws.at[:, 0:2 * N_HEADS].set(w_in[:, o_gates:o_gla])
    ws = ws.at[:, 2 * N_HEADS:2 * N_HEADS + GLA_RANK].set(w_in[:, o_r:o_r + GLA_RANK]).astype(BF16)
    bg = jnp.zeros((1, LANES), F32).at[0, 0:2 * N_HEADS].set(b_mlstm_gate)
    wgg = jnp.zeros((LANES, WK), F32).at[2 * N_HEADS:2 * N_HEADS + GLA_RANK, :].set(w_gla_gate)
    wr = jnp.zeros((D, LANES), F32)
    wr = wr.at[:, 0:N_GROUPS].set(w_group).at[:, EXP_LANE0:EXP_LANE0 + N_EXPERTS].set(w_expert)
    br = jnp.zeros((1, LANES), F32)
    br = br.at[0, 0:N_GROUPS].set(b_group).at[0, EXP_LANE0:EXP_LANE0 + N_EXPERTS].set(b_expert)
    return dict(wm=wm, wg=wg, ws=ws, bg=bg, wgg=wgg.astype(BF16), bgg=b_gla_gate[None, :],
                wo=w_out.astype(BF16), wr3=jnp.stack(_split3(wr)), br=br,
                wgu=w_gu.astype(BF16), wd=w_down.astype(BF16))


def _routing_tables(route, counts_row, T):
    eid = route[:, 0:2].astype(I32)
    rank = route[:, 4:6].astype(I32)
    counts = counts_row[EXP_LANE0:EXP_LANE0 + N_EXPERTS].astype(I32)
    pcounts = (counts + MOE_BLOCK - 1) // MOE_BLOCK * MOE_BLOCK
    pend = jnp.cumsum(pcounts)
    pstart = pend - pcounts
    pos = pstart[eid] + rank
    n_blocks = (T * 2 + MOE_BLOCK - 1) // MOE_BLOCK + N_EXPERTS
    blk_exp = jnp.minimum(jnp.searchsorted(pend, jnp.arange(n_blocks, dtype=I32) * MOE_BLOCK,
                                           side='right'), N_EXPERTS - 1).astype(I32)
    nused = (pend[-1] // MOE_BLOCK).astype(I32).reshape(1)
    return pos, blk_exp, nused, n_blocks


def kernel(x, norm_mix, w_in, conv_w, conv_b, b_mlstm_gate, w_gla_gate, b_gla_gate, head_norm, w_out, norm_ffn, w_group, b_group, w_expert, b_expert, w_gu, w_down, norm_final):
    B, S, D = x.shape
    T = B * S
    depth = w_in.shape[0]
    W = N_HEADS * DH
    tm = _pick_tile(T, 512)
    td = _pick_tile(T, 512)
    tc = _pick_tile(T, 256)
    lt = _pick_tile(S, 256)
    h = x.reshape(T, D)
    for l in range(depth):
        p = _prep_layer(w_in[l], b_mlstm_gate[l], w_gla_gate[l], b_gla_gate[l], w_out[l],
                        w_group[l], b_group[l], w_expert[l], b_expert[l], w_gu[l], w_down[l])
        main, gla, gates = _inproj(h, norm_mix[l][None, :], p['wm'], p['wg'], p['ws'], tm)
        ym = _mlstm(main, gates, conv_w[l], conv_b[l][None, :], p['bg'], head_norm[l][None, 0:W], B, S)
        yg = _gla(gla, gates, p['wgg'], p['bgg'], head_norm[l][None, W:2 * W], B, S, lt)
        h, xn, route, cnt = _outproj_router(ym, yg, h, p['wo'], norm_ffn[l][None, :], p['wr3'],
                                            p['br'], tm)
        pos, blk_exp, nused, n_blocks = _routing_tables(route, cnt[0], T)
        xs = _dispatch(pos.reshape(T // td, 1, 2 * td), xn,
                       jnp.zeros((n_blocks * MOE_BLOCK, D), F32), td)
        ys = _ffn(blk_exp, nused, xs, p['wgu'], p['wd'])
        h = _combine(pos.reshape(T // tc, 1, 2 * tc), ys, h, route, norm_final[None, :], tc,
                     final=(l == depth - 1))
    return h.reshape(B, S, D)
```

```python
import functools

import jax
import jax.numpy as jnp
from jax import lax
from jax.experimental import pallas as pl
from jax.experimental.pallas import tpu as pltpu

F32 = jnp.float32
BF16 = jnp.bfloat16
I32 = jnp.int32

EPS = 1e-6
LANES = 128
SUBLANES = 8
VMEM_LIMIT = 56 * 1024 * 1024

N_HEADS = 4
DH = 128
GLA_DK = 64
GLA_RANK = 16
GLA_GATE_NORM = 16.0
CONV_W = 4
MLSTM_L = 128
GLA_L = 64
GLA_R = 16
N_GROUPS = 4
EPG = 8
N_EXPERTS = N_GROUPS * EPG
MOE_BLOCK = 256
EXP_LANE0 = N_GROUPS

ROW_UNROLL = 8

NEG_INF = float("-inf")


def _cparams(sem):
    return pltpu.CompilerParams(dimension_semantics=sem, vmem_limit_bytes=VMEM_LIMIT)


def _split3(x):
    hi = x.astype(BF16)
    r1 = x - hi.astype(F32)
    mid = r1.astype(BF16)
    lo = (r1 - mid.astype(F32)).astype(BF16)
    return hi, mid, lo


def _cumsum_rows(tri_bf, x):
    hi, mid, lo = _split3(x)
    dot = functools.partial(jnp.dot, preferred_element_type=F32)
    return dot(tri_bf, hi) + dot(tri_bf, mid) + dot(tri_bf, lo)


def _sigmoid(x):
    return 1.0 / (1.0 + jnp.exp(-x))


def _log_sigmoid(x):
    return jnp.minimum(x, 0.0) - jnp.log(1.0 + jnp.exp(-jnp.abs(x)))


def _inproj_body(x_ref, g_ref, wm_ref, wg_ref, ws_ref, om_ref, og_ref, os_ref):
    x = x_ref[...]
    xn = x * lax.rsqrt(jnp.mean(x * x, axis=-1, keepdims=True) + EPS) * g_ref[...]
    xb = xn.astype(BF16)
    om_ref[...] = jnp.dot(xb, wm_ref[...], preferred_element_type=F32).astype(BF16)
    og_ref[...] = jnp.dot(xb, wg_ref[...], preferred_element_type=F32).astype(BF16)
    os_ref[...] = jnp.dot(xb, ws_ref[...], preferred_element_type=F32)


def _inproj(h2d, gain, wm, wg, ws, tm):
    T, D = h2d.shape
    nm, ng, ns = wm.shape[1], wg.shape[1], ws.shape[1]
    const = lambda i: (0, 0)
    row = lambda i: (i, 0)
    return pl.pallas_call(
        _inproj_body,
        grid=(T // tm,),
        in_specs=[pl.BlockSpec((tm, D), row), pl.BlockSpec((1, D), const),
                  pl.BlockSpec((D, nm), const), pl.BlockSpec((D, ng), const),
                  pl.BlockSpec((D, ns), const)],
        out_specs=[pl.BlockSpec((tm, nm), row), pl.BlockSpec((tm, ng), row),
                   pl.BlockSpec((tm, ns), row)],
        out_shape=[jax.ShapeDtypeStruct((T, nm), BF16), jax.ShapeDtypeStruct((T, ng), BF16),
                   jax.ShapeDtypeStruct((T, ns), F32)],
        compiler_params=_cparams(("parallel",)),
        name="inproj",
    )(h2d, gain, wm, wg, ws)


def _mlstm_body(main_ref, gate_ref, cw_ref, cb_ref, bg_ref, hn_ref, y_ref, ubuf, cst, mst):
    L = MLSTM_L
    W = N_HEADS * DH
    dot = functools.partial(jnp.dot, preferred_element_type=F32)

    @pl.when(pl.program_id(1) == 0)
    def _():
        ubuf[0:SUBLANES, :] = jnp.zeros((SUBLANES, 2 * W), F32)
        cst[...] = jnp.zeros(cst.shape, F32)
        mst[...] = jnp.zeros(mst.shape, F32)

    ubuf[SUBLANES:SUBLANES + L, :] = main_ref[:, 0:2 * W].astype(F32)
    acc = cb_ref[...]
    for i in range(CONV_W):
        off = SUBLANES - (CONV_W - 1) + i
        acc = acc + ubuf[off:off + L, :] * cw_ref[i:i + 1, :]
    ubuf[0:SUBLANES, :] = ubuf[L:L + SUBLANES, :]
    qk = acc * _sigmoid(acc)

    row = lax.broadcasted_iota(I32, (L, L), 0)
    col = lax.broadcasted_iota(I32, (L, L), 1)
    tri = col <= row
    tri_bf = jnp.where(tri, 1.0, 0.0).astype(BF16)
    lane = lax.broadcasted_iota(I32, (L, LANES), 1)
    gpre = gate_ref[...] + bg_ref[...]
    xg = jnp.where(lane < N_HEADS, gpre, _log_sigmoid(gpre))
    bcs = _cumsum_rows(tri_bf, xg)
    xg_t = xg.T
    bcs_t = bcs.T

    ones_col = jnp.where(lane == 0, 1.0, 0.0).astype(BF16)
    for h in range(N_HEADS):
        q = qk[:, h * DH:(h + 1) * DH].astype(BF16)
        k = (qk[:, W + h * DH:W + (h + 1) * DH] * (DH ** -0.5))
        v = main_ref[:, 2 * W + h * DH:2 * W + (h + 1) * DH]
        vext = jnp.concatenate([v, ones_col], axis=1)
        ig_col = xg[:, h:h + 1]
        b_col = bcs[:, N_HEADS + h:N_HEADS + h + 1]
        ig_row = xg_t[h:h + 1, :]
        b_row = bcs_t[N_HEADS + h:N_HEADS + h + 1, :]
        m_prev = mst[h][:, 0:1]
        cext = cst[h]

        logd = jnp.where(tri, b_col - b_row + ig_row, NEG_INF)
        m_inter = b_col + m_prev
        m_j = jnp.maximum(m_inter, jnp.max(logd, axis=-1, keepdims=True))
        s = lax.dot_general(q, k.astype(BF16), (((1,), (1,)), ((), ())),
                            preferred_element_type=F32) * jnp.exp(logd - m_j)
        a = jnp.exp(m_inter - m_j)
        num_ext = dot(s.astype(BF16), vext) + a * dot(q, cext.astype(BF16))
        den = num_ext[:, DH:DH + 1]
        hh = num_ext[:, 0:DH] / jnp.maximum(jnp.abs(den), jnp.exp(-m_j))

        g = b_row[:, L - 1:L]
        m_new = jnp.maximum(g + m_prev, jnp.max(g - b_row + ig_row, axis=-1, keepdims=True))
        kw = (k * jnp.exp(g - b_col + ig_col - m_new)).astype(BF16)
        decay = jnp.exp(g + m_prev - m_new)
        cst[h] = decay * cext + lax.dot_general(kw, vext, (((0,), (0,)), ((), ())),
                                                preferred_element_type=F32)
        mst[h] = jnp.broadcast_to(m_new, (1, LANES))

        hn = hh * lax.rsqrt(jnp.mean(hh * hh, axis=-1, keepdims=True) + EPS)
        o_gate = _sigmoid(main_ref[:, 3 * W + h * DH:3 * W + (h + 1) * DH].astype(F32))
        y_ref[:, h * DH:(h + 1) * DH] = (hn * hn_ref[:, h * DH:(h + 1) * DH] * o_gate).astype(BF16)


def _mlstm(main, gates, conv_w, conv_b, b_gate, head_norm, B, S):
    T = B * S
    L = MLSTM_L
    NC = S // L
    W = N_HEADS * DH
    tok = lambda b, c: (b * NC + c, 0)
    const = lambda b, c: (0, 0)
    return pl.pallas_call(
        _mlstm_body,
        grid=(B, NC),
        in_specs=[pl.BlockSpec((L, 4 * W), tok), pl.BlockSpec((L, LANES), tok),
                  pl.BlockSpec((CONV_W, 2 * W), const), pl.BlockSpec((1, 2 * W), const),
                  pl.BlockSpec((1, LANES), const), pl.BlockSpec((1, W), const)],
        out_specs=pl.BlockSpec((L, W), tok),
        out_shape=jax.ShapeDtypeStruct((T, W), BF16),
        scratch_shapes=[pltpu.VMEM((L + SUBLANES, 2 * W), F32),
                        pltpu.VMEM((N_HEADS, DH, 2 * DH), F32),
                        pltpu.VMEM((N_HEADS, 1, LANES), F32)],
        compiler_params=_cparams(("arbitrary", "arbitrary")),
        name="mlstm",
    )(main, gates, conv_w, conv_b, b_gate, head_norm)


def _gla_chunk(r0, gla_ref, gate_ref, wgg_ref, bgg_ref, hn_ref, y_ref, sst):
    L, R = GLA_L, GLA_R
    WK = N_HEADS * GLA_DK
    WV = N_HEADS * DH
    dot = functools.partial(jnp.dot, preferred_element_type=F32)
    nt = (((1,), (1,)), ((), ()))
    tn = (((0,), (0,)), ((), ()))
    rows = pl.ds(r0, L)

    row = lax.broadcasted_iota(I32, (L, L), 0)
    col = lax.broadcasted_iota(I32, (L, L), 1)
    tri_bf = jnp.where(col <= row, 1.0, 0.0).astype(BF16)
    lane = lax.broadcasted_iota(I32, (L, LANES), 1)
    rowl = lax.broadcasted_iota(I32, (L, LANES), 0)
    lane_r = lax.broadcasted_iota(I32, (R, LANES), 1)
    row_r = lax.broadcasted_iota(I32, (R, LANES), 0)
    wsel = jnp.where(lax.broadcasted_iota(I32, (LANES, 2 * LANES), 0) // GLA_DK
                     == lax.broadcasted_iota(I32, (LANES, 2 * LANES), 1) // LANES, 1.0, 0.0).astype(BF16)

    la = _log_sigmoid(dot(gate_ref[rows, :].astype(BF16), wgg_ref[...]) + bgg_ref[...]) / GLA_GATE_NORM

    for p in range(N_HEADS // 2):
        ls = slice(p * LANES, (p + 1) * LANES)
        bc = _cumsum_rows(tri_bf, la[:, ls])
        q2 = gla_ref[rows, p * LANES:(p + 1) * LANES].astype(F32) * (GLA_DK ** -0.5)
        k2 = gla_ref[rows, WK + p * LANES:WK + (p + 1) * LANES].astype(F32)
        g_last = bc[L - 1:L, :]
        q_in = q2 * jnp.exp(bc)
        kd = k2 * jnp.exp(g_last - bc)
        decay = jnp.exp(g_last)

        k_off = [None]
        q_off = [None]
        for j in range(1, L // R):
            rj = bc[j * R:j * R + 1, :]
            k_off.append((k2 * jnp.exp(jnp.where(rowl < j * R, rj - bc, NEG_INF))).astype(BF16))
            q_off.append(q2[j * R:(j + 1) * R, :] * jnp.exp(bc[j * R:(j + 1) * R, :] - rj))

        diag = []
        for j in range(L // R):
            qb = q2[j * R:(j + 1) * R, :]
            bq = bc[j * R:(j + 1) * R, :]
            terms = []
            for s in range(R):
                krow = k2[j * R + s:j * R + s + 1, :]
                brow = bc[j * R + s:j * R + s + 1, :]
                d = jnp.where(row_r >= s, bq - brow, NEG_INF)
                terms.append((qb * krow * jnp.exp(d)).astype(BF16))
            diag.append(dot(jnp.concatenate(terms, axis=0), wsel))

        for hh in range(2):
            h = 2 * p + hh
            mh = (lane // GLA_DK) == hh
            v = gla_ref[rows, 2 * WK + h * DH:2 * WK + (h + 1) * DH]
            st = sst[h]
            o = lax.dot_general(jnp.where(mh, q_in, 0.0).astype(BF16), st.astype(BF16), nt,
                                preferred_element_type=F32)
            sst[h] = st * decay + lax.dot_general(v, jnp.where(mh, kd, 0.0).astype(BF16), tn,
                                                  preferred_element_type=F32)
            a_rows = []
            for j in range(L // R):
                res = diag[j][:, hh * LANES:(hh + 1) * LANES]
                blk = jnp.zeros((R, LANES), F32)
                for s in range(R):
                    blk = jnp.where(lane_r == j * R + s, res[s * R:(s + 1) * R, :], blk)
                blk = blk[:, 0:L]
                if j > 0:
                    mq = (lane_r // GLA_DK) == hh
                    blk = blk + lax.dot_general(jnp.where(mq, q_off[j], 0.0).astype(BF16), k_off[j],
                                                nt, preferred_element_type=F32)
                a_rows.append(blk)
            a_mat = jnp.concatenate(a_rows, axis=0).astype(BF16)
            o = o + dot(a_mat, v)
            on = o * lax.rsqrt(jnp.mean(o * o, axis=-1, keepdims=True) + EPS)
            z = gla_ref[rows, 2 * WK + WV + h * DH:2 * WK + WV + (h + 1) * DH].astype(F32)
            y_ref[rows, h * DH:(h + 1) * DH] = (
                on * hn_ref[:, h * DH:(h + 1) * DH] * (z * _sigmoid(z))).astype(BF16)


def _gla_body(gla_ref, gate_ref, wgg_ref, bgg_ref, hn_ref, y_ref, sst, *, n_chunks):
    @pl.when(pl.program_id(1) == 0)
    def _():
        sst[...] = jnp.zeros(sst.shape, F32)

    def step(i, carry):
        _gla_chunk(pl.multiple_of(i * GLA_L, GLA_L), gla_ref, gate_ref, wgg_ref, bgg_ref, hn_ref,
                   y_ref, sst)
        return carry

    lax.fori_loop(0, n_chunks, step, 0)


def _gla(gla, gates, wgg, bgg, head_norm, B, S, lt):
    T = B * S
    NT = S // lt
    WK = N_HEADS * GLA_DK
    WV = N_HEADS * DH
    tok = lambda b, c: (b * NT + c, 0)
    const = lambda b, c: (0, 0)
    return pl.pallas_call(
        functools.partial(_gla_body, n_chunks=lt // GLA_L),
        grid=(B, NT),
        in_specs=[pl.BlockSpec((lt, 2 * WK + 2 * WV), tok), pl.BlockSpec((lt, LANES), tok),
                  pl.BlockSpec((LANES, WK), const), pl.BlockSpec((1, WK), const),
                  pl.BlockSpec((1, WV), const)],
        out_specs=pl.BlockSpec((lt, WV), tok),
        out_shape=jax.ShapeDtypeStruct((T, WV), BF16),
        scratch_shapes=[pltpu.VMEM((N_HEADS, DH, LANES), F32)],
        compiler_params=_cparams(("arbitrary", "arbitrary")),
        name="gla",
    )(gla, gates, wgg, bgg, head_norm)


def _outproj_router_body(ym_ref, yg_ref, h_ref, wo_ref, g_ref, wr_ref, br_ref,
                         hout_ref, xn_ref, route_ref, cnt_ref, carry):
    tm = h_ref.shape[0]
    W = ym_ref.shape[1]
    dot = functools.partial(jnp.dot, preferred_element_type=F32)

    @pl.when(pl.program_id(0) == 0)
    def _():
        carry[...] = jnp.zeros(carry.shape, F32)

    hnew = h_ref[...] + dot(ym_ref[...], wo_ref[0:W, :]) + dot(yg_ref[...], wo_ref[W:2 * W, :])
    hout_ref[...] = hnew
    xn = hnew * lax.rsqrt(jnp.mean(hnew * hnew, axis=-1, keepdims=True) + EPS) * g_ref[...]
    xn_ref[...] = xn

    xh, xm, xl = _split3(xn)
    wh, wm, wl = wr_ref[0], wr_ref[1], wr_ref[2]
    logits = (dot(xh, wh) + (dot(xh, wm) + dot(xm, wh))
              + (dot(xh, wl) + dot(xm, wm) + dot(xl, wh))) + br_ref[...]

    lane = lax.broadcasted_iota(I32, (tm, LANES), 1)
    lane_f = lane.astype(F32)
    big = float(LANES)
    gl = jnp.where(lane < N_GROUPS, logits, NEG_INF)
    gmax = jnp.max(gl, axis=-1, keepdims=True)
    gsel = jnp.min(jnp.where(gl == gmax, lane_f, big), axis=-1, keepdims=True)
    g_gate = 1.0 / jnp.sum(jnp.where(lane < N_GROUPS, jnp.exp(logits - gmax), 0.0),
                           axis=-1, keepdims=True)
    in_grp = ((lane >= EXP_LANE0) & (lane < EXP_LANE0 + N_EXPERTS)
              & (((lane - EXP_LANE0) // EPG).astype(F32) == gsel))
    el = jnp.where(in_grp, logits, NEG_INF)
    v1 = jnp.max(el, axis=-1, keepdims=True)
    i1 = jnp.min(jnp.where(el == v1, lane_f, big), axis=-1, keepdims=True)
    el2 = jnp.where(lane_f == i1, NEG_INF, el)
    v2 = jnp.max(el2, axis=-1, keepdims=True)
    i2 = jnp.min(jnp.where(el2 == v2, lane_f, big), axis=-1, keepdims=True)
    e2 = jnp.exp(v2 - v1)
    w1 = g_gate / (1.0 + e2)
    w2 = g_gate * e2 / (1.0 + e2)

    oh1 = lane_f == i1
    oh2 = lane_f == i2
    oh = jnp.where(oh1, 1.0, 0.0) + jnp.where(oh2, 1.0, 0.0)
    r_i = lax.broadcasted_iota(I32, (tm, tm), 0)
    c_i = lax.broadcasted_iota(I32, (tm, tm), 1)
    strict = jnp.where(c_i < r_i, 1.0, 0.0).astype(BF16)
    before = dot(strict, oh.astype(BF16)) + carry[0:1, :]
    rank1 = jnp.sum(jnp.where(oh1, before, 0.0), axis=-1, keepdims=True)
    rank2 = jnp.sum(jnp.where(oh2, before, 0.0), axis=-1, keepdims=True)
    total = carry[0:1, :] + jnp.sum(oh, axis=0, keepdims=True)
    carry[...] = jnp.broadcast_to(total, carry.shape)
    cnt_ref[...] = jnp.broadcast_to(total, cnt_ref.shape)

    rec = jnp.where(lane == 0, i1 - EXP_LANE0, 0.0)
    rec = jnp.where(lane == 1, i2 - EXP_LANE0, rec)
    rec = jnp.where(lane == 2, w1, rec)
    rec = jnp.where(lane == 3, w2, rec)
    rec = jnp.where(lane == 4, rank1, rec)
    rec = jnp.where(lane == 5, rank2, rec)
    route_ref[...] = rec


def _outproj_router(ym, yg, h2d, wo, gain, wr3, br, tm):
    T, D = h2d.shape
    W = ym.shape[1]
    row = lambda i: (i, 0)
    const = lambda i: (0, 0)
    return pl.pallas_call(
        _outproj_router_body,
        grid=(T // tm,),
        in_specs=[pl.BlockSpec((tm, W), row), pl.BlockSpec((tm, W), row), pl.BlockSpec((tm, D), row),
                  pl.BlockSpec((2 * W, D), const), pl.BlockSpec((1, D), const),
                  pl.BlockSpec((3, D, LANES), lambda i: (0, 0, 0)), pl.BlockSpec((1, LANES), const)],
        out_specs=[pl.BlockSpec((tm, D), row), pl.BlockSpec((tm, D), row),
                   pl.BlockSpec((tm, LANES), row), pl.BlockSpec((SUBLANES, LANES), const)],
        out_shape=[jax.ShapeDtypeStruct((T, D), F32), jax.ShapeDtypeStruct((T, D), F32),
                   jax.ShapeDtypeStruct((T, LANES), F32),
                   jax.ShapeDtypeStruct((SUBLANES, LANES), F32)],
        scratch_shapes=[pltpu.VMEM((SUBLANES, LANES), F32)],
        compiler_params=_cparams(("arbitrary",)),
        name="outproj_router",
    )(ym, yg, h2d, wo, gain, wr3, br)


def _dispatch_body(pos_ref, x_hbm, xs_in_hbm, xs_hbm, xbuf, load_sem, scat_sem, *, td, n_tiles):
    del xs_in_hbm
    i = pl.program_id(0)
    ns = xbuf.shape[0]
    slot = lax.rem(i, ns)

    def load(tile, s):
        rows = pl.ds(pl.multiple_of(tile * td, td), td)
        return pltpu.make_async_copy(x_hbm.at[rows], xbuf.at[s], load_sem.at[s])

    def row_copy(s, src_row, dst_row):
        return pltpu.make_async_copy(xbuf.at[s, pl.ds(src_row, 1)], xs_hbm.at[pl.ds(dst_row, 1)],
                                     scat_sem.at[s])

    def wait_scatter(s):
        for _ in range(2):
            pltpu.make_async_copy(xbuf.at[s], xs_hbm.at[pl.ds(0, td)], scat_sem.at[s]).wait()

    @pl.when(i == 0)
    def _():
        load(0, 0).start()

    @pl.when(i + 1 < n_tiles)
    def _():
        load(i + 1, lax.rem(i + 1, ns)).start()

    load(i, slot).wait()

    def issue(jb, carry):
        for u in range(ROW_UNROLL):
            j = jb * ROW_UNROLL + u
            row_copy(slot, j, pos_ref[0, 0, 2 * j]).start()
            row_copy(slot, j, pos_ref[0, 0, 2 * j + 1]).start()
        return carry

    lax.fori_loop(0, td // ROW_UNROLL, issue, 0)

    @pl.when(i > 0)
    def _():
        wait_scatter(lax.rem(i + ns - 1, ns))

    @pl.when(i == n_tiles - 1)
    def _():
        wait_scatter(slot)


def _dispatch(pos3, xn, xs_init, td):
    T, D = xn.shape
    n = T // td
    return pl.pallas_call(
        functools.partial(_dispatch_body, td=td, n_tiles=n),
        grid=(n,),
        in_specs=[pl.BlockSpec((1, 1, 2 * td), lambda i: (i, 0, 0), memory_space=pltpu.SMEM),
                  pl.BlockSpec(memory_space=pl.ANY), pl.BlockSpec(memory_space=pl.ANY)],
        out_specs=pl.BlockSpec(memory_space=pl.ANY),
        out_shape=jax.ShapeDtypeStruct(xs_init.shape, xs_init.dtype),
        scratch_shapes=[pltpu.VMEM((3, td, D), F32), pltpu.SemaphoreType.DMA((3,)),
                        pltpu.SemaphoreType.DMA((3,))],
        input_output_aliases={2: 0},
        compiler_params=_cparams(("arbitrary",)),
        name="dispatch",
    )(pos3, xn, xs_init)


def _ffn_body(blk_exp_ref, nused_ref, xs_ref, wgu_ref, wd_ref, ys_ref):
    del blk_exp_ref
    de = wd_ref.shape[1]
    used = pl.program_id(0) < nused_ref[0]

    @pl.when(used)
    def _():
        gu = jnp.dot(xs_ref[...].astype(BF16), wgu_ref[0], preferred_element_type=F32)
        gate, up = gu[:, 0:de], gu[:, de:2 * de]
        act = (gate * _sigmoid(gate) * up).astype(BF16)
        ys_ref[...] = jnp.dot(act, wd_ref[0], preferred_element_type=F32)

    @pl.when(jnp.logical_not(used))
    def _():
        ys_ref[...] = jnp.zeros(ys_ref.shape, F32)


def _ffn(blk_exp, nused, xs, wgu, wd):
    NR, D = xs.shape
    nb = NR // MOE_BLOCK
    de = wd.shape[1]
    grid_spec = pltpu.PrefetchScalarGridSpec(
        num_scalar_prefetch=2,
        grid=(nb,),
        in_specs=[pl.BlockSpec((MOE_BLOCK, D), lambda b, be, nu: (b, 0)),
                  pl.BlockSpec((1, D, 2 * de), lambda b, be, nu: (be[b], 0, 0)),
                  pl.BlockSpec((1, de, D), lambda b, be, nu: (be[b], 0, 0))],
        out_specs=pl.BlockSpec((MOE_BLOCK, D), lambda b, be, nu: (b, 0)),
    )
    return pl.pallas_call(
        _ffn_body,
        grid_spec=grid_spec,
        out_shape=jax.ShapeDtypeStruct((NR, D), F32),
        compiler_params=_cparams(("arbitrary",)),
        name="expert_ffn",
    )(blk_exp, nused, xs, wgu, wd)


def _combine_body(pos_ref, posn_ref, ys_hbm, h_ref, route_ref, gfin_ref, out_ref, gbuf, sem,
                  *, tc, n_tiles, final):
    i = pl.program_id(0)
    slot = lax.rem(i, 2)

    def issue_tile(p_ref, s):
        def issue(jb, carry):
            for u in range(ROW_UNROLL):
                j = jb * ROW_UNROLL + u
                for k in range(2):
                    pltpu.make_async_copy(ys_hbm.at[pl.ds(p_ref[0, 0, 2 * j + k], 1)],
                                          gbuf.at[s, k, pl.ds(j, 1)], sem.at[s]).start()
            return carry

        lax.fori_loop(0, tc // ROW_UNROLL, issue, 0)

    @pl.when(i == 0)
    def _():
        issue_tile(pos_ref, 0)

    @pl.when(i + 1 < n_tiles)
    def _():
        issue_tile(posn_ref, lax.rem(i + 1, 2))

    for k in range(2):
        pltpu.make_async_copy(ys_hbm.at[pl.ds(0, tc)], gbuf.at[slot, k], sem.at[slot]).wait()

    w1 = route_ref[:, 2:3]
    w2 = route_ref[:, 3:4]
    out = h_ref[...] + (w1 * gbuf[slot, 0] + w2 * gbuf[slot, 1])
    if final:
        out = out * lax.rsqrt(jnp.mean(out * out, axis=-1, keepdims=True) + EPS) * gfin_ref[...]
    out_ref[...] = out


def _combine(pos3, ys, h2d, route, gfin, tc, final):
    T, D = h2d.shape
    n = T // tc
    row = lambda i: (i, 0)
    smem = functools.partial(pl.BlockSpec, (1, 1, 2 * tc), memory_space=pltpu.SMEM)
    return pl.pallas_call(
        functools.partial(_combine_body, tc=tc, n_tiles=n, final=final),
        grid=(n,),
        in_specs=[smem(index_map=lambda i: (i, 0, 0)),
                  smem(index_map=lambda i: (jnp.minimum(i + 1, n - 1), 0, 0)),
                  pl.BlockSpec(memory_space=pl.ANY),
                  pl.BlockSpec((tc, D), row), pl.BlockSpec((tc, LANES), row),
                  pl.BlockSpec((1, D), lambda i: (0, 0))],
        out_specs=pl.BlockSpec((tc, D), row),
        out_shape=jax.ShapeDtypeStruct((T, D), F32),
        scratch_shapes=[pltpu.VMEM((2, 2, tc, D), F32), pltpu.SemaphoreType.DMA((2,))],
        compiler_params=_cparams(("arbitrary",)),
        name="combine",
    )(pos3, pos3, ys, h2d, route, gfin)


def _pick_tile(n, pref):
    t = min(pref, n)
    while n % t:
        t //= 2
    return t


def _prep_layer(w_in, b_mlstm_gate, w_gla_gate, b_gla_gate, w_out, w_group, b_group, w_expert,
                b_expert, w_gu, w_down):
    W = N_HEADS * DH
    WK = N_HEADS * GLA_DK
    D = w_in.shape[0]
    o_gates = 4 * W
    o_gla = o_gates + 2 * N_HEADS
    o_r = o_gla + 2 * WK + 2 * W
    wm = w_in[:, 0:o_gates].astype(BF16)
    wg = w_in[:, o_gla:o_r].astype(BF16)
    ws = jnp.zeros((D, LANES), F32)
    ws = ws.at[:, 0:2 * N_HEADS].set(w_in[:, o_gates:o_gla])
    ws = ws.at[:, 2 * N_HEADS:2 * N_HEADS + GLA_RANK].set(w_in[:, o_r:o_r + GLA_RANK]).astype(BF16)
    bg = jnp.zeros((1, LANES), F32).at[0, 0:2 * N_HEADS].set(b_mlstm_gate)
    wgg = jnp.zeros((LANES, WK), F32).at[2 * N_HEADS:2 * N_HEADS + GLA_RANK, :].set(w_gla_gate)
    wr = jnp.zeros((D, LANES), F32)
    wr = wr.at[:, 0:N_GROUPS].set(w_group).at[:, EXP_LANE0:EXP_LANE0 + N_EXPERTS].set(w_expert)
    br = jnp.zeros((1, LANES), F32)
    br = br.at[0, 0:N_GROUPS].set(b_group).at[0, EXP_LANE0:EXP_LANE0 + N_EXPERTS].set(b_expert)
    return dict(wm=wm, wg=wg, ws=ws, bg=bg, wgg=wgg.astype(BF16), bgg=b_gla_gate[None, :],
                wo=w_out.astype(BF16), wr3=jnp.stack(_split3(wr)), br=br,
                wgu=w_gu.astype(BF16), wd=w_down.astype(BF16))


def _routing_tables(route, counts_row, T):
    eid = route[:, 0:2].astype(I32)
    rank = route[:, 4:6].astype(I32)
    counts = counts_row[EXP_LANE0:EXP_LANE0 + N_EXPERTS].astype(I32)
    pcounts = (counts + MOE_BLOCK - 1) // MOE_BLOCK * MOE_BLOCK
    pend = jnp.cumsum(pcounts)
    pstart = pend - pcounts
    pos = pstart[eid] + rank
    n_blocks = (T * 2 + MOE_BLOCK - 1) // MOE_BLOCK + N_EXPERTS
    blk_start = jnp.arange(n_blocks, dtype=I32) * MOE_BLOCK
    blk_exp = jnp.minimum(jnp.sum((pend[None, :] <= blk_start[:, None]).astype(I32), axis=1),
                          N_EXPERTS - 1)
    nused = (pend[-1] // MOE_BLOCK).astype(I32).reshape(1)
    return pos, blk_exp, nused, n_blocks


def kernel(x, norm_mix, w_in, conv_w, conv_b, b_mlstm_gate, w_gla_gate, b_gla_gate, head_norm, w_out, norm_ffn, w_group, b_group, w_expert, b_expert, w_gu, w_down, norm_final):
    B, S, D = x.shape
    T = B * S
    depth = w_in.shape[0]
    W = N_HEADS * DH
    tm = _pick_tile(T, 512)
    td = _pick_tile(T, 512)
    tc = _pick_tile(T, 256)
    lt = _pick_tile(S, 256)
    h = x.reshape(T, D)
    for l in range(depth):
        p = _prep_layer(w_in[l], b_mlstm_gate[l], w_gla_gate[l], b_gla_gate[l], w_out[l],
                        w_group[l], b_group[l], w_expert[l], b_expert[l], w_gu[l], w_down[l])
        main, gla, gates = _inproj(h, norm_mix[l][None, :], p['wm'], p['wg'], p['ws'], tm)
        ym = _mlstm(main, gates, conv_w[l], conv_b[l][None, :], p['bg'], head_norm[l][None, 0:W], B, S)
        yg = _gla(gla, gates, p['wgg'], p['bgg'], head_norm[l][None, W:2 * W], B, S, lt)
        h, xn, route, cnt = _outproj_router(ym, yg, h, p['wo'], norm_ffn[l][None, :], p['wr3'],
                                            p['br'], tm)
        pos, blk_exp, nused, n_blocks = _routing_tables(route, cnt[0], T)
        xs = _dispatch(pos.reshape(T // td, 1, 2 * td), xn,
                       jnp.zeros((n_blocks * MOE_BLOCK, D), F32), td)
        ys = _ffn(blk_exp, nused, xs, p['wgu'], p['wd'])
        h = _combine(pos.reshape(T // tc, 1, 2 * tc), ys, h, route, norm_final[None, :], tc,
                     final=(l == depth - 1))
    return h.reshape(B, S, D)
```

```python
import functools

import jax
import jax.numpy as jnp
from jax import lax
from jax.experimental import pallas as pl
from jax.experimental.pallas import tpu as pltpu

F32 = jnp.float32
BF16 = jnp.bfloat16
I32 = jnp.int32

EPS = 1e-6
LANES = 128
SUBLANES = 8
VMEM_LIMIT = 56 * 1024 * 1024

N_HEADS = 4
DH = 128
GLA_DK = 64
GLA_RANK = 16
GLA_GATE_NORM = 16.0
CONV_W = 4
MLSTM_L = 128
GLA_L = 64
GLA_R = 16
N_GROUPS = 4
EPG = 8
N_EXPERTS = N_GROUPS * EPG
MOE_BLOCK = 256
EXP_LANE0 = N_GROUPS

ROW_UNROLL = 8

NEG_INF = float("-inf")


def _cparams(sem):
    return pltpu.CompilerParams(dimension_semantics=sem, vmem_limit_bytes=VMEM_LIMIT)


def _split3(x):
    hi = x.astype(BF16)
    r1 = x - hi.astype(F32)
    mid = r1.astype(BF16)
    lo = (r1 - mid.astype(F32)).astype(BF16)
    return hi, mid, lo


def _cumsum_rows(tri_bf, x):
    hi, mid, lo = _split3(x)
    dot = functools.partial(jnp.dot, preferred_element_type=F32)
    return dot(tri_bf, hi) + dot(tri_bf, mid) + dot(tri_bf, lo)


def _sigmoid(x):
    return 1.0 / (1.0 + jnp.exp(-x))


def _log_sigmoid(x):
    return jnp.minimum(x, 0.0) - jnp.log(1.0 + jnp.exp(-jnp.abs(x)))


def _inproj_body(x_ref, g_ref, wm_ref, wg_ref, ws_ref, om_ref, og_ref, os_ref):
    x = x_ref[...]
    xn = x * lax.rsqrt(jnp.mean(x * x, axis=-1, keepdims=True) + EPS) * g_ref[...]
    xb = xn.astype(BF16)
    om_ref[...] = jnp.dot(xb, wm_ref[...], preferred_element_type=F32).astype(BF16)
    og_ref[...] = jnp.dot(xb, wg_ref[...], preferred_element_type=F32).astype(BF16)
    os_ref[...] = jnp.dot(xb, ws_ref[...], preferred_element_type=F32)


def _inproj(h2d, gain, wm, wg, ws, tm):
    T, D = h2d.shape
    nm, ng, ns = wm.shape[1], wg.shape[1], ws.shape[1]
    const = lambda i: (0, 0)
    row = lambda i: (i, 0)
    return pl.pallas_call(
        _inproj_body,
        grid=(T // tm,),
        in_specs=[pl.BlockSpec((tm, D), row), pl.BlockSpec((1, D), const),
                  pl.BlockSpec((D, nm), const), pl.BlockSpec((D, ng), const),
                  pl.BlockSpec((D, ns), const)],
        out_specs=[pl.BlockSpec((tm, nm), row), pl.BlockSpec((tm, ng), row),
                   pl.BlockSpec((tm, ns), row)],
        out_shape=[jax.ShapeDtypeStruct((T, nm), BF16), jax.ShapeDtypeStruct((T, ng), BF16),
                   jax.ShapeDtypeStruct((T, ns), F32)],
        compiler_params=_cparams(("parallel",)),
        name="inproj",
    )(h2d, gain, wm, wg, ws)


def _mlstm_body(main_ref, gate_ref, cw_ref, cb_ref, bg_ref, hn_ref, y_ref, *scratch):
    L = MLSTM_L
    W = N_HEADS * DH
    TAIL = 2 * SUBLANES
    nb = main_ref.shape[0]
    ubufs, qkbufs, csts, msts = (scratch[i * nb:(i + 1) * nb] for i in range(4))
    dot = functools.partial(jnp.dot, preferred_element_type=F32)

    @pl.when(pl.program_id(1) == 0)
    def _():
        for bb in range(nb):
            ubufs[bb][0:TAIL, :] = jnp.zeros((TAIL, 2 * W), BF16)
            csts[bb][...] = jnp.zeros(csts[bb].shape, F32)
            msts[bb][...] = jnp.zeros(msts[bb].shape, F32)

    row = lax.broadcasted_iota(I32, (L, L), 0)
    col = lax.broadcasted_iota(I32, (L, L), 1)
    tri = col <= row
    tri_bf = jnp.where(tri, 1.0, 0.0).astype(BF16)
    lane = lax.broadcasted_iota(I32, (L, LANES), 1)
    sh_r = lax.broadcasted_iota(I32, ((CONV_W - 1) * L, L + TAIL), 0)
    sh_c = lax.broadcasted_iota(I32, ((CONV_W - 1) * L, L + TAIL), 1)
    shift = jnp.where(sh_c == (sh_r % L) + TAIL - (CONV_W - 1) + sh_r // L, 1.0, 0.0).astype(BF16)
    rp_r = lax.broadcasted_iota(I32, (LANES, 2 * N_HEADS * LANES), 0)
    rp_c = lax.broadcasted_iota(I32, (LANES, 2 * N_HEADS * LANES), 1)
    rep = jnp.where(rp_r == rp_c // LANES, 1.0, 0.0).astype(BF16)

    gates = []
    for bb in range(nb):
        ubuf = ubufs[bb]
        u = main_ref[bb, :, 0:2 * W]
        ubuf[TAIL:TAIL + L, :] = u
        shifted = dot(shift, ubuf[...])
        ubuf[0:TAIL, :] = ubuf[L:L + TAIL, :]
        acc = cb_ref[...] + u.astype(F32) * cw_ref[CONV_W - 1:CONV_W, :]
        for i in range(CONV_W - 1):
            acc = acc + shifted[i * L:(i + 1) * L, :] * cw_ref[i:i + 1, :]
        qk = acc * _sigmoid(acc)
        qkbufs[bb][:, 0:W] = qk[:, 0:W]
        qkbufs[bb][:, W:2 * W] = qk[:, W:2 * W] * (DH ** -0.5)

        gpre = gate_ref[bb] + bg_ref[...]
        xg = jnp.where(lane < N_HEADS, gpre, _log_sigmoid(gpre))
        gm = jnp.where(lane < N_HEADS, xg, _cumsum_rows(tri_bf, xg))
        hi, mid, lo = _split3(gm)
        gates.append((dot(hi, rep) + dot(mid, rep) + dot(lo, rep), gm.T))

    chains = [_mlstm_head(h, main_ref.at[bb], hn_ref, y_ref.at[bb], qkbufs[bb], csts[bb], msts[bb],
                          gates[bb], tri)
              for h in range(N_HEADS) for bb in range(nb)]
    for _ in range(MLSTM_STAGES):
        for c in chains:
            next(c)


MLSTM_STAGES = 3


def _mlstm_head(h, main_ref, hn_ref, y_ref, qkbuf, cst, mst, gates, tri):
    L = MLSTM_L
    W = N_HEADS * DH
    dot = functools.partial(jnp.dot, preferred_element_type=F32)
    g_rep, g_t = gates
    hs = slice(h * DH, (h + 1) * DH)

    q = qkbuf[:, hs].astype(BF16)
    k = qkbuf[:, W + h * DH:W + (h + 1) * DH]
    v = main_ref[:, 2 * W + h * DH:2 * W + (h + 1) * DH]
    vext = jnp.concatenate([v, jnp.ones((L, DH), BF16)], axis=1)
    cext = cst[h]
    s_raw = lax.dot_general(q, k.astype(BF16), (((1,), (1,)), ((), ())), preferred_element_type=F32)
    qc = dot(q, cext.astype(BF16))
    yield

    ig_rep = g_rep[:, h * LANES:(h + 1) * LANES]
    b_rep = g_rep[:, (N_HEADS + h) * LANES:(N_HEADS + h + 1) * LANES]
    ig_row = g_t[h:h + 1, :]
    b_row = g_t[N_HEADS + h:N_HEADS + h + 1, :]
    m_prev = mst[h]
    logd = jnp.where(tri, b_rep - b_row + ig_row, NEG_INF)
    m_inter = b_rep + m_prev
    m_j = jnp.maximum(m_inter, jnp.max(logd, axis=-1, keepdims=True))
    s = s_raw * jnp.exp(logd - m_j)
    a = jnp.exp(m_inter - m_j)
    sv = dot(s.astype(BF16), vext)
    g = b_rep[L - 1:L, :]
    m_new = jnp.maximum(g + m_prev, jnp.max(g - b_row + ig_row, axis=-1, keepdims=True))
    kw = (k * jnp.exp(g - b_rep + ig_rep - m_new)).astype(BF16)
    decay = jnp.exp(g + m_prev - m_new)
    upd = lax.dot_general(kw, vext, (((0,), (0,)), ((), ())), preferred_element_type=F32)
    yield

    num = sv[:, 0:DH] + a * qc[:, 0:DH]
    den = sv[:, DH:2 * DH] + a * qc[:, DH:2 * DH]
    hh = num / jnp.maximum(jnp.abs(den), jnp.exp(-m_j))
    cst[h] = jnp.concatenate([decay, decay], axis=1) * cext + upd
    mst[h] = m_new
    hn = hh * lax.rsqrt(jnp.mean(hh * hh, axis=-1, keepdims=True) + EPS)
    o_gate = _sigmoid(main_ref[:, 3 * W + h * DH:3 * W + (h + 1) * DH].astype(F32))
    y_ref[:, hs] = (hn * hn_ref[:, hs] * o_gate).astype(BF16)
    yield


def _mlstm(main, gates, conv_w, conv_b, b_gate, head_norm, B, S, nb):
    L = MLSTM_L
    W = N_HEADS * DH
    tok = lambda b, c: (b, c, 0)
    const = lambda b, c: (0, 0)
    return pl.pallas_call(
        _mlstm_body,
        grid=(B // nb, S // L),
        in_specs=[pl.BlockSpec((nb, L, 4 * W), tok), pl.BlockSpec((nb, L, LANES), tok),
                  pl.BlockSpec((CONV_W, 2 * W), const), pl.BlockSpec((1, 2 * W), const),
                  pl.BlockSpec((1, LANES), const), pl.BlockSpec((1, W), const)],
        out_specs=pl.BlockSpec((nb, L, W), tok),
        out_shape=jax.ShapeDtypeStruct((B, S, W), BF16),
        scratch_shapes=([pltpu.VMEM((L + 2 * SUBLANES, 2 * W), BF16)] * nb
                        + [pltpu.VMEM((L, 2 * W), F32)] * nb
                        + [pltpu.VMEM((N_HEADS, DH, 2 * DH), F32)] * nb
                        + [pltpu.VMEM((N_HEADS, 1, LANES), F32)] * nb),
        compiler_params=_cparams(("arbitrary", "arbitrary")),
        name="mlstm",
    )(main.reshape(B, S, 4 * W), gates.reshape(B, S, LANES), conv_w, conv_b, b_gate,
      head_norm).reshape(B * S, W)


GLA_STAGES = 4


def _gla_consts():
    L, R = GLA_L, GLA_R
    row = lax.broadcasted_iota(I32, (L, L), 0)
    col = lax.broadcasted_iota(I32, (L, L), 1)
    return dict(
        tri_bf=jnp.where(col <= row, 1.0, 0.0).astype(BF16),
        lane=lax.broadcasted_iota(I32, (L, LANES), 1),
        rowl=lax.broadcasted_iota(I32, (L, LANES), 0),
        lane_r=lax.broadcasted_iota(I32, (R, LANES), 1),
        row_r=lax.broadcasted_iota(I32, (R, LANES), 0),
        wsel=jnp.where(lax.broadcasted_iota(I32, (LANES, 2 * LANES), 0) // GLA_DK
                       == lax.broadcasted_iota(I32, (LANES, 2 * LANES), 1) // LANES,
                       1.0, 0.0).astype(BF16))


def _gla_pair(p, la, rows, c, gla_ref, hn_ref, y_ref, sst):
    L, R = GLA_L, GLA_R
    WK = N_HEADS * GLA_DK
    WV = N_HEADS * DH
    dot = functools.partial(jnp.dot, preferred_element_type=F32)
    nt = (((1,), (1,)), ((), ()))
    tn = (((0,), (0,)), ((), ()))
    lane, rowl, lane_r, row_r = c['lane'], c['rowl'], c['lane_r'], c['row_r']

    bc = _cumsum_rows(c['tri_bf'], la[:, p * LANES:(p + 1) * LANES])
    yield

    q2 = gla_ref[rows, p * LANES:(p + 1) * LANES].astype(F32) * (GLA_DK ** -0.5)
    k2 = gla_ref[rows, WK + p * LANES:WK + (p + 1) * LANES].astype(F32)
    g_last = bc[L - 1:L, :]
    q_in = q2 * jnp.exp(bc)
    kd = k2 * jnp.exp(g_last - bc)
    decay = jnp.exp(g_last)

    k_off = [None]
    q_off = [None]
    for j in range(1, L // R):
        rj = bc[j * R:j * R + 1, :]
        k_off.append((k2 * jnp.exp(jnp.where(rowl < j * R, rj - bc, NEG_INF))).astype(BF16))
        q_off.append(q2[j * R:(j + 1) * R, :] * jnp.exp(bc[j * R:(j + 1) * R, :] - rj))

    diag = []
    for j in range(L // R):
        qb = q2[j * R:(j + 1) * R, :]
        bq = bc[j * R:(j + 1) * R, :]
        terms = []
        for s in range(R):
            krow = k2[j * R + s:j * R + s + 1, :]
            brow = bc[j * R + s:j * R + s + 1, :]
            d = jnp.where(row_r >= s, bq - brow, NEG_INF)
            terms.append((qb * krow * jnp.exp(d)).astype(BF16))
        diag.append(dot(jnp.concatenate(terms, axis=0), c['wsel']))

    heads = []
    for hh in range(2):
        h = 2 * p + hh
        mh = (lane // GLA_DK) == hh
        mq = (lane_r // GLA_DK) == hh
        v = gla_ref[rows, 2 * WK + h * DH:2 * WK + (h + 1) * DH]
        st = sst[h]
        o_inter = lax.dot_general(jnp.where(mh, q_in, 0.0).astype(BF16), st.astype(BF16), nt,
                                  preferred_element_type=F32)
        upd = lax.dot_general(v, jnp.where(mh, kd, 0.0).astype(BF16), tn, preferred_element_type=F32)
        offs = [None] + [lax.dot_general(jnp.where(mq, q_off[j], 0.0).astype(BF16), k_off[j], nt,
                                         preferred_element_type=F32) for j in range(1, L // R)]
        heads.append((h, v, st, o_inter, upd, offs))
    yield

    outs = []
    for hh, (h, v, st, o_inter, upd, offs) in enumerate(heads):
        sst[h] = st * decay + upd
        a_rows = []
        for j in range(L // R):
            res = diag[j][:, hh * LANES:(hh + 1) * LANES]
            blk = jnp.zeros((R, LANES), F32)
            for s in range(R):
                blk = jnp.where(lane_r == j * R + s, res[s * R:(s + 1) * R, :], blk)
            blk = blk[:, 0:L]
            if j > 0:
                blk = blk + offs[j]
            a_rows.append(blk)
        a_mat = jnp.concatenate(a_rows, axis=0).astype(BF16)
        outs.append((h, o_inter, dot(a_mat, v)))
    yield

    for h, o_inter, o_intra in outs:
        o = o_inter + o_intra
        on = o * lax.rsqrt(jnp.mean(o * o, axis=-1, keepdims=True) + EPS)
        z = gla_ref[rows, 2 * WK + WV + h * DH:2 * WK + WV + (h + 1) * DH].astype(F32)
        y_ref[rows, h * DH:(h + 1) * DH] = (
            on * hn_ref[:, h * DH:(h + 1) * DH] * (z * _sigmoid(z))).astype(BF16)
    yield


def _gla_body(gla_ref, gate_ref, wgg_ref, bgg_ref, hn_ref, y_ref, *ssts, n_chunks):
    nb = gla_ref.shape[0]

    @pl.when(pl.program_id(1) == 0)
    def _():
        for bb in range(nb):
            ssts[bb][...] = jnp.zeros(ssts[bb].shape, F32)

    consts = _gla_consts()

    def step(i, carry):
        rows = pl.ds(pl.multiple_of(i * GLA_L, GLA_L), GLA_L)
        chains = []
        for bb in range(nb):
            la = _log_sigmoid(jnp.dot(gate_ref[bb, rows, :].astype(BF16), wgg_ref[...],
                                      preferred_element_type=F32) + bgg_ref[...]) / GLA_GATE_NORM
            chains += [_gla_pair(p, la, rows, consts, gla_ref.at[bb], hn_ref, y_ref.at[bb], ssts[bb])
                       for p in range(N_HEADS // 2)]
        for _ in range(GLA_STAGES):
            for ch in chains:
                next(ch)
        return carry

    lax.fori_loop(0, n_chunks, step, 0)


def _gla(gla, gates, wgg, bgg, head_norm, B, S, lt, nb):
    WK = N_HEADS * GLA_DK
    WV = N_HEADS * DH
    tok = lambda b, c: (b, c, 0)
    const = lambda b, c: (0, 0)
    return pl.pallas_call(
        functools.partial(_gla_body, n_chunks=lt // GLA_L),
        grid=(B // nb, S // lt),
        in_specs=[pl.BlockSpec((nb, lt, 2 * WK + 2 * WV), tok), pl.BlockSpec((nb, lt, LANES), tok),
                  pl.BlockSpec((LANES, WK), const), pl.BlockSpec((1, WK), const),
                  pl.BlockSpec((1, WV), const)],
        out_specs=pl.BlockSpec((nb, lt, WV), tok),
        out_shape=jax.ShapeDtypeStruct((B, S, WV), BF16),
        scratch_shapes=[pltpu.VMEM((N_HEADS, DH, LANES), F32)] * nb,
        compiler_params=_cparams(("arbitrary", "arbitrary")),
        name="gla",
    )(gla.reshape(B, S, 2 * WK + 2 * WV), gates.reshape(B, S, LANES), wgg, bgg,
      head_norm).reshape(B * S, WV)


def _outproj_router_body(ym_ref, yg_ref, h_ref, wo_ref, g_ref, wr_ref, br_ref,
                         hout_ref, xn_ref, route_ref, cnt_ref, carry):
    tm = h_ref.shape[0]
    W = ym_ref.shape[1]
    dot = functools.partial(jnp.dot, preferred_element_type=F32)

    @pl.when(pl.program_id(0) == 0)
    def _():
        carry[...] = jnp.zeros(carry.shape, F32)

    hnew = h_ref[...] + dot(ym_ref[...], wo_ref[0:W, :]) + dot(yg_ref[...], wo_ref[W:2 * W, :])
    hout_ref[...] = hnew
    xn = hnew * lax.rsqrt(jnp.mean(hnew * hnew, axis=-1, keepdims=True) + EPS) * g_ref[...]
    xn_ref[...] = xn

    xh, xm, xl = _split3(xn)
    wh, wm, wl = wr_ref[0], wr_ref[1], wr_ref[2]
    logits = (dot(xh, wh) + (dot(xh, wm) + dot(xm, wh))
              + (dot(xh, wl) + dot(xm, wm) + dot(xl, wh))) + br_ref[...]

    lane = lax.broadcasted_iota(I32, (tm, LANES), 1)
    lane_f = lane.astype(F32)
    big = float(LANES)
    gl = jnp.where(lane < N_GROUPS, logits, NEG_INF)
    gmax = jnp.max(gl, axis=-1, keepdims=True)
    gsel = jnp.min(jnp.where(gl == gmax, lane_f, big), axis=-1, keepdims=True)
    g_gate = 1.0 / jnp.sum(jnp.where(lane < N_GROUPS, jnp.exp(logits - gmax), 0.0),
                           axis=-1, keepdims=True)
    in_grp = ((lane >= EXP_LANE0) & (lane < EXP_LANE0 + N_EXPERTS)
              & (((lane - EXP_LANE0) // EPG).astype(F32) == gsel))
    el = jnp.where(in_grp, logits, NEG_INF)
    v1 = jnp.max(el, axis=-1, keepdims=True)
    i1 = jnp.min(jnp.where(el == v1, lane_f, big), axis=-1, keepdims=True)
    el2 = jnp.where(lane_f == i1, NEG_INF, el)
    v2 = jnp.max(el2, axis=-1, keepdims=True)
    i2 = jnp.min(jnp.where(el2 == v2, lane_f, big), axis=-1, keepdims=True)
    e2 = jnp.exp(v2 - v1)
    w1 = g_gate / (1.0 + e2)
    w2 = g_gate * e2 / (1.0 + e2)

    oh1 = lane_f == i1
    oh2 = lane_f == i2
    oh = jnp.where(oh1, 1.0, 0.0) + jnp.where(oh2, 1.0, 0.0)
    r_i = lax.broadcasted_iota(I32, (tm, tm), 0)
    c_i = lax.broadcasted_iota(I32, (tm, tm), 1)
    strict = jnp.where(c_i < r_i, 1.0, 0.0).astype(BF16)
    before = dot(strict, oh.astype(BF16)) + carry[0:1, :]
    rank1 = jnp.sum(jnp.where(oh1, before, 0.0), axis=-1, keepdims=True)
    rank2 = jnp.sum(jnp.where(oh2, before, 0.0), axis=-1, keepdims=True)
    total = carry[0:1, :] + jnp.sum(oh, axis=0, keepdims=True)
    carry[...] = jnp.broadcast_to(total, carry.shape)
    cnt_ref[...] = jnp.broadcast_to(total, cnt_ref.shape)

    rec = jnp.where(lane == 0, i1 - EXP_LANE0, 0.0)
    rec = jnp.where(lane == 1, i2 - EXP_LANE0, rec)
    rec = jnp.where(lane == 2, w1, rec)
    rec = jnp.where(lane == 3, w2, rec)
    rec = jnp.where(lane == 4, rank1, rec)
    rec = jnp.where(lane == 5, rank2, rec)
    route_ref[...] = rec


def _outproj_router(ym, yg, h2d, wo, gain, wr3, br, tm):
    T, D = h2d.shape
    W = ym.shape[1]
    row = lambda i: (i, 0)
    const = lambda i: (0, 0)
    return pl.pallas_call(
        _outproj_router_body,
        grid=(T // tm,),
        in_specs=[pl.BlockSpec((tm, W), row), pl.BlockSpec((tm, W), row), pl.BlockSpec((tm, D), row),
                  pl.BlockSpec((2 * W, D), const), pl.BlockSpec((1, D), const),
                  pl.BlockSpec((3, D, LANES), lambda i: (0, 0, 0)), pl.BlockSpec((1, LANES), const)],
        out_specs=[pl.BlockSpec((tm, D), row), pl.BlockSpec((tm, D), row),
                   pl.BlockSpec((tm, LANES), row), pl.BlockSpec((SUBLANES, LANES), const)],
        out_shape=[jax.ShapeDtypeStruct((T, D), F32), jax.ShapeDtypeStruct((T, D), F32),
                   jax.ShapeDtypeStruct((T, LANES), F32),
                   jax.ShapeDtypeStruct((SUBLANES, LANES), F32)],
        scratch_shapes=[pltpu.VMEM((SUBLANES, LANES), F32)],
        compiler_params=_cparams(("arbitrary",)),
        name="outproj_router",
    )(ym, yg, h2d, wo, gain, wr3, br)


def _dispatch_body(pos_ref, x_hbm, xs_in_hbm, xs_hbm, xbuf, load_sem, scat_sem, *, td, n_tiles):
    del xs_in_hbm
    i = pl.program_id(0)
    ns = xbuf.shape[0]
    slot = lax.rem(i, ns)

    def load(tile, s):
        rows = pl.ds(pl.multiple_of(tile * td, td), td)
        return pltpu.make_async_copy(x_hbm.at[rows], xbuf.at[s], load_sem.at[s])

    def row_copy(s, src_row, dst_row):
        return pltpu.make_async_copy(xbuf.at[s, pl.ds(src_row, 1)], xs_hbm.at[pl.ds(dst_row, 1)],
                                     scat_sem.at[s])

    def wait_scatter(s):
        for _ in range(2):
            pltpu.make_async_copy(xbuf.at[s], xs_hbm.at[pl.ds(0, td)], scat_sem.at[s]).wait()

    @pl.when(i == 0)
    def _():
        load(0, 0).start()

    @pl.when(i + 1 < n_tiles)
    def _():
        load(i + 1, lax.rem(i + 1, ns)).start()

    load(i, slot).wait()

    def issue(jb, carry):
        for u in range(ROW_UNROLL):
            j = jb * ROW_UNROLL + u
            row_copy(slot, j, pos_ref[0, 0, 2 * j]).start()
            row_copy(slot, j, pos_ref[0, 0, 2 * j + 1]).start()
        return carry

    lax.fori_loop(0, td // ROW_UNROLL, issue, 0)

    @pl.when(i > 0)
    def _():
        wait_scatter(lax.rem(i + ns - 1, ns))

    @pl.when(i == n_tiles - 1)
    def _():
        wait_scatter(slot)


def _dispatch(pos3, xn, xs_init, td):
    T, D = xn.shape
    n = T // td
    return pl.pallas_call(
        functools.partial(_dispatch_body, td=td, n_tiles=n),
        grid=(n,),
        in_specs=[pl.BlockSpec((1, 1, 2 * td), lambda i: (i, 0, 0), memory_space=pltpu.SMEM),
                  pl.BlockSpec(memory_space=pl.ANY), pl.BlockSpec(memory_space=pl.ANY)],
        out_specs=pl.BlockSpec(memory_space=pl.ANY),
        out_shape=jax.ShapeDtypeStruct(xs_init.shape, xs_init.dtype),
        scratch_shapes=[pltpu.VMEM((3, td, D), F32), pltpu.SemaphoreType.DMA((3,)),
                        pltpu.SemaphoreType.DMA((3,))],
        input_output_aliases={2: 0},
        compiler_params=_cparams(("arbitrary",)),
        name="dispatch",
    )(pos3, xn, xs_init)


def _ffn_body(blk_exp_ref, nused_ref, xs_ref, wgu_ref, wd_ref, ys_ref):
    del blk_exp_ref
    de = wd_ref.shape[1]
    used = pl.program_id(0) < nused_ref[0]

    @pl.when(used)
    def _():
        gu = jnp.dot(xs_ref[...].astype(BF16), wgu_ref[0], preferred_element_type=F32)
        gate, up = gu[:, 0:de], gu[:, de:2 * de]
        act = (gate * _sigmoid(gate) * up).astype(BF16)
        ys_ref[...] = jnp.dot(act, wd_ref[0], preferred_element_type=F32)

    @pl.when(jnp.logical_not(used))
    def _():
        ys_ref[...] = jnp.zeros(ys_ref.shape, F32)


def _ffn(blk_exp, nused, xs, wgu, wd):
    NR, D = xs.shape
    nb = NR // MOE_BLOCK
    de = wd.shape[1]
    grid_spec = pltpu.PrefetchScalarGridSpec(
        num_scalar_prefetch=2,
        grid=(nb,),
        in_specs=[pl.BlockSpec((MOE_BLOCK, D), lambda b, be, nu: (b, 0)),
                  pl.BlockSpec((1, D, 2 * de), lambda b, be, nu: (be[b], 0, 0)),
                  pl.BlockSpec((1, de, D), lambda b, be, nu: (be[b], 0, 0))],
        out_specs=pl.BlockSpec((MOE_BLOCK, D), lambda b, be, nu: (b, 0)),
    )
    return pl.pallas_call(
        _ffn_body,
        grid_spec=grid_spec,
        out_shape=jax.ShapeDtypeStruct((NR, D), F32),
        compiler_params=_cparams(("arbitrary",)),
        name="expert_ffn",
    )(blk_exp, nused, xs, wgu, wd)


def _combine_body(pos_ref, posn_ref, ys_hbm, h_ref, route_ref, gfin_ref, out_ref, gbuf, sem,
                  *, tc, n_tiles, final):
    i = pl.program_id(0)
    slot = lax.rem(i, 2)

    def issue_tile(p_ref, s):
        def issue(jb, carry):
            for u in range(ROW_UNROLL):
                j = jb * ROW_UNROLL + u
                for k in range(2):
                    pltpu.make_async_copy(ys_hbm.at[pl.ds(p_ref[0, 0, 2 * j + k], 1)],
                                          gbuf.at[s, k, pl.ds(j, 1)], sem.at[s]).start()
            return carry

        lax.fori_loop(0, tc // ROW_UNROLL, issue, 0)

    @pl.when(i == 0)
    def _():
        issue_tile(pos_ref, 0)

    @pl.when(i + 1 < n_tiles)
    def _():
        issue_tile(posn_ref, lax.rem(i + 1, 2))

    for k in range(2):
        pltpu.make_async_copy(ys_hbm.at[pl.ds(0, tc)], gbuf.at[slot, k], sem.at[slot]).wait()

    w1 = route_ref[:, 2:3]
    w2 = route_ref[:, 3:4]
    out = h_ref[...] + (w1 * gbuf[slot, 0] + w2 * gbuf[slot, 1])
    if final:
        out = out * lax.rsqrt(jnp.mean(out * out, axis=-1, keepdims=True) + EPS) * gfin_ref[...]
    out_ref[...] = out


def _combine(pos3, ys, h2d, route, gfin, tc, final):
    T, D = h2d.shape
    n = T // tc
    row = lambda i: (i, 0)
    smem = functools.partial(pl.BlockSpec, (1, 1, 2 * tc), memory_space=pltpu.SMEM)
    return pl.pallas_call(
        functools.partial(_combine_body, tc=tc, n_tiles=n, final=final),
        grid=(n,),
        in_specs=[smem(index_map=lambda i: (i, 0, 0)),
                  smem(index_map=lambda i: (jnp.minimum(i + 1, n - 1), 0, 0)),
                  pl.BlockSpec(memory_space=pl.ANY),
                  pl.BlockSpec((tc, D), row), pl.BlockSpec((tc, LANES), row),
                  pl.BlockSpec((1, D), lambda i: (0, 0))],
        out_specs=pl.BlockSpec((tc, D), row),
        out_shape=jax.ShapeDtypeStruct((T, D), F32),
        scratch_shapes=[pltpu.VMEM((2, 2, tc, D), F32), pltpu.SemaphoreType.DMA((2,))],
        compiler_params=_cparams(("arbitrary",)),
        name="combine",
    )(pos3, pos3, ys, h2d, route, gfin)


def _pick_tile(n, pref):
    t = min(pref, n)
    while n % t:
        t //= 2
    return t


def _prep_layer(w_in, b_mlstm_gate, w_gla_gate, b_gla_gate, w_out, w_group, b_group, w_expert,
                b_expert, w_gu, w_down):
    W = N_HEADS * DH
    WK = N_HEADS * GLA_DK
    D = w_in.shape[0]
    o_gates = 4 * W
    o_gla = o_gates + 2 * N_HEADS
    o_r = o_gla + 2 * WK + 2 * W
    wm = w_in[:, 0:o_gates].astype(BF16)
    wg = w_in[:, o_gla:o_r].astype(BF16)
    ws = jnp.zeros((D, LANES), F32)
    ws = ws.at[:, 0:2 * N_HEADS].set(w_in[:, o_gates:o_gla])
    ws = ws.at[:, 2 * N_HEADS:2 * N_HEADS + GLA_RANK].set(w_in[:, o_r:o_r + GLA_RANK]).astype(BF16)
    bg = jnp.zeros((1, LANES), F32).at[0, 0:2 * N_HEADS].set(b_mlstm_gate)
    wgg = jnp.zeros((LANES, WK), F32).at[2 * N_HEADS:2 * N_HEADS + GLA_RANK, :].set(w_gla_gate)
    wr = jnp.zeros((D, LANES), F32)
    wr = wr.at[:, 0:N_GROUPS].set(w_group).at[:, EXP_LANE0:EXP_LANE0 + N_EXPERTS].set(w_expert)
    br = jnp.zeros((1, LANES), F32)
    br = br.at[0, 0:N_GROUPS].set(b_group).at[0, EXP_LANE0:EXP_LANE0 + N_EXPERTS].set(b_expert)
    return dict(wm=wm, wg=wg, ws=ws, bg=bg, wgg=wgg.astype(BF16), bgg=b_gla_gate[None, :],
                wo=w_out.astype(BF16), wr3=jnp.stack(_split3(wr)), br=br,
                wgu=w_gu.astype(BF16), wd=w_down.astype(BF16))


def _routing_tables(route, counts_row, T):
    eid = route[:, 0:2].astype(I32)
    rank = route[:, 4:6].astype(I32)
    counts = counts_row[EXP_LANE0:EXP_LANE0 + N_EXPERTS].astype(I32)
    pcounts = (counts + MOE_BLOCK - 1) // MOE_BLOCK * MOE_BLOCK
    pend = jnp.cumsum(pcounts)
    pstart = pend - pcounts
    onehot = eid[:, :, None] == jnp.arange(N_EXPERTS, dtype=I32)[None, None, :]
    pos = jnp.sum(jnp.where(onehot, pstart[None, None, :], 0), axis=-1) + rank
    n_blocks = (T * 2 + MOE_BLOCK - 1) // MOE_BLOCK + N_EXPERTS
    blk_start = jnp.arange(n_blocks, dtype=I32) * MOE_BLOCK
    blk_exp = jnp.minimum(jnp.sum((pend[None, :] <= blk_start[:, None]).astype(I32), axis=1),
                          N_EXPERTS - 1)
    nused = (pend[-1] // MOE_BLOCK).astype(I32).reshape(1)
    return pos, blk_exp, nused, n_blocks


def kernel(x, norm_mix, w_in, conv_w, conv_b, b_mlstm_gate, w_gla_gate, b_gla_gate, head_norm, w_out, norm_ffn, w_group, b_group, w_expert, b_expert, w_gu, w_down, norm_final):
    B, S, D = x.shape
    T = B * S
    depth = w_in.shape[0]
    W = N_HEADS * DH
    tm = _pick_tile(T, 512)
    td = _pick_tile(T, 512)
    tc = _pick_tile(T, 256)
    lt = _pick_tile(S, 256)
    nb = _pick_tile(B, 2)
    h = x.reshape(T, D)
    for l in range(depth):
        p = _prep_layer(w_in[l], b_mlstm_gate[l], w_gla_gate[l], b_gla_gate[l], w_out[l],
                        w_group[l], b_group[l], w_expert[l], b_expert[l], w_gu[l], w_down[l])
        main, gla, gates = _inproj(h, norm_mix[l][None, :], p['wm'], p['wg'], p['ws'], tm)
        ym = _mlstm(main, gates, conv_w[l], conv_b[l][None, :], p['bg'], head_norm[l][None, 0:W], B, S,
                    nb)
        yg = _gla(gla, gates, p['wgg'], p['bgg'], head_norm[l][None, W:2 * W], B, S, lt, nb)
        h, xn, route, cnt = _outproj_router(ym, yg, h, p['wo'], norm_ffn[l][None, :], p['wr3'],
                                            p['br'], tm)
        pos, blk_exp, nused, n_blocks = _routing_tables(route, cnt[0], T)
        xs = _dispatch(pos.reshape(T // td, 1, 2 * td), xn,
                       jnp.zeros((n_blocks * MOE_BLOCK, D), F32), td)
        ys = _ffn(blk_exp, nused, xs, p['wgu'], p['wd'])
        h = _combine(pos.reshape(T // tc, 1, 2 * tc), ys, h, route, norm_final[None, :], tc,
                     final=(l == depth - 1))
    return h.reshape(B, S, D)
```

```python
import functools

import jax
import jax.numpy as jnp
from jax import lax
from jax.experimental import pallas as pl
from jax.experimental.pallas import tpu as pltpu

F32 = jnp.float32
BF16 = jnp.bfloat16
I32 = jnp.int32

EPS = 1e-6
LANES = 128
SUBLANES = 8
VMEM_LIMIT = 56 * 1024 * 1024

N_HEADS = 4
DH = 128
GLA_DK = 64
GLA_RANK = 16
GLA_GATE_NORM = 16.0
CONV_W = 4
MLSTM_L = 128
GLA_L = 64
GLA_R = 16
N_GROUPS = 4
EPG = 8
N_EXPERTS = N_GROUPS * EPG
MOE_BLOCK = 512
EXP_LANE0 = N_GROUPS

NEG_INF = float("-inf")


def _cparams(sem):
    return pltpu.CompilerParams(dimension_semantics=sem, vmem_limit_bytes=VMEM_LIMIT)


def _split3(x):
    hi = x.astype(BF16)
    r1 = x - hi.astype(F32)
    mid = r1.astype(BF16)
    lo = (r1 - mid.astype(F32)).astype(BF16)
    return hi, mid, lo


def _cumsum_rows(tri_bf, x):
    hi, mid, lo = _split3(x)
    dot = functools.partial(jnp.dot, preferred_element_type=F32)
    return dot(tri_bf, hi) + dot(tri_bf, mid) + dot(tri_bf, lo)


def _sigmoid(x):
    return 1.0 / (1.0 + jnp.exp(-x))


def _log_sigmoid(x):
    return jnp.minimum(x, 0.0) - jnp.log(1.0 + jnp.exp(-jnp.abs(x)))


def _inproj_body(x_ref, g_ref, wm_ref, wg_ref, ws_ref, om_ref, og_ref, os_ref):
    x = x_ref[...]
    xn = x * lax.rsqrt(jnp.mean(x * x, axis=-1, keepdims=True) + EPS) * g_ref[...]
    xb = xn.astype(BF16)
    om_ref[...] = jnp.dot(xb, wm_ref[...], preferred_element_type=F32).astype(BF16)
    og_ref[...] = jnp.dot(xb, wg_ref[...], preferred_element_type=F32).astype(BF16)
    os_ref[...] = jnp.dot(xb, ws_ref[...], preferred_element_type=F32)


def _inproj(h2d, gain, wm, wg, ws, tm):
    T, D = h2d.shape
    nm, ng, ns = wm.shape[1], wg.shape[1], ws.shape[1]
    const = lambda i: (0, 0)
    row = lambda i: (i, 0)
    return pl.pallas_call(
        _inproj_body,
        grid=(T // tm,),
        in_specs=[pl.BlockSpec((tm, D), row), pl.BlockSpec((1, D), const),
                  pl.BlockSpec((D, nm), const), pl.BlockSpec((D, ng), const),
                  pl.BlockSpec((D, ns), const)],
        out_specs=[pl.BlockSpec((tm, nm), row), pl.BlockSpec((tm, ng), row),
                   pl.BlockSpec((tm, ns), row)],
        out_shape=[jax.ShapeDtypeStruct((T, nm), BF16), jax.ShapeDtypeStruct((T, ng), BF16),
                   jax.ShapeDtypeStruct((T, ns), F32)],
        compiler_params=_cparams(("parallel",)),
        name="inproj",
    )(h2d, gain, wm, wg, ws)


def _mlstm_body(main_ref, gate_ref, cw_ref, cb_ref, bg_ref, hn_ref, y_ref, *scratch):
    L = MLSTM_L
    W = N_HEADS * DH
    TAIL = 2 * SUBLANES
    nb = main_ref.shape[0]
    ubufs, qkbufs, csts, msts = (scratch[i * nb:(i + 1) * nb] for i in range(4))
    dot = functools.partial(jnp.dot, preferred_element_type=F32)

    @pl.when(pl.program_id(1) == 0)
    def _():
        for bb in range(nb):
            ubufs[bb][0:TAIL, :] = jnp.zeros((TAIL, 2 * W), BF16)
            csts[bb][...] = jnp.zeros(csts[bb].shape, F32)
            msts[bb][...] = jnp.zeros(msts[bb].shape, F32)

    row = lax.broadcasted_iota(I32, (L, L), 0)
    col = lax.broadcasted_iota(I32, (L, L), 1)
    tri = col <= row
    tri_bf = jnp.where(tri, 1.0, 0.0).astype(BF16)
    lane = lax.broadcasted_iota(I32, (L, LANES), 1)
    sh_r = lax.broadcasted_iota(I32, ((CONV_W - 1) * L, L + TAIL), 0)
    sh_c = lax.broadcasted_iota(I32, ((CONV_W - 1) * L, L + TAIL), 1)
    shift = jnp.where(sh_c == (sh_r % L) + TAIL - (CONV_W - 1) + sh_r // L, 1.0, 0.0).astype(BF16)
    rp_r = lax.broadcasted_iota(I32, (LANES, 2 * N_HEADS * LANES), 0)
    rp_c = lax.broadcasted_iota(I32, (LANES, 2 * N_HEADS * LANES), 1)
    rep = jnp.where(rp_r == rp_c // LANES, 1.0, 0.0).astype(BF16)

    gates = []
    for bb in range(nb):
        ubuf = ubufs[bb]
        u = main_ref[bb, :, 0:2 * W]
        ubuf[TAIL:TAIL + L, :] = u
        shifted = dot(shift, ubuf[...])
        ubuf[0:TAIL, :] = ubuf[L:L + TAIL, :]
        acc = cb_ref[...] + u.astype(F32) * cw_ref[CONV_W - 1:CONV_W, :]
        for i in range(CONV_W - 1):
            acc = acc + shifted[i * L:(i + 1) * L, :] * cw_ref[i:i + 1, :]
        qk = acc * _sigmoid(acc)
        qkbufs[bb][:, 0:W] = qk[:, 0:W]
        qkbufs[bb][:, W:2 * W] = qk[:, W:2 * W] * (DH ** -0.5)

        gpre = gate_ref[bb] + bg_ref[...]
        xg = jnp.where(lane < N_HEADS, gpre, _log_sigmoid(gpre))
        gm = jnp.where(lane < N_HEADS, xg, _cumsum_rows(tri_bf, xg))
        hi, mid, lo = _split3(gm)
        gates.append((dot(hi, rep) + dot(mid, rep) + dot(lo, rep), gm.T))

    chains = [_mlstm_head(h, main_ref.at[bb], hn_ref, y_ref.at[bb], qkbufs[bb], csts[bb], msts[bb],
                          gates[bb], tri)
              for h in range(N_HEADS) for bb in range(nb)]
    for _ in range(MLSTM_STAGES):
        for c in chains:
            next(c)


MLSTM_STAGES = 3


def _mlstm_head(h, main_ref, hn_ref, y_ref, qkbuf, cst, mst, gates, tri):
    L = MLSTM_L
    W = N_HEADS * DH
    dot = functools.partial(jnp.dot, preferred_element_type=F32)
    g_rep, g_t = gates
    hs = slice(h * DH, (h + 1) * DH)

    q = qkbuf[:, hs].astype(BF16)
    k = qkbuf[:, W + h * DH:W + (h + 1) * DH]
    v = main_ref[:, 2 * W + h * DH:2 * W + (h + 1) * DH]
    vext = jnp.concatenate([v, jnp.ones((L, DH), BF16)], axis=1)
    cext = cst[h]
    s_raw = lax.dot_general(q, k.astype(BF16), (((1,), (1,)), ((), ())), preferred_element_type=F32)
    qc = dot(q, cext.astype(BF16))
    yield

    ig_rep = g_rep[:, h * LANES:(h + 1) * LANES]
    b_rep = g_rep[:, (N_HEADS + h) * LANES:(N_HEADS + h + 1) * LANES]
    ig_row = g_t[h:h + 1, :]
    b_row = g_t[N_HEADS + h:N_HEADS + h + 1, :]
    m_prev = mst[h]
    logd = jnp.where(tri, b_rep - b_row + ig_row, NEG_INF)
    m_inter = b_rep + m_prev
    m_j = jnp.maximum(m_inter, jnp.max(logd, axis=-1, keepdims=True))
    s = s_raw * jnp.exp(logd - m_j)
    a = jnp.exp(m_inter - m_j)
    sv = dot(s.astype(BF16), vext)
    g = b_rep[L - 1:L, :]
    m_new = jnp.maximum(g + m_prev, jnp.max(g - b_row + ig_row, axis=-1, keepdims=True))
    kw = (k * jnp.exp(g - b_rep + ig_rep - m_new)).astype(BF16)
    decay = jnp.exp(g + m_prev - m_new)
    upd = lax.dot_general(kw, vext, (((0,), (0,)), ((), ())), preferred_element_type=F32)
    yield

    num = sv[:, 0:DH] + a * qc[:, 0:DH]
    den = sv[:, DH:2 * DH] + a * qc[:, DH:2 * DH]
    hh = num / jnp.maximum(jnp.abs(den), jnp.exp(-m_j))
    cst[h] = jnp.concatenate([decay, decay], axis=1) * cext + upd
    mst[h] = m_new
    hn = hh * lax.rsqrt(jnp.mean(hh * hh, axis=-1, keepdims=True) + EPS)
    o_gate = _sigmoid(main_ref[:, 3 * W + h * DH:3 * W + (h + 1) * DH].astype(F32))
    y_ref[:, hs] = (hn * hn_ref[:, hs] * o_gate).astype(BF16)
    yield


def _mlstm(main, gates, conv_w, conv_b, b_gate, head_norm, B, S, nb):
    L = MLSTM_L
    W = N_HEADS * DH
    tok = lambda b, c: (b, c, 0)
    const = lambda b, c: (0, 0)
    return pl.pallas_call(
        _mlstm_body,
        grid=(B // nb, S // L),
        in_specs=[pl.BlockSpec((nb, L, 4 * W), tok), pl.BlockSpec((nb, L, LANES), tok),
                  pl.BlockSpec((CONV_W, 2 * W), const), pl.BlockSpec((1, 2 * W), const),
                  pl.BlockSpec((1, LANES), const), pl.BlockSpec((1, W), const)],
        out_specs=pl.BlockSpec((nb, L, W), tok),
        out_shape=jax.ShapeDtypeStruct((B, S, W), BF16),
        scratch_shapes=([pltpu.VMEM((L + 2 * SUBLANES, 2 * W), BF16)] * nb
                        + [pltpu.VMEM((L, 2 * W), F32)] * nb
                        + [pltpu.VMEM((N_HEADS, DH, 2 * DH), F32)] * nb
                        + [pltpu.VMEM((N_HEADS, 1, LANES), F32)] * nb),
        compiler_params=_cparams(("arbitrary", "arbitrary")),
        name="mlstm",
    )(main.reshape(B, S, 4 * W), gates.reshape(B, S, LANES), conv_w, conv_b, b_gate,
      head_norm).reshape(B * S, W)


GLA_STAGES = 4


def _gla_consts():
    L, R = GLA_L, GLA_R
    row = lax.broadcasted_iota(I32, (L, L), 0)
    col = lax.broadcasted_iota(I32, (L, L), 1)
    return dict(
        tri_bf=jnp.where(col <= row, 1.0, 0.0).astype(BF16),
        lane=lax.broadcasted_iota(I32, (L, LANES), 1),
        rowl=lax.broadcasted_iota(I32, (L, LANES), 0),
        lane_r=lax.broadcasted_iota(I32, (R, LANES), 1),
        row_r=lax.broadcasted_iota(I32, (R, LANES), 0),
        wsel=jnp.where(lax.broadcasted_iota(I32, (LANES, 2 * LANES), 0) // GLA_DK
                       == lax.broadcasted_iota(I32, (LANES, 2 * LANES), 1) // LANES,
                       1.0, 0.0).astype(BF16))


def _gla_pair(p, la, rows, c, gla_ref, hn_ref, y_ref, sst):
    L, R = GLA_L, GLA_R
    WK = N_HEADS * GLA_DK
    WV = N_HEADS * DH
    dot = functools.partial(jnp.dot, preferred_element_type=F32)
    nt = (((1,), (1,)), ((), ()))
    tn = (((0,), (0,)), ((), ()))
    lane, rowl, lane_r, row_r = c['lane'], c['rowl'], c['lane_r'], c['row_r']

    bc = _cumsum_rows(c['tri_bf'], la[:, p * LANES:(p + 1) * LANES])
    yield

    q2 = gla_ref[rows, p * LANES:(p + 1) * LANES].astype(F32) * (GLA_DK ** -0.5)
    k2 = gla_ref[rows, WK + p * LANES:WK + (p + 1) * LANES].astype(F32)
    g_last = bc[L - 1:L, :]
    q_in = q2 * jnp.exp(bc)
    kd = k2 * jnp.exp(g_last - bc)
    decay = jnp.exp(g_last)

    k_off = [None]
    q_off = [None]
    for j in range(1, L // R):
        rj = bc[j * R:j * R + 1, :]
        k_off.append((k2 * jnp.exp(jnp.where(rowl < j * R, rj - bc, NEG_INF))).astype(BF16))
        q_off.append(q2[j * R:(j + 1) * R, :] * jnp.exp(bc[j * R:(j + 1) * R, :] - rj))

    diag = []
    for j in range(L // R):
        qb = q2[j * R:(j + 1) * R, :]
        bq = bc[j * R:(j + 1) * R, :]
        terms = []
        for s in range(R):
            krow = k2[j * R + s:j * R + s + 1, :]
            brow = bc[j * R + s:j * R + s + 1, :]
            d = jnp.where(row_r >= s, bq - brow, NEG_INF)
            terms.append((qb * krow * jnp.exp(d)).astype(BF16))
        diag.append(dot(jnp.concatenate(terms, axis=0), c['wsel']))

    heads = []
    for hh in range(2):
        h = 2 * p + hh
        mh = (lane // GLA_DK) == hh
        mq = (lane_r // GLA_DK) == hh
        v = gla_ref[rows, 2 * WK + h * DH:2 * WK + (h + 1) * DH]
        st = sst[h]
        o_inter = lax.dot_general(jnp.where(mh, q_in, 0.0).astype(BF16), st.astype(BF16), nt,
                                  preferred_element_type=F32)
        upd = lax.dot_general(v, jnp.where(mh, kd, 0.0).astype(BF16), tn, preferred_element_type=F32)
        offs = [None] + [lax.dot_general(jnp.where(mq, q_off[j], 0.0).astype(BF16), k_off[j], nt,
                                         preferred_element_type=F32) for j in range(1, L // R)]
        heads.append((h, v, st, o_inter, upd, offs))
    yield

    outs = []
    for hh, (h, v, st, o_inter, upd, offs) in enumerate(heads):
        sst[h] = st * decay + upd
        a_rows = []
        for j in range(L // R):
            res = diag[j][:, hh * LANES:(hh + 1) * LANES]
            blk = jnp.zeros((R, LANES), F32)
            for s in range(R):
                blk = jnp.where(lane_r == j * R + s, res[s * R:(s + 1) * R, :], blk)
            blk = blk[:, 0:L]
            if j > 0:
                blk = blk + offs[j]
            a_rows.append(blk)
        a_mat = jnp.concatenate(a_rows, axis=0).astype(BF16)
        outs.append((h, o_inter, dot(a_mat, v)))
    yield

    for h, o_inter, o_intra in outs:
        o = o_inter + o_intra
        on = o * lax.rsqrt(jnp.mean(o * o, axis=-1, keepdims=True) + EPS)
        z = gla_ref[rows, 2 * WK + WV + h * DH:2 * WK + WV + (h + 1) * DH].astype(F32)
        y_ref[rows, h * DH:(h + 1) * DH] = (
            on * hn_ref[:, h * DH:(h + 1) * DH] * (z * _sigmoid(z))).astype(BF16)
    yield


def _gla_body(gla_ref, gate_ref, wgg_ref, bgg_ref, hn_ref, y_ref, *ssts, n_chunks):
    nb = gla_ref.shape[0]

    @pl.when(pl.program_id(1) == 0)
    def _():
        for bb in range(nb):
            ssts[bb][...] = jnp.zeros(ssts[bb].shape, F32)

    consts = _gla_consts()

    def step(i, carry):
        rows = pl.ds(pl.multiple_of(i * GLA_L, GLA_L), GLA_L)
        chains = []
        for bb in range(nb):
            la = _log_sigmoid(jnp.dot(gate_ref[bb, rows, :].astype(BF16), wgg_ref[...],
                                      preferred_element_type=F32) + bgg_ref[...]) / GLA_GATE_NORM
            chains += [_gla_pair(p, la, rows, consts, gla_ref.at[bb], hn_ref, y_ref.at[bb], ssts[bb])
                       for p in range(N_HEADS // 2)]
        for _ in range(GLA_STAGES):
            for ch in chains:
                next(ch)
        return carry

    lax.fori_loop(0, n_chunks, step, 0)


def _gla(gla, gates, wgg, bgg, head_norm, B, S, lt, nb):
    WK = N_HEADS * GLA_DK
    WV = N_HEADS * DH
    tok = lambda b, c: (b, c, 0)
    const = lambda b, c: (0, 0)
    return pl.pallas_call(
        functools.partial(_gla_body, n_chunks=lt // GLA_L),
        grid=(B // nb, S // lt),
        in_specs=[pl.BlockSpec((nb, lt, 2 * WK + 2 * WV), tok), pl.BlockSpec((nb, lt, LANES), tok),
                  pl.BlockSpec((LANES, WK), const), pl.BlockSpec((1, WK), const),
                  pl.BlockSpec((1, WV), const)],
        out_specs=pl.BlockSpec((nb, lt, WV), tok),
        out_shape=jax.ShapeDtypeStruct((B, S, WV), BF16),
        scratch_shapes=[pltpu.VMEM((N_HEADS, DH, LANES), F32)] * nb,
        compiler_params=_cparams(("arbitrary", "arbitrary")),
        name="gla",
    )(gla.reshape(B, S, 2 * WK + 2 * WV), gates.reshape(B, S, LANES), wgg, bgg,
      head_norm).reshape(B * S, WV)


def _outproj_router_body(ym_ref, yg_ref, h_ref, wo_ref, g_ref, wr_ref, br_ref,
                         hout_ref, xn_ref, route_ref, cnt_ref, carry):
    tm = h_ref.shape[0]
    W = ym_ref.shape[1]
    dot = functools.partial(jnp.dot, preferred_element_type=F32)

    @pl.when(pl.program_id(0) == 0)
    def _():
        carry[...] = jnp.zeros(carry.shape, F32)

    hnew = h_ref[...] + dot(ym_ref[...], wo_ref[0:W, :]) + dot(yg_ref[...], wo_ref[W:2 * W, :])
    hout_ref[...] = hnew
    xn = hnew * lax.rsqrt(jnp.mean(hnew * hnew, axis=-1, keepdims=True) + EPS) * g_ref[...]
    xn_ref[...] = xn

    xh = xn.astype(BF16)
    xm = (xn - xh.astype(F32)).astype(BF16)
    l2 = dot(xh, wr_ref[...])
    logits = l2[:, 0:LANES] + (l2[:, LANES:2 * LANES] + dot(xm, wr_ref[:, 0:LANES])) + br_ref[...]

    lane = lax.broadcasted_iota(I32, (tm, LANES), 1)
    lane_f = lane.astype(F32)
    big = float(LANES)
    gl = jnp.where(lane < N_GROUPS, logits, NEG_INF)
    gmax = jnp.max(gl, axis=-1, keepdims=True)
    gsel = jnp.min(jnp.where(gl == gmax, lane_f, big), axis=-1, keepdims=True)
    g_gate = 1.0 / jnp.sum(jnp.where(lane < N_GROUPS, jnp.exp(logits - gmax), 0.0),
                           axis=-1, keepdims=True)
    in_grp = ((lane >= EXP_LANE0) & (lane < EXP_LANE0 + N_EXPERTS)
              & (((lane - EXP_LANE0) // EPG).astype(F32) == gsel))
    el = jnp.where(in_grp, logits, NEG_INF)
    v1 = jnp.max(el, axis=-1, keepdims=True)
    i1 = jnp.min(jnp.where(el == v1, lane_f, big), axis=-1, keepdims=True)
    el2 = jnp.where(lane_f == i1, NEG_INF, el)
    v2 = jnp.max(el2, axis=-1, keepdims=True)
    i2 = jnp.min(jnp.where(el2 == v2, lane_f, big), axis=-1, keepdims=True)
    e2 = jnp.exp(v2 - v1)
    w1 = g_gate / (1.0 + e2)
    w2 = g_gate * e2 / (1.0 + e2)

    oh1 = lane_f == i1
    oh2 = lane_f == i2
    oh = jnp.where(oh1, 1.0, 0.0) + jnp.where(oh2, 1.0, 0.0)
    r_i = lax.broadcasted_iota(I32, (tm, tm), 0)
    c_i = lax.broadcasted_iota(I32, (tm, tm), 1)
    strict = jnp.where(c_i < r_i, 1.0, 0.0).astype(BF16)
    before = dot(strict, oh.astype(BF16)) + carry[0:1, :]
    rank1 = jnp.sum(jnp.where(oh1, before, 0.0), axis=-1, keepdims=True)
    rank2 = jnp.sum(jnp.where(oh2, before, 0.0), axis=-1, keepdims=True)
    total = carry[0:1, :] + jnp.sum(oh, axis=0, keepdims=True)
    carry[...] = jnp.broadcast_to(total, carry.shape)
    cnt_ref[...] = jnp.broadcast_to(total, cnt_ref.shape)

    rec = jnp.where(lane == 0, i1 - EXP_LANE0, 0.0)
    rec = jnp.where(lane == 1, i2 - EXP_LANE0, rec)
    rec = jnp.where(lane == 2, w1, rec)
    rec = jnp.where(lane == 3, w2, rec)
    rec = jnp.where(lane == 4, rank1, rec)
    rec = jnp.where(lane == 5, rank2, rec)
    route_ref[...] = rec


def _outproj_router(ym, yg, h2d, wo, gain, wr3, br, tm):
    T, D = h2d.shape
    W = ym.shape[1]
    row = lambda i: (i, 0)
    const = lambda i: (0, 0)
    return pl.pallas_call(
        _outproj_router_body,
        grid=(T // tm,),
        in_specs=[pl.BlockSpec((tm, W), row), pl.BlockSpec((tm, W), row), pl.BlockSpec((tm, D), row),
                  pl.BlockSpec((2 * W, D), const), pl.BlockSpec((1, D), const),
                  pl.BlockSpec((D, 2 * LANES), const), pl.BlockSpec((1, LANES), const)],
        out_specs=[pl.BlockSpec((tm, D), row), pl.BlockSpec((tm, D), row),
                   pl.BlockSpec((tm, LANES), row), pl.BlockSpec((SUBLANES, LANES), const)],
        out_shape=[jax.ShapeDtypeStruct((T, D), F32), jax.ShapeDtypeStruct((T, D), F32),
                   jax.ShapeDtypeStruct((T, LANES), F32),
                   jax.ShapeDtypeStruct((SUBLANES, LANES), F32)],
        scratch_shapes=[pltpu.VMEM((SUBLANES, LANES), F32)],
        compiler_params=_cparams(("arbitrary",)),
        name="outproj_router",
    )(ym, yg, h2d, wo, gain, wr3, br)


def _dispatch_body(pos_ref, pad_ref, x_hbm, xs_hbm, xbuf, zbuf, load_sem, scat_sem, pad_sem,
                   *, td, n_tiles):
    i = pl.program_id(0)
    ns = xbuf.shape[0]
    tg = td // SUBLANES
    slot = lax.rem(i, ns)

    def load(tile, s):
        groups = pl.ds(pl.multiple_of(tile * tg, tg), tg)
        return pltpu.make_async_copy(x_hbm.at[groups], xbuf.at[s], load_sem.at[s])

    def wait_scatter(s):
        for _ in range(2):
            pltpu.make_async_copy(xbuf.at[s], xs_hbm.at[pl.ds(0, tg)], scat_sem.at[s]).wait()

    def for_each_pad_copy(fn):
        def per_expert(e, carry):
            first = pad_ref[0, 0, e]

            def per_row(r, c):
                row = first + r
                fn(pltpu.make_async_copy(
                    zbuf.at[0, pl.ds(0, 1)],
                    xs_hbm.at[row // SUBLANES, pl.ds(lax.rem(row, SUBLANES), 1)], pad_sem))
                return c

            return lax.fori_loop(0, pad_ref[0, 0, N_EXPERTS + e], per_row, carry)

        lax.fori_loop(0, N_EXPERTS, per_expert, 0)
        bg = zbuf.shape[0]

        def per_block(bk, carry):
            fn(pltpu.make_async_copy(zbuf, xs_hbm.at[pl.ds(pl.multiple_of(bk * bg, bg), bg)], pad_sem))
            return carry

        lax.fori_loop(pad_ref[0, 0, 2 * N_EXPERTS], xs_hbm.shape[0] // bg, per_block, 0)

    @pl.when(i == 0)
    def _():
        load(0, 0).start()
        zbuf[...] = jnp.zeros(zbuf.shape, F32)
        for_each_pad_copy(lambda cp: cp.start())

    @pl.when(i + 1 < n_tiles)
    def _():
        load(i + 1, lax.rem(i + 1, ns)).start()

    load(i, slot).wait()

    def issue(jb, carry):
        for u in range(SUBLANES):
            for k in range(2):
                e = 4 * (jb * SUBLANES + u) + 2 * k
                pltpu.make_async_copy(
                    xbuf.at[slot, jb, pl.ds(u, 1)],
                    xs_hbm.at[pos_ref[0, 0, e], pl.ds(pos_ref[0, 0, e + 1], 1)],
                    scat_sem.at[slot]).start(priority=k)
        return carry

    lax.fori_loop(0, tg, issue, 0)

    @pl.when(i > 0)
    def _():
        wait_scatter(lax.rem(i + ns - 1, ns))

    @pl.when(i == n_tiles - 1)
    def _():
        wait_scatter(slot)
        for_each_pad_copy(lambda cp: cp.wait())


def _dispatch(pos4, pads, xn, n_rows, td):
    T, D = xn.shape
    n = T // td
    tg = td // SUBLANES
    return pl.pallas_call(
        functools.partial(_dispatch_body, td=td, n_tiles=n),
        grid=(n,),
        in_specs=[pl.BlockSpec((1, 1, 4 * td), lambda i: (i, 0, 0), memory_space=pltpu.SMEM),
                  pl.BlockSpec((1, 1, 2 * N_EXPERTS + 1), lambda i: (0, 0, 0),
                               memory_space=pltpu.SMEM),
                  pl.BlockSpec(memory_space=pl.ANY)],
        out_specs=pl.BlockSpec(memory_space=pl.ANY),
        out_shape=jax.ShapeDtypeStruct((n_rows // SUBLANES, SUBLANES, D), F32),
        scratch_shapes=[pltpu.VMEM((3, tg, SUBLANES, D), F32),
                        pltpu.VMEM((MOE_BLOCK // SUBLANES, SUBLANES, D), F32),
                        pltpu.SemaphoreType.DMA((3,)), pltpu.SemaphoreType.DMA((3,)),
                        pltpu.SemaphoreType.DMA(())],
        compiler_params=_cparams(("arbitrary",)),
        name="dispatch",
    )(pos4.reshape(n, 1, 4 * td), pads.reshape(1, 1, 2 * N_EXPERTS + 1),
      xn.reshape(T // SUBLANES, SUBLANES, D)).reshape(n_rows, D)


def _ffn_body(blk_exp_ref, nused_ref, xs_ref, wgu_ref, wd_ref, ys_ref, wgu_bf, wd_bf):
    b = pl.program_id(0)
    de = wd_ref.shape[1]
    used = b < nused_ref[0]
    new_expert = (b == 0) | (blk_exp_ref[b] != blk_exp_ref[jnp.maximum(b - 1, 0)])

    @pl.when(used & new_expert)
    def _():
        wgu_bf[...] = wgu_ref[0].astype(BF16)
        wd_bf[...] = wd_ref[0].astype(BF16)

    @pl.when(used)
    def _():
        gu = jnp.dot(xs_ref[...].astype(BF16), wgu_bf[...], preferred_element_type=F32)
        gate, up = gu[:, 0:de], gu[:, de:2 * de]
        act = (gate * _sigmoid(gate) * up).astype(BF16)
        ys_ref[...] = jnp.dot(act, wd_bf[...], preferred_element_type=F32)

    @pl.when(jnp.logical_not(used))
    def _():
        ys_ref[...] = jnp.zeros(ys_ref.shape, F32)


def _ffn(blk_exp, nused, xs, wgu, wd):
    NR, D = xs.shape
    nb = NR // MOE_BLOCK
    de = wd.shape[1]
    grid_spec = pltpu.PrefetchScalarGridSpec(
        num_scalar_prefetch=2,
        grid=(nb,),
        in_specs=[pl.BlockSpec((MOE_BLOCK, D), lambda b, be, nu: (jnp.minimum(b, nu[0] - 1), 0)),
                  pl.BlockSpec((1, D, 2 * de), lambda b, be, nu: (be[b], 0, 0)),
                  pl.BlockSpec((1, de, D), lambda b, be, nu: (be[b], 0, 0))],
        out_specs=pl.BlockSpec((MOE_BLOCK, D), lambda b, be, nu: (b, 0)),
        scratch_shapes=[pltpu.VMEM((D, 2 * de), BF16), pltpu.VMEM((de, D), BF16)],
    )
    return pl.pallas_call(
        _ffn_body,
        grid_spec=grid_spec,
        out_shape=jax.ShapeDtypeStruct((NR, D), F32),
        compiler_params=_cparams(("arbitrary",)),
        name="expert_ffn",
    )(blk_exp, nused, xs, wgu, wd)


def _combine_body(pos_ref, posn_ref, ys_hbm, h_ref, route_ref, gfin_ref, out_ref, gbuf, sem,
                  *, tc, n_tiles, final):
    i = pl.program_id(0)
    slot = lax.rem(i, 2)
    tg = tc // SUBLANES

    def issue_tile(p_ref, s):
        def issue(jb, carry):
            for u in range(SUBLANES):
                for k in range(2):
                    e = 4 * (jb * SUBLANES + u) + 2 * k
                    pltpu.make_async_copy(
                        ys_hbm.at[p_ref[0, 0, e], pl.ds(p_ref[0, 0, e + 1], 1)],
                        gbuf.at[s, k, jb, pl.ds(u, 1)], sem.at[s]).start(priority=k)
            return carry

        lax.fori_loop(0, tg, issue, 0)

    @pl.when(i == 0)
    def _():
        issue_tile(pos_ref, 0)

    @pl.when(i + 1 < n_tiles)
    def _():
        issue_tile(posn_ref, lax.rem(i + 1, 2))

    for k in range(2):
        pltpu.make_async_copy(ys_hbm.at[pl.ds(0, tg)], gbuf.at[slot, k], sem.at[slot]).wait()

    D = h_ref.shape[1]
    w1 = route_ref[:, 2:3]
    w2 = route_ref[:, 3:4]
    out = h_ref[...] + (w1 * gbuf[slot, 0].reshape(tc, D) + w2 * gbuf[slot, 1].reshape(tc, D))
    if final:
        out = out * lax.rsqrt(jnp.mean(out * out, axis=-1, keepdims=True) + EPS) * gfin_ref[...]
    out_ref[...] = out


def _combine(pos4, ys, h2d, route, gfin, tc, final):
    T, D = h2d.shape
    n = T // tc
    row = lambda i: (i, 0)
    smem = functools.partial(pl.BlockSpec, (1, 1, 4 * tc), memory_space=pltpu.SMEM)
    pos3 = pos4.reshape(n, 1, 4 * tc)
    return pl.pallas_call(
        functools.partial(_combine_body, tc=tc, n_tiles=n, final=final),
        grid=(n,),
        in_specs=[smem(index_map=lambda i: (i, 0, 0)),
                  smem(index_map=lambda i: (jnp.minimum(i + 1, n - 1), 0, 0)),
                  pl.BlockSpec(memory_space=pl.ANY),
                  pl.BlockSpec((tc, D), row), pl.BlockSpec((tc, LANES), row),
                  pl.BlockSpec((1, D), lambda i: (0, 0))],
        out_specs=pl.BlockSpec((tc, D), row),
        out_shape=jax.ShapeDtypeStruct((T, D), F32),
        scratch_shapes=[pltpu.VMEM((2, 2, tc // SUBLANES, SUBLANES, D), F32),
                        pltpu.SemaphoreType.DMA((2,))],
        compiler_params=_cparams(("arbitrary",)),
        name="combine",
    )(pos3, pos3, ys.reshape(ys.shape[0] // SUBLANES, SUBLANES, D), h2d, route, gfin)


def _pick_tile(n, pref):
    t = min(pref, n)
    while n % t:
        t //= 2
    return t


def _prep_layer(w_in, b_mlstm_gate, w_gla_gate, b_gla_gate, w_out, w_group, b_group, w_expert,
                b_expert, w_gu, w_down):
    W = N_HEADS * DH
    WK = N_HEADS * GLA_DK
    D = w_in.shape[0]
    o_gates = 4 * W
    o_gla = o_gates + 2 * N_HEADS
    o_r = o_gla + 2 * WK + 2 * W
    wm = w_in[:, 0:o_gates].astype(BF16)
    wg = w_in[:, o_gla:o_r].astype(BF16)
    ws = jnp.zeros((D, LANES), F32)
    ws = ws.at[:, 0:2 * N_HEADS].set(w_in[:, o_gates:o_gla])
    ws = ws.at[:, 2 * N_HEADS:2 * N_HEADS + GLA_RANK].set(w_in[:, o_r:o_r + GLA_RANK]).astype(BF16)
    bg = jnp.zeros((1, LANES), F32).at[0, 0:2 * N_HEADS].set(b_mlstm_gate)
    wgg = jnp.zeros((LANES, WK), F32).at[2 * N_HEADS:2 * N_HEADS + GLA_RANK, :].set(w_gla_gate)
    wr = jnp.zeros((D, LANES), F32)
    wr = wr.at[:, 0:N_GROUPS].set(w_group).at[:, EXP_LANE0:EXP_LANE0 + N_EXPERTS].set(w_expert)
    br = jnp.zeros((1, LANES), F32)
    br = br.at[0, 0:N_GROUPS].set(b_group).at[0, EXP_LANE0:EXP_LANE0 + N_EXPERTS].set(b_expert)
    return dict(wm=wm, wg=wg, ws=ws, bg=bg, wgg=wgg.astype(BF16), bgg=b_gla_gate[None, :],
                wo=w_out.astype(BF16), wr3=jnp.concatenate(_split3(wr)[0:2], axis=1), br=br,
                wgu=w_gu, wd=w_down)


def _routing_tables(route, counts_row, T):
    eid = route[:, 0:2].astype(I32)
    rank = route[:, 4:6].astype(I32)
    counts = counts_row[EXP_LANE0:EXP_LANE0 + N_EXPERTS].astype(I32)
    pcounts = (counts + MOE_BLOCK - 1) // MOE_BLOCK * MOE_BLOCK
    pend = jnp.cumsum(pcounts)
    pstart = pend - pcounts
    onehot = eid[:, :, None] == jnp.arange(N_EXPERTS, dtype=I32)[None, None, :]
    pos = jnp.sum(jnp.where(onehot, pstart[None, None, :], 0), axis=-1) + rank
    n_blocks = (T * 2 + MOE_BLOCK - 1) // MOE_BLOCK + N_EXPERTS
    blk_start = jnp.arange(n_blocks, dtype=I32) * MOE_BLOCK
    blk_exp = jnp.minimum(jnp.sum((pend[None, :] <= blk_start[:, None]).astype(I32), axis=1),
                          N_EXPERTS - 1)
    nused = (pend[-1] // MOE_BLOCK).astype(I32).reshape(1)
    pos4 = jnp.stack([pos // SUBLANES, pos % SUBLANES], axis=-1)
    pads = jnp.concatenate([pstart + counts, pcounts - counts, nused])
    return pos4, pads, blk_exp, nused, n_blocks


def kernel(x, norm_mix, w_in, conv_w, conv_b, b_mlstm_gate, w_gla_gate, b_gla_gate, head_norm, w_out, norm_ffn, w_group, b_group, w_expert, b_expert, w_gu, w_down, norm_final):
    B, S, D = x.shape
    T = B * S
    depth = w_in.shape[0]
    W = N_HEADS * DH
    tm = _pick_tile(T, 512)
    td = _pick_tile(T, 512)
    tc = _pick_tile(T, 256)
    lt = _pick_tile(S, 256)
    nb = _pick_tile(B, 2)
    h = x.reshape(T, D)
    for l in range(depth):
        p = _prep_layer(w_in[l], b_mlstm_gate[l], w_gla_gate[l], b_gla_gate[l], w_out[l],
                        w_group[l], b_group[l], w_expert[l], b_expert[l], w_gu[l], w_down[l])
        main, gla, gates = _inproj(h, norm_mix[l][None, :], p['wm'], p['wg'], p['ws'], tm)
        ym = _mlstm(main, gates, conv_w[l], conv_b[l][None, :], p['bg'], head_norm[l][None, 0:W], B, S,
                    nb)
        yg = _gla(gla, gates, p['wgg'], p['bgg'], head_norm[l][None, W:2 * W], B, S, lt, nb)
        h, xn, route, cnt = _outproj_router(ym, yg, h, p['wo'], norm_ffn[l][None, :], p['wr3'],
                                            p['br'], tm)
        pos4, pads, blk_exp, nused, n_blocks = _routing_tables(route, cnt[0], T)
        xs = _dispatch(pos4, pads, xn, n_blocks * MOE_BLOCK, td)
        ys = _ffn(blk_exp, nused, xs, p['wgu'], p['wd'])
        h = _combine(pos4, ys, h, route, norm_final[None, :], tc, final=(l == depth - 1))
    return h.reshape(B, S, D)
```

```python
import functools

import jax
import jax.numpy as jnp
from jax import lax
from jax.experimental import pallas as pl
from jax.experimental.pallas import tpu as pltpu

F32 = jnp.float32
BF16 = jnp.bfloat16
I32 = jnp.int32

EPS = 1e-6
LANES = 128
SUBLANES = 8
VMEM_LIMIT = 56 * 1024 * 1024

N_HEADS = 4
DH = 128
GLA_DK = 64
GLA_RANK = 16
GLA_GATE_NORM = 16.0
CONV_W = 4
MLSTM_L = 128
GLA_L = 64
GLA_R = 16
N_GROUPS = 4
EPG = 8
N_EXPERTS = N_GROUPS * EPG
MOE_BLOCK = 512
EXP_LANE0 = N_GROUPS

ROW_UNROLL = 8

NEG_INF = float("-inf")


def _cparams(sem):
    return pltpu.CompilerParams(dimension_semantics=sem, vmem_limit_bytes=VMEM_LIMIT)


def _split3(x):
    hi = x.astype(BF16)
    r1 = x - hi.astype(F32)
    mid = r1.astype(BF16)
    lo = (r1 - mid.astype(F32)).astype(BF16)
    return hi, mid, lo


def _cumsum_rows(tri_bf, x):
    hi, mid, lo = _split3(x)
    dot = functools.partial(jnp.dot, preferred_element_type=F32)
    return dot(tri_bf, hi) + dot(tri_bf, mid) + dot(tri_bf, lo)


def _sigmoid(x):
    return 1.0 / (1.0 + jnp.exp(-x))


def _log_sigmoid(x):
    return jnp.minimum(x, 0.0) - jnp.log(1.0 + jnp.exp(-jnp.abs(x)))


def _inproj_body(x_ref, g_ref, wm_ref, wg_ref, ws_ref, om_ref, og_ref, os_ref):
    x = x_ref[...]
    xn = x * lax.rsqrt(jnp.mean(x * x, axis=-1, keepdims=True) + EPS) * g_ref[...]
    xb = xn.astype(BF16)
    om_ref[...] = jnp.dot(xb, wm_ref[...], preferred_element_type=F32).astype(BF16)
    og_ref[...] = jnp.dot(xb, wg_ref[...], preferred_element_type=F32).astype(BF16)
    os_ref[...] = jnp.dot(xb, ws_ref[...], preferred_element_type=F32)


def _inproj(h2d, gain, wm, wg, ws, tm):
    T, D = h2d.shape
    nm, ng, ns = wm.shape[1], wg.shape[1], ws.shape[1]
    const = lambda i: (0, 0)
    row = lambda i: (i, 0)
    return pl.pallas_call(
        _inproj_body,
        grid=(T // tm,),
        in_specs=[pl.BlockSpec((tm, D), row), pl.BlockSpec((1, D), const),
                  pl.BlockSpec((D, nm), const), pl.BlockSpec((D, ng), const),
                  pl.BlockSpec((D, ns), const)],
        out_specs=[pl.BlockSpec((tm, nm), row), pl.BlockSpec((tm, ng), row),
                   pl.BlockSpec((tm, ns), row)],
        out_shape=[jax.ShapeDtypeStruct((T, nm), BF16), jax.ShapeDtypeStruct((T, ng), BF16),
                   jax.ShapeDtypeStruct((T, ns), F32)],
        compiler_params=_cparams(("parallel",)),
        name="inproj",
    )(h2d, gain, wm, wg, ws)


def _mlstm_body(main_ref, gate_ref, cw_ref, cb_ref, bg_ref, hn_ref, y_ref, *scratch):
    L = MLSTM_L
    W = N_HEADS * DH
    TAIL = 2 * SUBLANES
    nb = main_ref.shape[0]
    ubufs, qkbufs, csts, msts = (scratch[i * nb:(i + 1) * nb] for i in range(4))
    dot = functools.partial(jnp.dot, preferred_element_type=F32)

    @pl.when(pl.program_id(1) == 0)
    def _():
        for bb in range(nb):
            ubufs[bb][0:TAIL, :] = jnp.zeros((TAIL, 2 * W), BF16)
            csts[bb][...] = jnp.zeros(csts[bb].shape, F32)
            msts[bb][...] = jnp.zeros(msts[bb].shape, F32)

    row = lax.broadcasted_iota(I32, (L, L), 0)
    col = lax.broadcasted_iota(I32, (L, L), 1)
    tri = col <= row
    tri_bf = jnp.where(tri, 1.0, 0.0).astype(BF16)
    lane = lax.broadcasted_iota(I32, (L, LANES), 1)
    sh_r = lax.broadcasted_iota(I32, ((CONV_W - 1) * L, L + TAIL), 0)
    sh_c = lax.broadcasted_iota(I32, ((CONV_W - 1) * L, L + TAIL), 1)
    shift = jnp.where(sh_c == (sh_r % L) + TAIL - (CONV_W - 1) + sh_r // L, 1.0, 0.0).astype(BF16)
    rp_r = lax.broadcasted_iota(I32, (LANES, 2 * N_HEADS * LANES), 0)
    rp_c = lax.broadcasted_iota(I32, (LANES, 2 * N_HEADS * LANES), 1)
    rep = jnp.where(rp_r == rp_c // LANES, 1.0, 0.0).astype(BF16)

    gates = []
    for bb in range(nb):
        ubuf = ubufs[bb]
        u = main_ref[bb, :, 0:2 * W]
        ubuf[TAIL:TAIL + L, :] = u
        shifted = dot(shift, ubuf[...])
        ubuf[0:TAIL, :] = ubuf[L:L + TAIL, :]
        acc = cb_ref[...] + u.astype(F32) * cw_ref[CONV_W - 1:CONV_W, :]
        for i in range(CONV_W - 1):
            acc = acc + shifted[i * L:(i + 1) * L, :] * cw_ref[i:i + 1, :]
        qk = acc * _sigmoid(acc)
        qkbufs[bb][:, 0:W] = qk[:, 0:W]
        qkbufs[bb][:, W:2 * W] = qk[:, W:2 * W] * (DH ** -0.5)

        gpre = gate_ref[bb] + bg_ref[...]
        xg = jnp.where(lane < N_HEADS, gpre, _log_sigmoid(gpre))
        gm = jnp.where(lane < N_HEADS, xg, _cumsum_rows(tri_bf, xg))
        hi, mid, lo = _split3(gm)
        gates.append((dot(hi, rep) + dot(mid, rep) + dot(lo, rep), gm.T))

    chains = [_mlstm_head(h, main_ref.at[bb], hn_ref, y_ref.at[bb], qkbufs[bb], csts[bb], msts[bb],
                          gates[bb], tri)
              for h in range(N_HEADS) for bb in range(nb)]
    for _ in range(MLSTM_STAGES):
        for c in chains:
            next(c)


MLSTM_STAGES = 3


def _mlstm_head(h, main_ref, hn_ref, y_ref, qkbuf, cst, mst, gates, tri):
    L = MLSTM_L
    W = N_HEADS * DH
    dot = functools.partial(jnp.dot, preferred_element_type=F32)
    g_rep, g_t = gates
    hs = slice(h * DH, (h + 1) * DH)

    q = qkbuf[:, hs].astype(BF16)
    k = qkbuf[:, W + h * DH:W + (h + 1) * DH]
    v = main_ref[:, 2 * W + h * DH:2 * W + (h + 1) * DH]
    vext = jnp.concatenate([v, jnp.ones((L, DH), BF16)], axis=1)
    cext = cst[h]
    s_raw = lax.dot_general(q, k.astype(BF16), (((1,), (1,)), ((), ())), preferred_element_type=F32)
    qc = dot(q, cext.astype(BF16))
    yield

    ig_rep = g_rep[:, h * LANES:(h + 1) * LANES]
    b_rep = g_rep[:, (N_HEADS + h) * LANES:(N_HEADS + h + 1) * LANES]
    ig_row = g_t[h:h + 1, :]
    b_row = g_t[N_HEADS + h:N_HEADS + h + 1, :]
    m_prev = mst[h]
    logd = jnp.where(tri, b_rep - b_row + ig_row, NEG_INF)
    m_inter = b_rep + m_prev
    m_j = jnp.maximum(m_inter, jnp.max(logd, axis=-1, keepdims=True))
    s = s_raw * jnp.exp(logd - m_j)
    a = jnp.exp(m_inter - m_j)
    sv = dot(s.astype(BF16), vext)
    g = b_rep[L - 1:L, :]
    m_new = jnp.maximum(g + m_prev, jnp.max(g - b_row + ig_row, axis=-1, keepdims=True))
    kw = (k * jnp.exp(g - b_rep + ig_rep - m_new)).astype(BF16)
    decay = jnp.exp(g + m_prev - m_new)
    upd = lax.dot_general(kw, vext, (((0,), (0,)), ((), ())), preferred_element_type=F32)
    yield

    num = sv[:, 0:DH] + a * qc[:, 0:DH]
    den = sv[:, DH:2 * DH] + a * qc[:, DH:2 * DH]
    hh = num / jnp.maximum(jnp.abs(den), jnp.exp(-m_j))
    cst[h] = jnp.concatenate([decay, decay], axis=1) * cext + upd
    mst[h] = m_new
    hn = hh * lax.rsqrt(jnp.mean(hh * hh, axis=-1, keepdims=True) + EPS)
    o_gate = _sigmoid(main_ref[:, 3 * W + h * DH:3 * W + (h + 1) * DH].astype(F32))
    y_ref[:, hs] = (hn * hn_ref[:, hs] * o_gate).astype(BF16)
    yield


def _mlstm(main, gates, conv_w, conv_b, b_gate, head_norm, B, S, nb):
    L = MLSTM_L
    W = N_HEADS * DH
    tok = lambda b, c: (b, c, 0)
    const = lambda b, c: (0, 0)
    return pl.pallas_call(
        _mlstm_body,
        grid=(B // nb, S // L),
        in_specs=[pl.BlockSpec((nb, L, 4 * W), tok), pl.BlockSpec((nb, L, LANES), tok),
                  pl.BlockSpec((CONV_W, 2 * W), const), pl.BlockSpec((1, 2 * W), const),
                  pl.BlockSpec((1, LANES), const), pl.BlockSpec((1, W), const)],
        out_specs=pl.BlockSpec((nb, L, W), tok),
        out_shape=jax.ShapeDtypeStruct((B, S, W), BF16),
        scratch_shapes=([pltpu.VMEM((L + 2 * SUBLANES, 2 * W), BF16)] * nb
                        + [pltpu.VMEM((L, 2 * W), F32)] * nb
                        + [pltpu.VMEM((N_HEADS, DH, 2 * DH), F32)] * nb
                        + [pltpu.VMEM((N_HEADS, 1, LANES), F32)] * nb),
        compiler_params=_cparams(("arbitrary", "arbitrary")),
        name="mlstm",
    )(main.reshape(B, S, 4 * W), gates.reshape(B, S, LANES), conv_w, conv_b, b_gate,
      head_norm).reshape(B * S, W)


GLA_STAGES = 4


def _gla_consts():
    L, R = GLA_L, GLA_R
    row = lax.broadcasted_iota(I32, (L, L), 0)
    col = lax.broadcasted_iota(I32, (L, L), 1)
    return dict(
        tri_bf=jnp.where(col <= row, 1.0, 0.0).astype(BF16),
        lane=lax.broadcasted_iota(I32, (L, LANES), 1),
        rowl=lax.broadcasted_iota(I32, (L, LANES), 0),
        lane_r=lax.broadcasted_iota(I32, (R, LANES), 1),
        row_r=lax.broadcasted_iota(I32, (R, LANES), 0),
        wsel=jnp.where(lax.broadcasted_iota(I32, (LANES, 2 * LANES), 0) // GLA_DK
                       == lax.broadcasted_iota(I32, (LANES, 2 * LANES), 1) // LANES,
                       1.0, 0.0).astype(BF16))


def _gla_pair(p, la, rows, c, gla_ref, hn_ref, y_ref, sst):
    L, R = GLA_L, GLA_R
    WK = N_HEADS * GLA_DK
    WV = N_HEADS * DH
    dot = functools.partial(jnp.dot, preferred_element_type=F32)
    nt = (((1,), (1,)), ((), ()))
    tn = (((0,), (0,)), ((), ()))
    lane, rowl, lane_r, row_r = c['lane'], c['rowl'], c['lane_r'], c['row_r']

    bc = _cumsum_rows(c['tri_bf'], la[:, p * LANES:(p + 1) * LANES])
    yield

    q2 = gla_ref[rows, p * LANES:(p + 1) * LANES].astype(F32) * (GLA_DK ** -0.5)
    k2 = gla_ref[rows, WK + p * LANES:WK + (p + 1) * LANES].astype(F32)
    g_last = bc[L - 1:L, :]
    q_in = q2 * jnp.exp(bc)
    kd = k2 * jnp.exp(g_last - bc)
    decay = jnp.exp(g_last)

    k_off = [None]
    q_off = [None]
    for j in range(1, L // R):
        rj = bc[j * R:j * R + 1, :]
        k_off.append((k2 * jnp.exp(jnp.where(rowl < j * R, rj - bc, NEG_INF))).astype(BF16))
        q_off.append(q2[j * R:(j + 1) * R, :] * jnp.exp(bc[j * R:(j + 1) * R, :] - rj))

    diag = []
    for j in range(L // R):
        qb = q2[j * R:(j + 1) * R, :]
        bq = bc[j * R:(j + 1) * R, :]
        terms = []
        for s in range(R):
            krow = k2[j * R + s:j * R + s + 1, :]
            brow = bc[j * R + s:j * R + s + 1, :]
            d = jnp.where(row_r >= s, bq - brow, NEG_INF)
            terms.append((qb * krow * jnp.exp(d)).astype(BF16))
        diag.append(dot(jnp.concatenate(terms, axis=0), c['wsel']))

    heads = []
    for hh in range(2):
        h = 2 * p + hh
        mh = (lane // GLA_DK) == hh
        mq = (lane_r // GLA_DK) == hh
        v = gla_ref[rows, 2 * WK + h * DH:2 * WK + (h + 1) * DH]
        st = sst[h]
        o_inter = lax.dot_general(jnp.where(mh, q_in, 0.0).astype(BF16), st.astype(BF16), nt,
                                  preferred_element_type=F32)
        upd = lax.dot_general(v, jnp.where(mh, kd, 0.0).astype(BF16), tn, preferred_element_type=F32)
        offs = [None] + [lax.dot_general(jnp.where(mq, q_off[j], 0.0).astype(BF16), k_off[j], nt,
                                         preferred_element_type=F32) for j in range(1, L // R)]
        heads.append((h, v, st, o_inter, upd, offs))
    yield

    outs = []
    for hh, (h, v, st, o_inter, upd, offs) in enumerate(heads):
        sst[h] = st * decay + upd
        a_rows = []
        for j in range(L // R):
            res = diag[j][:, hh * LANES:(hh + 1) * LANES]
            blk = jnp.zeros((R, LANES), F32)
            for s in range(R):
                blk = jnp.where(lane_r == j * R + s, res[s * R:(s + 1) * R, :], blk)
            blk = blk[:, 0:L]
            if j > 0:
                blk = blk + offs[j]
            a_rows.append(blk)
        a_mat = jnp.concatenate(a_rows, axis=0).astype(BF16)
        outs.append((h, o_inter, dot(a_mat, v)))
    yield

    for h, o_inter, o_intra in outs:
        o = o_inter + o_intra
        on = o * lax.rsqrt(jnp.mean(o * o, axis=-1, keepdims=True) + EPS)
        z = gla_ref[rows, 2 * WK + WV + h * DH:2 * WK + WV + (h + 1) * DH].astype(F32)
        y_ref[rows, h * DH:(h + 1) * DH] = (
            on * hn_ref[:, h * DH:(h + 1) * DH] * (z * _sigmoid(z))).astype(BF16)
    yield


def _gla_body(gla_ref, gate_ref, wgg_ref, bgg_ref, hn_ref, y_ref, *ssts, n_chunks):
    nb = gla_ref.shape[0]

    @pl.when(pl.program_id(1) == 0)
    def _():
        for bb in range(nb):
            ssts[bb][...] = jnp.zeros(ssts[bb].shape, F32)

    consts = _gla_consts()

    def step(i, carry):
        rows = pl.ds(pl.multiple_of(i * GLA_L, GLA_L), GLA_L)
        chains = []
        for bb in range(nb):
            la = _log_sigmoid(jnp.dot(gate_ref[bb, rows, :].astype(BF16), wgg_ref[...],
                                      preferred_element_type=F32) + bgg_ref[...]) / GLA_GATE_NORM
            chains += [_gla_pair(p, la, rows, consts, gla_ref.at[bb], hn_ref, y_ref.at[bb], ssts[bb])
                       for p in range(N_HEADS // 2)]
        for _ in range(GLA_STAGES):
            for ch in chains:
                next(ch)
        return carry

    lax.fori_loop(0, n_chunks, step, 0)


def _gla(gla, gates, wgg, bgg, head_norm, B, S, lt, nb):
    WK = N_HEADS * GLA_DK
    WV = N_HEADS * DH
    tok = lambda b, c: (b, c, 0)
    const = lambda b, c: (0, 0)
    return pl.pallas_call(
        functools.partial(_gla_body, n_chunks=lt // GLA_L),
        grid=(B // nb, S // lt),
        in_specs=[pl.BlockSpec((nb, lt, 2 * WK + 2 * WV), tok), pl.BlockSpec((nb, lt, LANES), tok),
                  pl.BlockSpec((LANES, WK), const), pl.BlockSpec((1, WK), const),
                  pl.BlockSpec((1, WV), const)],
        out_specs=pl.BlockSpec((nb, lt, WV), tok),
        out_shape=jax.ShapeDtypeStruct((B, S, WV), BF16),
        scratch_shapes=[pltpu.VMEM((N_HEADS, DH, LANES), F32)] * nb,
        compiler_params=_cparams(("arbitrary", "arbitrary")),
        name="gla",
    )(gla.reshape(B, S, 2 * WK + 2 * WV), gates.reshape(B, S, LANES), wgg, bgg,
      head_norm).reshape(B * S, WV)


def _outproj_router_body(ym_ref, yg_ref, h_ref, wo_ref, g_ref, wr_ref, br_ref,
                         hout_ref, xn_ref, route_ref, cnt_ref, carry):
    tm = h_ref.shape[0]
    W = ym_ref.shape[1]
    dot = functools.partial(jnp.dot, preferred_element_type=F32)

    @pl.when(pl.program_id(0) == 0)
    def _():
        carry[...] = jnp.zeros(carry.shape, F32)

    hnew = h_ref[...] + dot(ym_ref[...], wo_ref[0:W, :]) + dot(yg_ref[...], wo_ref[W:2 * W, :])
    hout_ref[...] = hnew
    xn = hnew * lax.rsqrt(jnp.mean(hnew * hnew, axis=-1, keepdims=True) + EPS) * g_ref[...]
    xn_ref[...] = xn

    xh = xn.astype(BF16)
    xm = (xn - xh.astype(F32)).astype(BF16)
    l2 = dot(xh, wr_ref[...])
    logits = l2[:, 0:LANES] + (l2[:, LANES:2 * LANES] + dot(xm, wr_ref[:, 0:LANES])) + br_ref[...]

    lane = lax.broadcasted_iota(I32, (tm, LANES), 1)
    lane_f = lane.astype(F32)
    big = float(LANES)
    gl = jnp.where(lane < N_GROUPS, logits, NEG_INF)
    gmax = jnp.max(gl, axis=-1, keepdims=True)
    gsel = jnp.min(jnp.where(gl == gmax, lane_f, big), axis=-1, keepdims=True)
    g_gate = 1.0 / jnp.sum(jnp.where(lane < N_GROUPS, jnp.exp(logits - gmax), 0.0),
                           axis=-1, keepdims=True)
    in_grp = ((lane >= EXP_LANE0) & (lane < EXP_LANE0 + N_EXPERTS)
              & (((lane - EXP_LANE0) // EPG).astype(F32) == gsel))
    el = jnp.where(in_grp, logits, NEG_INF)
    v1 = jnp.max(el, axis=-1, keepdims=True)
    i1 = jnp.min(jnp.where(el == v1, lane_f, big), axis=-1, keepdims=True)
    el2 = jnp.where(lane_f == i1, NEG_INF, el)
    v2 = jnp.max(el2, axis=-1, keepdims=True)
    i2 = jnp.min(jnp.where(el2 == v2, lane_f, big), axis=-1, keepdims=True)
    e2 = jnp.exp(v2 - v1)
    w1 = g_gate / (1.0 + e2)
    w2 = g_gate * e2 / (1.0 + e2)

    oh1 = lane_f == i1
    oh2 = lane_f == i2
    oh = jnp.where(oh1, 1.0, 0.0) + jnp.where(oh2, 1.0, 0.0)
    r_i = lax.broadcasted_iota(I32, (tm, tm), 0)
    c_i = lax.broadcasted_iota(I32, (tm, tm), 1)
    strict = jnp.where(c_i < r_i, 1.0, 0.0).astype(BF16)
    before = dot(strict, oh.astype(BF16)) + carry[0:1, :]
    rank1 = jnp.sum(jnp.where(oh1, before, 0.0), axis=-1, keepdims=True)
    rank2 = jnp.sum(jnp.where(oh2, before, 0.0), axis=-1, keepdims=True)
    total = carry[0:1, :] + jnp.sum(oh, axis=0, keepdims=True)
    carry[...] = jnp.broadcast_to(total, carry.shape)
    cnt_ref[...] = jnp.broadcast_to(total, cnt_ref.shape)

    rec = jnp.where(lane == 0, i1 - EXP_LANE0, 0.0)
    rec = jnp.where(lane == 1, i2 - EXP_LANE0, rec)
    rec = jnp.where(lane == 2, w1, rec)
    rec = jnp.where(lane == 3, w2, rec)
    rec = jnp.where(lane == 4, rank1, rec)
    rec = jnp.where(lane == 5, rank2, rec)
    route_ref[...] = rec


def _outproj_router(ym, yg, h2d, wo, gain, wr3, br, tm):
    T, D = h2d.shape
    W = ym.shape[1]
    row = lambda i: (i, 0)
    const = lambda i: (0, 0)
    return pl.pallas_call(
        _outproj_router_body,
        grid=(T // tm,),
        in_specs=[pl.BlockSpec((tm, W), row), pl.BlockSpec((tm, W), row), pl.BlockSpec((tm, D), row),
                  pl.BlockSpec((2 * W, D), const), pl.BlockSpec((1, D), const),
                  pl.BlockSpec((D, 2 * LANES), const), pl.BlockSpec((1, LANES), const)],
        out_specs=[pl.BlockSpec((tm, D), row), pl.BlockSpec((tm, D), row),
                   pl.BlockSpec((tm, LANES), row), pl.BlockSpec((SUBLANES, LANES), const)],
        out_shape=[jax.ShapeDtypeStruct((T, D), F32), jax.ShapeDtypeStruct((T, D), F32),
                   jax.ShapeDtypeStruct((T, LANES), F32),
                   jax.ShapeDtypeStruct((SUBLANES, LANES), F32)],
        scratch_shapes=[pltpu.VMEM((SUBLANES, LANES), F32)],
        compiler_params=_cparams(("arbitrary",)),
        name="outproj_router",
    )(ym, yg, h2d, wo, gain, wr3, br)


def _dispatch_body(pos_ref, pad_ref, x_hbm, xs_hbm, xbuf, zbuf, load_sem, scat_sem, pad_sem,
                   *, td, n_tiles):
    i = pl.program_id(0)
    ns = xbuf.shape[0]
    slot = lax.rem(i, ns)

    def load(tile, s):
        rows = pl.ds(pl.multiple_of(tile * td, td), td)
        return pltpu.make_async_copy(x_hbm.at[rows], xbuf.at[s], load_sem.at[s])

    def wait_scatter(s):
        for _ in range(2):
            pltpu.make_async_copy(xbuf.at[s], xs_hbm.at[pl.ds(0, td)], scat_sem.at[s]).wait()

    def for_each_pad_copy(fn):
        def per_expert(e, carry):
            first = pad_ref[0, 0, e]
            count = pad_ref[0, 0, N_EXPERTS + e]
            lead = jnp.minimum(count, lax.rem(SUBLANES - lax.rem(first, SUBLANES), SUBLANES))

            def per_row(r, c):
                fn(pltpu.make_async_copy(zbuf.at[pl.ds(0, 1)], xs_hbm.at[pl.ds(first + r, 1)], pad_sem))
                return c

            def per_group(g, c):
                rows = pl.ds(pl.multiple_of(first + lead + g * SUBLANES, SUBLANES), SUBLANES)
                fn(pltpu.make_async_copy(zbuf.at[pl.ds(0, SUBLANES)], xs_hbm.at[rows], pad_sem))
                return c

            carry = lax.fori_loop(0, lead, per_row, carry)
            return lax.fori_loop(0, (count - lead) // SUBLANES, per_group, carry)

        lax.fori_loop(0, N_EXPERTS, per_expert, 0)

        def per_block(bk, carry):
            rows = pl.ds(pl.multiple_of(bk * MOE_BLOCK, MOE_BLOCK), MOE_BLOCK)
            fn(pltpu.make_async_copy(zbuf, xs_hbm.at[rows], pad_sem))
            return carry

        lax.fori_loop(pad_ref[0, 0, 2 * N_EXPERTS], xs_hbm.shape[0] // MOE_BLOCK, per_block, 0)

    @pl.when(i == 0)
    def _():
        load(0, 0).start()
        zbuf[...] = jnp.zeros(zbuf.shape, F32)
        for_each_pad_copy(lambda cp: cp.start())

    @pl.when(i + 1 < n_tiles)
    def _():
        load(i + 1, lax.rem(i + 1, ns)).start()

    load(i, slot).wait()

    def issue(jb, carry):
        for u in range(ROW_UNROLL):
            j = jb * ROW_UNROLL + u
            for k in range(2):
                pltpu.make_async_copy(xbuf.at[slot, pl.ds(j, 1)],
                                      xs_hbm.at[pl.ds(pos_ref[0, 0, 2 * j + k], 1)],
                                      scat_sem.at[slot]).start(priority=k)
        return carry

    lax.fori_loop(0, td // ROW_UNROLL, issue, 0)

    @pl.when(i > 0)
    def _():
        wait_scatter(lax.rem(i + ns - 1, ns))

    @pl.when(i == n_tiles - 1)
    def _():
        wait_scatter(slot)
        for_each_pad_copy(lambda cp: cp.wait())


def _dispatch(pos, pads, xn, n_rows, td):
    T, D = xn.shape
    n = T // td
    return pl.pallas_call(
        functools.partial(_dispatch_body, td=td, n_tiles=n),
        grid=(n,),
        in_specs=[pl.BlockSpec((1, 1, 2 * td), lambda i: (i, 0, 0), memory_space=pltpu.SMEM),
                  pl.BlockSpec((1, 1, 2 * N_EXPERTS + 1), lambda i: (0, 0, 0),
                               memory_space=pltpu.SMEM),
                  pl.BlockSpec(memory_space=pl.ANY)],
        out_specs=pl.BlockSpec(memory_space=pl.ANY),
        out_shape=jax.ShapeDtypeStruct((n_rows, D), F32),
        scratch_shapes=[pltpu.VMEM((3, td, D), F32), pltpu.VMEM((MOE_BLOCK, D), F32),
                        pltpu.SemaphoreType.DMA((3,)), pltpu.SemaphoreType.DMA((3,)),
                        pltpu.SemaphoreType.DMA(())],
        compiler_params=_cparams(("arbitrary",)),
        name="dispatch",
    )(pos.reshape(n, 1, 2 * td), pads.reshape(1, 1, 2 * N_EXPERTS + 1), xn)


def _ffn_body(blk_exp_ref, nused_ref, xs_ref, wgu_ref, wd_ref, ys_ref, wgu_bf, wd_bf):
    b = pl.program_id(0)
    de = wd_ref.shape[1]
    used = b < nused_ref[0]
    new_expert = (b == 0) | (blk_exp_ref[b] != blk_exp_ref[jnp.maximum(b - 1, 0)])

    @pl.when(used & new_expert)
    def _():
        wgu_bf[...] = wgu_ref[0].astype(BF16)
        wd_bf[...] = wd_ref[0].astype(BF16)

    @pl.when(used)
    def _():
        gu = jnp.dot(xs_ref[...].astype(BF16), wgu_bf[...], preferred_element_type=F32)
        gate, up = gu[:, 0:de], gu[:, de:2 * de]
        act = (gate * _sigmoid(gate) * up).astype(BF16)
        ys_ref[...] = jnp.dot(act, wd_bf[...], preferred_element_type=F32)

    @pl.when(jnp.logical_not(used))
    def _():
        ys_ref[...] = jnp.zeros(ys_ref.shape, F32)


def _ffn(blk_exp, nused, xs, wgu, wd):
    NR, D = xs.shape
    nb = NR // MOE_BLOCK
    de = wd.shape[1]
    grid_spec = pltpu.PrefetchScalarGridSpec(
        num_scalar_prefetch=2,
        grid=(nb,),
        in_specs=[pl.BlockSpec((MOE_BLOCK, D), lambda b, be, nu: (jnp.minimum(b, nu[0] - 1), 0)),
                  pl.BlockSpec((1, D, 2 * de), lambda b, be, nu: (be[b], 0, 0)),
                  pl.BlockSpec((1, de, D), lambda b, be, nu: (be[b], 0, 0))],
        out_specs=pl.BlockSpec((MOE_BLOCK, D), lambda b, be, nu: (b, 0)),
        scratch_shapes=[pltpu.VMEM((D, 2 * de), BF16), pltpu.VMEM((de, D), BF16)],
    )
    return pl.pallas_call(
        _ffn_body,
        grid_spec=grid_spec,
        out_shape=jax.ShapeDtypeStruct((NR, D), F32),
        compiler_params=_cparams(("arbitrary",)),
        name="expert_ffn",
    )(blk_exp, nused, xs, wgu, wd)


def _combine_body(pos_ref, posn_ref, ys_hbm, h_ref, route_ref, gfin_ref, out_ref, gbuf, sem,
                  *, tc, n_tiles, final):
    i = pl.program_id(0)
    slot = lax.rem(i, 2)

    def issue_tile(p_ref, s):
        def issue(jb, carry):
            for u in range(ROW_UNROLL):
                j = jb * ROW_UNROLL + u
                for k in range(2):
                    pltpu.make_async_copy(ys_hbm.at[pl.ds(p_ref[0, 0, 2 * j + k], 1)],
                                          gbuf.at[s, k, pl.ds(j, 1)], sem.at[s]).start(priority=k)
            return carry

        lax.fori_loop(0, tc // ROW_UNROLL, issue, 0)

    @pl.when(i == 0)
    def _():
        issue_tile(pos_ref, 0)

    @pl.when(i + 1 < n_tiles)
    def _():
        issue_tile(posn_ref, lax.rem(i + 1, 2))

    for k in range(2):
        pltpu.make_async_copy(ys_hbm.at[pl.ds(0, tc)], gbuf.at[slot, k], sem.at[slot]).wait()

    w1 = route_ref[:, 2:3]
    w2 = route_ref[:, 3:4]
    out = h_ref[...] + (w1 * gbuf[slot, 0] + w2 * gbuf[slot, 1])
    if final:
        out = out * lax.rsqrt(jnp.mean(out * out, axis=-1, keepdims=True) + EPS) * gfin_ref[...]
    out_ref[...] = out


def _combine(pos, ys, h2d, route, gfin, tc, final):
    T, D = h2d.shape
    n = T // tc
    pos3 = pos.reshape(n, 1, 2 * tc)
    row = lambda i: (i, 0)
    smem = functools.partial(pl.BlockSpec, (1, 1, 2 * tc), memory_space=pltpu.SMEM)
    return pl.pallas_call(
        functools.partial(_combine_body, tc=tc, n_tiles=n, final=final),
        grid=(n,),
        in_specs=[smem(index_map=lambda i: (i, 0, 0)),
                  smem(index_map=lambda i: (jnp.minimum(i + 1, n - 1), 0, 0)),
                  pl.BlockSpec(memory_space=pl.ANY),
                  pl.BlockSpec((tc, D), row), pl.BlockSpec((tc, LANES), row),
                  pl.BlockSpec((1, D), lambda i: (0, 0))],
        out_specs=pl.BlockSpec((tc, D), row),
        out_shape=jax.ShapeDtypeStruct((T, D), F32),
        scratch_shapes=[pltpu.VMEM((2, 2, tc, D), F32), pltpu.SemaphoreType.DMA((2,))],
        compiler_params=_cparams(("arbitrary",)),
        name="combine",
    )(pos3, pos3, ys, h2d, route, gfin)


def _pick_tile(n, pref):
    t = min(pref, n)
    while n % t:
        t //= 2
    return t


def _prep_layer(w_in, b_mlstm_gate, w_gla_gate, b_gla_gate, w_out, w_group, b_group, w_expert,
                b_expert, w_gu, w_down):
    W = N_HEADS * DH
    WK = N_HEADS * GLA_DK
    D = w_in.shape[0]
    o_gates = 4 * W
    o_gla = o_gates + 2 * N_HEADS
    o_r = o_gla + 2 * WK + 2 * W
    wm = w_in[:, 0:o_gates].astype(BF16)
    wg = w_in[:, o_gla:o_r].astype(BF16)
    ws = jnp.zeros((D, LANES), F32)
    ws = ws.at[:, 0:2 * N_HEADS].set(w_in[:, o_gates:o_gla])
    ws = ws.at[:, 2 * N_HEADS:2 * N_HEADS + GLA_RANK].set(w_in[:, o_r:o_r + GLA_RANK]).astype(BF16)
    bg = jnp.zeros((1, LANES), F32).at[0, 0:2 * N_HEADS].set(b_mlstm_gate)
    wgg = jnp.zeros((LANES, WK), F32).at[2 * N_HEADS:2 * N_HEADS + GLA_RANK, :].set(w_gla_gate)
    wr = jnp.zeros((D, LANES), F32)
    wr = wr.at[:, 0:N_GROUPS].set(w_group).at[:, EXP_LANE0:EXP_LANE0 + N_EXPERTS].set(w_expert)
    br = jnp.zeros((1, LANES), F32)
    br = br.at[0, 0:N_GROUPS].set(b_group).at[0, EXP_LANE0:EXP_LANE0 + N_EXPERTS].set(b_expert)
    return dict(wm=wm, wg=wg, ws=ws, bg=bg, wgg=wgg.astype(BF16), bgg=b_gla_gate[None, :],
                wo=w_out.astype(BF16), wr3=jnp.concatenate(_split3(wr)[0:2], axis=1), br=br,
                wgu=w_gu, wd=w_down)


def _routing_tables(route, counts_row, T):
    eid = route[:, 0:2].astype(I32)
    rank = route[:, 4:6].astype(I32)
    counts = counts_row[EXP_LANE0:EXP_LANE0 + N_EXPERTS].astype(I32)
    pcounts = (counts + MOE_BLOCK - 1) // MOE_BLOCK * MOE_BLOCK
    pend = jnp.cumsum(pcounts)
    pstart = pend - pcounts
    onehot = eid[:, :, None] == jnp.arange(N_EXPERTS, dtype=I32)[None, None, :]
    pos = jnp.sum(jnp.where(onehot, pstart[None, None, :], 0), axis=-1) + rank
    n_blocks = (T * 2 + MOE_BLOCK - 1) // MOE_BLOCK + N_EXPERTS
    blk_start = jnp.arange(n_blocks, dtype=I32) * MOE_BLOCK
    blk_exp = jnp.minimum(jnp.sum((pend[None, :] <= blk_start[:, None]).astype(I32), axis=1),
                          N_EXPERTS - 1)
    nused = (pend[-1] // MOE_BLOCK).astype(I32).reshape(1)
    pads = jnp.concatenate([pstart + counts, pcounts - counts, nused])
    return pos, pads, blk_exp, nused, n_blocks


def kernel(x, norm_mix, w_in, conv_w, conv_b, b_mlstm_gate, w_gla_gate, b_gla_gate, head_norm, w_out, norm_ffn, w_group, b_group, w_expert, b_expert, w_gu, w_down, norm_final):
    B, S, D = x.shape
    T = B * S
    depth = w_in.shape[0]
    W = N_HEADS * DH
    tm = _pick_tile(T, 512)
    td = _pick_tile(T, 512)
    tc = _pick_tile(T, 256)
    lt = _pick_tile(S, 256)
    nb = _pick_tile(B, 2)
    h = x.reshape(T, D)
    for l in range(depth):
        p = _prep_layer(w_in[l], b_mlstm_gate[l], w_gla_gate[l], b_gla_gate[l], w_out[l],
                        w_group[l], b_group[l], w_expert[l], b_expert[l], w_gu[l], w_down[l])
        main, gla, gates = _inproj(h, norm_mix[l][None, :], p['wm'], p['wg'], p['ws'], tm)
        ym = _mlstm(main, gates, conv_w[l], conv_b[l][None, :], p['bg'], head_norm[l][None, 0:W], B, S,
                    nb)
        yg = _gla(gla, gates, p['wgg'], p['bgg'], head_norm[l][None, W:2 * W], B, S, lt, nb)
        h, xn, route, cnt = _outproj_router(ym, yg, h, p['wo'], norm_ffn[l][None, :], p['wr3'],
                                            p['br'], tm)
        pos, pads, blk_exp, nused, n_blocks = _routing_tables(route, cnt[0], T)
        xs = _dispatch(pos, pads, xn, n_blocks * MOE_BLOCK, td)
        ys = _ffn(blk_exp, nused, xs, p['wgu'], p['wd'])
        h = _combine(pos, ys, h, route, norm_final[None, :], tc, final=(l == depth - 1))
    return h.reshape(B, S, D)
```

```python
import functools

import jax
import jax.numpy as jnp
from jax import lax
from jax.experimental import pallas as pl
from jax.experimental.pallas import tpu as pltpu

F32 = jnp.float32
BF16 = jnp.bfloat16
I32 = jnp.int32

EPS = 1e-6
LANES = 128
SUBLANES = 8
VMEM_LIMIT = 56 * 1024 * 1024

N_HEADS = 4
DH = 128
GLA_DK = 64
GLA_RANK = 16
GLA_GATE_NORM = 16.0
CONV_W = 4
MLSTM_L = 128
GLA_L = 64
GLA_R = 16
N_GROUPS = 4
EPG = 8
N_EXPERTS = N_GROUPS * EPG
MOE_BLOCK = 512
EXP_LANE0 = N_GROUPS

ROW_UNROLL = 8

NEG_INF = float("-inf")


def _cparams(sem):
    return pltpu.CompilerParams(dimension_semantics=sem, vmem_limit_bytes=VMEM_LIMIT)


def _split3(x):
    hi = x.astype(BF16)
    r1 = x - hi.astype(F32)
    mid = r1.astype(BF16)
    lo = (r1 - mid.astype(F32)).astype(BF16)
    return hi, mid, lo


def _cumsum_rows(tri_bf, x):
    hi, mid, lo = _split3(x)
    dot = functools.partial(jnp.dot, preferred_element_type=F32)
    return dot(tri_bf, hi) + dot(tri_bf, mid) + dot(tri_bf, lo)


def _sigmoid(x):
    return 1.0 / (1.0 + jnp.exp(-x))


def _log_sigmoid(x):
    return jnp.minimum(x, 0.0) - jnp.log(1.0 + jnp.exp(-jnp.abs(x)))


def _inproj_body(x_ref, g_ref, wm_ref, wg_ref, ws_ref, om_ref, og_ref, os_ref):
    x = x_ref[...]
    xn = x * lax.rsqrt(jnp.mean(x * x, axis=-1, keepdims=True) + EPS) * g_ref[...]
    xb = xn.astype(BF16)
    om_ref[...] = jnp.dot(xb, wm_ref[...], preferred_element_type=F32).astype(BF16)
    og_ref[...] = jnp.dot(xb, wg_ref[...], preferred_element_type=F32).astype(BF16)
    os_ref[...] = jnp.dot(xb, ws_ref[...], preferred_element_type=F32)


def _inproj(h2d, gain, wm, wg, ws, tm):
    T, D = h2d.shape
    nm, ng, ns = wm.shape[1], wg.shape[1], ws.shape[1]
    const = lambda i: (0, 0)
    row = lambda i: (i, 0)
    return pl.pallas_call(
        _inproj_body,
        grid=(T // tm,),
        in_specs=[pl.BlockSpec((tm, D), row), pl.BlockSpec((1, D), const),
                  pl.BlockSpec((D, nm), const), pl.BlockSpec((D, ng), const),
                  pl.BlockSpec((D, ns), const)],
        out_specs=[pl.BlockSpec((tm, nm), row), pl.BlockSpec((tm, ng), row),
                   pl.BlockSpec((tm, ns), row)],
        out_shape=[jax.ShapeDtypeStruct((T, nm), BF16), jax.ShapeDtypeStruct((T, ng), BF16),
                   jax.ShapeDtypeStruct((T, ns), F32)],
        compiler_params=_cparams(("parallel",)),
        name="inproj",
    )(h2d, gain, wm, wg, ws)


def _inproj_combine_body(pos_ref, posn_ref, ys_hbm, h_ref, route_ref, g_ref, wm_ref, wg_ref, ws_ref,
                         hout_ref, om_ref, og_ref, os_ref, gbuf, sem, *, n_tiles):
    tm = h_ref.shape[0]
    i = pl.program_id(0)
    slot = lax.rem(i, 2)

    def row_copy(p_ref, s, j, k):
        return pltpu.make_async_copy(ys_hbm.at[pl.ds(p_ref[0, 0, 2 * j + k], 1)],
                                     gbuf.at[s, k, pl.ds(j, 1)], sem.at[s])

    def wait_tile(s):
        for k in range(2):
            pltpu.make_async_copy(ys_hbm.at[pl.ds(0, tm)], gbuf.at[s, k], sem.at[s]).wait()

    @pl.when(i == 0)
    def _():
        def issue(jb, carry):
            for u in range(ROW_UNROLL):
                for k in range(2):
                    row_copy(pos_ref, 0, jb * ROW_UNROLL + u, k).start(priority=k)
            return carry

        lax.fori_loop(0, tm // ROW_UNROLL, issue, 0)

    wait_tile(slot)
    x = h_ref[...] + (route_ref[:, 2:3] * gbuf[slot, 0] + route_ref[:, 3:4] * gbuf[slot, 1])
    hout_ref[...] = x

    for j in range(tm):
        for k in range(2):
            row_copy(posn_ref, 1 - slot, j, k).start(priority=k)

    xn = x * lax.rsqrt(jnp.mean(x * x, axis=-1, keepdims=True) + EPS) * g_ref[...]
    xb = xn.astype(BF16)
    om_ref[...] = jnp.dot(xb, wm_ref[...], preferred_element_type=F32).astype(BF16)
    og_ref[...] = jnp.dot(xb, wg_ref[...], preferred_element_type=F32).astype(BF16)
    os_ref[...] = jnp.dot(xb, ws_ref[...], preferred_element_type=F32)

    @pl.when(i == n_tiles - 1)
    def _():
        wait_tile(1 - slot)


def _inproj_combine(pos, ys, h2d, route, gain, wm, wg, ws, tm):
    T, D = h2d.shape
    n = T // tm
    nm, ng, ns = wm.shape[1], wg.shape[1], ws.shape[1]
    const = lambda i: (0, 0)
    row = lambda i: (i, 0)
    pos3 = pos.reshape(n, 1, 2 * tm)
    smem = functools.partial(pl.BlockSpec, (1, 1, 2 * tm), memory_space=pltpu.SMEM)
    return pl.pallas_call(
        functools.partial(_inproj_combine_body, n_tiles=n),
        grid=(n,),
        in_specs=[smem(index_map=lambda i: (i, 0, 0)),
                  smem(index_map=lambda i: (jnp.minimum(i + 1, n - 1), 0, 0)),
                  pl.BlockSpec(memory_space=pl.ANY),
                  pl.BlockSpec((tm, D), row), pl.BlockSpec((tm, LANES), row), pl.BlockSpec((1, D), const),
                  pl.BlockSpec((D, nm), const), pl.BlockSpec((D, ng), const),
                  pl.BlockSpec((D, ns), const)],
        out_specs=[pl.BlockSpec((tm, D), row), pl.BlockSpec((tm, nm), row), pl.BlockSpec((tm, ng), row),
                   pl.BlockSpec((tm, ns), row)],
        out_shape=[jax.ShapeDtypeStruct((T, D), F32), jax.ShapeDtypeStruct((T, nm), BF16),
                   jax.ShapeDtypeStruct((T, ng), BF16), jax.ShapeDtypeStruct((T, ns), F32)],
        scratch_shapes=[pltpu.VMEM((2, 2, tm, D), F32), pltpu.SemaphoreType.DMA((2,))],
        compiler_params=_cparams(("arbitrary",)),
        name="inproj_combine",
    )(pos3, pos3, ys, h2d, route, gain, wm, wg, ws)


def _mlstm_body(main_ref, gate_ref, cw_ref, cb_ref, bg_ref, hn_ref, y_ref, *scratch):
    L = MLSTM_L
    W = N_HEADS * DH
    TAIL = 2 * SUBLANES
    nb = main_ref.shape[0]
    ubufs, qkbufs, csts, msts = (scratch[i * nb:(i + 1) * nb] for i in range(4))
    dot = functools.partial(jnp.dot, preferred_element_type=F32)

    @pl.when(pl.program_id(1) == 0)
    def _():
        for bb in range(nb):
            ubufs[bb][0:TAIL, :] = jnp.zeros((TAIL, 2 * W), BF16)
            csts[bb][...] = jnp.zeros(csts[bb].shape, F32)
            msts[bb][...] = jnp.zeros(msts[bb].shape, F32)

    row = lax.broadcasted_iota(I32, (L, L), 0)
    col = lax.broadcasted_iota(I32, (L, L), 1)
    tri = col <= row
    tri_bf = jnp.where(tri, 1.0, 0.0).astype(BF16)
    lane = lax.broadcasted_iota(I32, (L, LANES), 1)
    sh_r = lax.broadcasted_iota(I32, ((CONV_W - 1) * L, L + TAIL), 0)
    sh_c = lax.broadcasted_iota(I32, ((CONV_W - 1) * L, L + TAIL), 1)
    shift = jnp.where(sh_c == (sh_r % L) + TAIL - (CONV_W - 1) + sh_r // L, 1.0, 0.0).astype(BF16)
    rp_r = lax.broadcasted_iota(I32, (LANES, 2 * N_HEADS * LANES), 0)
    rp_c = lax.broadcasted_iota(I32, (LANES, 2 * N_HEADS * LANES), 1)
    rep = jnp.where(rp_r == rp_c // LANES, 1.0, 0.0).astype(BF16)

    gates = []
    for bb in range(nb):
        ubuf = ubufs[bb]
        u = main_ref[bb, :, 0:2 * W]
        ubuf[TAIL:TAIL + L, :] = u
        shifted = dot(shift, ubuf[...])
        ubuf[0:TAIL, :] = ubuf[L:L + TAIL, :]
        acc = cb_ref[...] + u.astype(F32) * cw_ref[CONV_W - 1:CONV_W, :]
        for i in range(CONV_W - 1):
            acc = acc + shifted[i * L:(i + 1) * L, :] * cw_ref[i:i + 1, :]
        qk = acc * _sigmoid(acc)
        qkbufs[bb][:, 0:W] = qk[:, 0:W]
        qkbufs[bb][:, W:2 * W] = qk[:, W:2 * W] * (DH ** -0.5)

        gpre = gate_ref[bb] + bg_ref[...]
        xg = jnp.where(lane < N_HEADS, gpre, _log_sigmoid(gpre))
        gm = jnp.where(lane < N_HEADS, xg, _cumsum_rows(tri_bf, xg))
        hi, mid, lo = _split3(gm)
        gates.append((dot(hi, rep) + dot(mid, rep) + dot(lo, rep), gm.T))

    chains = [_mlstm_head(h, main_ref.at[bb], hn_ref, y_ref.at[bb], qkbufs[bb], csts[bb], msts[bb],
                          gates[bb], tri)
              for h in range(N_HEADS) for bb in range(nb)]
    for _ in range(MLSTM_STAGES):
        for c in chains:
            next(c)


MLSTM_STAGES = 3


def _mlstm_head(h, main_ref, hn_ref, y_ref, qkbuf, cst, mst, gates, tri):
    L = MLSTM_L
    W = N_HEADS * DH
    dot = functools.partial(jnp.dot, preferred_element_type=F32)
    g_rep, g_t = gates
    hs = slice(h * DH, (h + 1) * DH)

    q = qkbuf[:, hs].astype(BF16)
    k = qkbuf[:, W + h * DH:W + (h + 1) * DH]
    v = main_ref[:, 2 * W + h * DH:2 * W + (h + 1) * DH]
    vext = jnp.concatenate([v, jnp.ones((L, DH), BF16)], axis=1)
    cext = cst[h]
    s_raw = lax.dot_general(q, k.astype(BF16), (((1,), (1,)), ((), ())), preferred_element_type=F32)
    qc = dot(q, cext.astype(BF16))
    yield

    ig_rep = g_rep[:, h * LANES:(h + 1) * LANES]
    b_rep = g_rep[:, (N_HEADS + h) * LANES:(N_HEADS + h + 1) * LANES]
    ig_row = g_t[h:h + 1, :]
    b_row = g_t[N_HEADS + h:N_HEADS + h + 1, :]
    m_prev = mst[h]
    logd = jnp.where(tri, b_rep - b_row + ig_row, NEG_INF)
    m_inter = b_rep + m_prev
    m_j = jnp.maximum(m_inter, jnp.max(logd, axis=-1, keepdims=True))
    s = s_raw * jnp.exp(logd - m_j)
    a = jnp.exp(m_inter - m_j)
    sv = dot(s.astype(BF16), vext)
    g = b_rep[L - 1:L, :]
    m_new = jnp.maximum(g + m_prev, jnp.max(g - b_row + ig_row, axis=-1, keepdims=True))
    kw = (k * jnp.exp(g - b_rep + ig_rep - m_new)).astype(BF16)
    decay = jnp.exp(g + m_prev - m_new)
    upd = lax.dot_general(kw, vext, (((0,), (0,)), ((), ())), preferred_element_type=F32)
    yield

    num = sv[:, 0:DH] + a * qc[:, 0:DH]
    den = sv[:, DH:2 * DH] + a * qc[:, DH:2 * DH]
    hh = num / jnp.maximum(jnp.abs(den), jnp.exp(-m_j))
    cst[h] = jnp.concatenate([decay, decay], axis=1) * cext + upd
    mst[h] = m_new
    hn = hh * lax.rsqrt(jnp.mean(hh * hh, axis=-1, keepdims=True) + EPS)
    o_gate = _sigmoid(main_ref[:, 3 * W + h * DH:3 * W + (h + 1) * DH].astype(F32))
    y_ref[:, hs] = (hn * hn_ref[:, hs] * o_gate).astype(BF16)
    yield


def _mlstm(main, gates, conv_w, conv_b, b_gate, head_norm, B, S, nb):
    L = MLSTM_L
    W = N_HEADS * DH
    tok = lambda b, c: (b, c, 0)
    const = lambda b, c: (0, 0)
    return pl.pallas_call(
        _mlstm_body,
        grid=(B // nb, S // L),
        in_specs=[pl.BlockSpec((nb, L, 4 * W), tok), pl.BlockSpec((nb, L, LANES), tok),
                  pl.BlockSpec((CONV_W, 2 * W), const), pl.BlockSpec((1, 2 * W), const),
                  pl.BlockSpec((1, LANES), const), pl.BlockSpec((1, W), const)],
        out_specs=pl.BlockSpec((nb, L, W), tok),
        out_shape=jax.ShapeDtypeStruct((B, S, W), BF16),
        scratch_shapes=([pltpu.VMEM((L + 2 * SUBLANES, 2 * W), BF16)] * nb
                        + [pltpu.VMEM((L, 2 * W), F32)] * nb
                        + [pltpu.VMEM((N_HEADS, DH, 2 * DH), F32)] * nb
                        + [pltpu.VMEM((N_HEADS, 1, LANES), F32)] * nb),
        compiler_params=_cparams(("arbitrary", "arbitrary")),
        name="mlstm",
    )(main.reshape(B, S, 4 * W), gates.reshape(B, S, LANES), conv_w, conv_b, b_gate,
      head_norm).reshape(B * S, W)


GLA_STAGES = 4


def _gla_consts():
    L, R = GLA_L, GLA_R
    row = lax.broadcasted_iota(I32, (L, L), 0)
    col = lax.broadcasted_iota(I32, (L, L), 1)
    return dict(
        tri_bf=jnp.where(col <= row, 1.0, 0.0).astype(BF16),
        lane=lax.broadcasted_iota(I32, (L, LANES), 1),
        rowl=lax.broadcasted_iota(I32, (L, LANES), 0),
        lane_r=lax.broadcasted_iota(I32, (R, LANES), 1),
        row_r=lax.broadcasted_iota(I32, (R, LANES), 0),
        wsel=jnp.where(lax.broadcasted_iota(I32, (LANES, 2 * LANES), 0) // GLA_DK
                       == lax.broadcasted_iota(I32, (LANES, 2 * LANES), 1) // LANES,
                       1.0, 0.0).astype(BF16))


def _gla_pair(p, la, rows, c, gla_ref, hn_ref, y_ref, sst):
    L, R = GLA_L, GLA_R
    WK = N_HEADS * GLA_DK
    WV = N_HEADS * DH
    dot = functools.partial(jnp.dot, preferred_element_type=F32)
    nt = (((1,), (1,)), ((), ()))
    tn = (((0,), (0,)), ((), ()))
    lane, rowl, lane_r, row_r = c['lane'], c['rowl'], c['lane_r'], c['row_r']

    bc = _cumsum_rows(c['tri_bf'], la[:, p * LANES:(p + 1) * LANES])
    yield

    q2 = gla_ref[rows, p * LANES:(p + 1) * LANES].astype(F32) * (GLA_DK ** -0.5)
    k2 = gla_ref[rows, WK + p * LANES:WK + (p + 1) * LANES].astype(F32)
    g_last = bc[L - 1:L, :]
    q_in = q2 * jnp.exp(bc)
    kd = k2 * jnp.exp(g_last - bc)
    decay = jnp.exp(g_last)

    k_off = [None]
    q_off = [None]
    for j in range(1, L // R):
        rj = bc[j * R:j * R + 1, :]
        k_off.append((k2 * jnp.exp(jnp.where(rowl < j * R, rj - bc, NEG_INF))).astype(BF16))
        q_off.append(q2[j * R:(j + 1) * R, :] * jnp.exp(bc[j * R:(j + 1) * R, :] - rj))

    diag = []
    for j in range(L // R):
        qb = q2[j * R:(j + 1) * R, :]
        bq = bc[j * R:(j + 1) * R, :]
        terms = []
        for s in range(R):
            krow = k2[j * R + s:j * R + s + 1, :]
            brow = bc[j * R + s:j * R + s + 1, :]
            d = jnp.where(row_r >= s, bq - brow, NEG_INF)
            terms.append((qb * krow * jnp.exp(d)).astype(BF16))
        diag.append(dot(jnp.concatenate(terms, axis=0), c['wsel']))

    heads = []
    for hh in range(2):
        h = 2 * p + hh
        mh = (lane // GLA_DK) == hh
        mq = (lane_r // GLA_DK) == hh
        v = gla_ref[rows, 2 * WK + h * DH:2 * WK + (h + 1) * DH]
        st = sst[h]
        o_inter = lax.dot_general(jnp.where(mh, q_in, 0.0).astype(BF16), st.astype(BF16), nt,
                                  preferred_element_type=F32)
        upd = lax.dot_general(v, jnp.where(mh, kd, 0.0).astype(BF16), tn, preferred_element_type=F32)
        offs = [None] + [lax.dot_general(jnp.where(mq, q_off[j], 0.0).astype(BF16), k_off[j], nt,
                                         preferred_element_type=F32) for j in range(1, L // R)]
        heads.append((h, v, st, o_inter, upd, offs))
    yield

    outs = []
    for hh, (h, v, st, o_inter, upd, offs) in enumerate(heads):
        sst[h] = st * decay + upd
        a_rows = []
        for j in range(L // R):
            res = diag[j][:, hh * LANES:(hh + 1) * LANES]
            blk = jnp.zeros((R, LANES), F32)
            for s in range(R):
                blk = jnp.where(lane_r == j * R + s, res[s * R:(s + 1) * R, :], blk)
            blk = blk[:, 0:L]
            if j > 0:
                blk = blk + offs[j]
            a_rows.append(blk)
        a_mat = jnp.concatenate(a_rows, axis=0).astype(BF16)
        outs.append((h, o_inter, dot(a_mat, v)))
    yield

    for h, o_inter, o_intra in outs:
        o = o_inter + o_intra
        on = o * lax.rsqrt(jnp.mean(o * o, axis=-1, keepdims=True) + EPS)
        z = gla_ref[rows, 2 * WK + WV + h * DH:2 * WK + WV + (h + 1) * DH].astype(F32)
        y_ref[rows, h * DH:(h + 1) * DH] = (
            on * hn_ref[:, h * DH:(h + 1) * DH] * (z * _sigmoid(z))).astype(BF16)
    yield


def _gla_body(gla_ref, gate_ref, wgg_ref, bgg_ref, hn_ref, y_ref, *ssts, n_chunks):
    nb = gla_ref.shape[0]

    @pl.when(pl.program_id(1) == 0)
    def _():
        for bb in range(nb):
            ssts[bb][...] = jnp.zeros(ssts[bb].shape, F32)

    consts = _gla_consts()

    def step(i, carry):
        rows = pl.ds(pl.multiple_of(i * GLA_L, GLA_L), GLA_L)
        chains = []
        for bb in range(nb):
            la = _log_sigmoid(jnp.dot(gate_ref[bb, rows, :].astype(BF16), wgg_ref[...],
                                      preferred_element_type=F32) + bgg_ref[...]) / GLA_GATE_NORM
            chains += [_gla_pair(p, la, rows, consts, gla_ref.at[bb], hn_ref, y_ref.at[bb], ssts[bb])
                       for p in range(N_HEADS // 2)]
        for _ in range(GLA_STAGES):
            for ch in chains:
                next(ch)
        return carry

    lax.fori_loop(0, n_chunks, step, 0)


def _gla(gla, gates, wgg, bgg, head_norm, B, S, lt, nb):
    WK = N_HEADS * GLA_DK
    WV = N_HEADS * DH
    tok = lambda b, c: (b, c, 0)
    const = lambda b, c: (0, 0)
    return pl.pallas_call(
        functools.partial(_gla_body, n_chunks=lt // GLA_L),
        grid=(B // nb, S // lt),
        in_specs=[pl.BlockSpec((nb, lt, 2 * WK + 2 * WV), tok), pl.BlockSpec((nb, lt, LANES), tok),
                  pl.BlockSpec((LANES, WK), const), pl.BlockSpec((1, WK), const),
                  pl.BlockSpec((1, WV), const)],
        out_specs=pl.BlockSpec((nb, lt, WV), tok),
        out_shape=jax.ShapeDtypeStruct((B, S, WV), BF16),
        scratch_shapes=[pltpu.VMEM((N_HEADS, DH, LANES), F32)] * nb,
        compiler_params=_cparams(("arbitrary", "arbitrary")),
        name="gla",
    )(gla.reshape(B, S, 2 * WK + 2 * WV), gates.reshape(B, S, LANES), wgg, bgg,
      head_norm).reshape(B * S, WV)


def _outproj_router_body(ym_ref, yg_ref, h_ref, wo_ref, g_ref, wr_ref, br_ref,
                         hout_ref, xn_ref, route_ref, cnt_ref, carry):
    tm = h_ref.shape[0]
    W = ym_ref.shape[1]
    dot = functools.partial(jnp.dot, preferred_element_type=F32)

    @pl.when(pl.program_id(0) == 0)
    def _():
        carry[...] = jnp.zeros(carry.shape, F32)

    hnew = h_ref[...] + dot(ym_ref[...], wo_ref[0:W, :]) + dot(yg_ref[...], wo_ref[W:2 * W, :])
    hout_ref[...] = hnew
    xn = hnew * lax.rsqrt(jnp.mean(hnew * hnew, axis=-1, keepdims=True) + EPS) * g_ref[...]
    xn_ref[...] = xn

    xh = xn.astype(BF16)
    xm = (xn - xh.astype(F32)).astype(BF16)
    l2 = dot(xh, wr_ref[...])
    logits = l2[:, 0:LANES] + (l2[:, LANES:2 * LANES] + dot(xm, wr_ref[:, 0:LANES])) + br_ref[...]

    lane = lax.broadcasted_iota(I32, (tm, LANES), 1)
    lane_f = lane.astype(F32)
    big = float(LANES)
    gl = jnp.where(lane < N_GROUPS, logits, NEG_INF)
    gmax = jnp.max(gl, axis=-1, keepdims=True)
    gsel = jnp.min(jnp.where(gl == gmax, lane_f, big), axis=-1, keepdims=True)
    g_gate = 1.0 / jnp.sum(jnp.where(lane < N_GROUPS, jnp.exp(logits - gmax), 0.0),
                           axis=-1, keepdims=True)
    in_grp = ((lane >= EXP_LANE0) & (lane < EXP_LANE0 + N_EXPERTS)
              & (((lane - EXP_LANE0) // EPG).astype(F32) == gsel))
    el = jnp.where(in_grp, logits, NEG_INF)
    v1 = jnp.max(el, axis=-1, keepdims=True)
    i1 = jnp.min(jnp.where(el == v1, lane_f, big), axis=-1, keepdims=True)
    el2 = jnp.where(lane_f == i1, NEG_INF, el)
    v2 = jnp.max(el2, axis=-1, keepdims=True)
    i2 = jnp.min(jnp.where(el2 == v2, lane_f, big), axis=-1, keepdims=True)
    e2 = jnp.exp(v2 - v1)
    w1 = g_gate / (1.0 + e2)
    w2 = g_gate * e2 / (1.0 + e2)

    oh1 = lane_f == i1
    oh2 = lane_f == i2
    oh = jnp.where(oh1, 1.0, 0.0) + jnp.where(oh2, 1.0, 0.0)
    r_i = lax.broadcasted_iota(I32, (tm, tm), 0)
    c_i = lax.broadcasted_iota(I32, (tm, tm), 1)
    strict = jnp.where(c_i < r_i, 1.0, 0.0).astype(BF16)
    before = dot(strict, oh.astype(BF16)) + carry[0:1, :]
    rank1 = jnp.sum(jnp.where(oh1, before, 0.0), axis=-1, keepdims=True)
    rank2 = jnp.sum(jnp.where(oh2, before, 0.0), axis=-1, keepdims=True)
    total = carry[0:1, :] + jnp.sum(oh, axis=0, keepdims=True)
    carry[...] = jnp.broadcast_to(total, carry.shape)
    cnt_ref[...] = jnp.broadcast_to(total, cnt_ref.shape)

    rec = jnp.where(lane == 0, i1 - EXP_LANE0, 0.0)
    rec = jnp.where(lane == 1, i2 - EXP_LANE0, rec)
    rec = jnp.where(lane == 2, w1, rec)
    rec = jnp.where(lane == 3, w2, rec)
    rec = jnp.where(lane == 4, rank1, rec)
    rec = jnp.where(lane == 5, rank2, rec)
    route_ref[...] = rec


def _outproj_router(ym, yg, h2d, wo, gain, wr3, br, tm):
    T, D = h2d.shape
    W = ym.shape[1]
    row = lambda i: (i, 0)
    const = lambda i: (0, 0)
    return pl.pallas_call(
        _outproj_router_body,
        grid=(T // tm,),
        in_specs=[pl.BlockSpec((tm, W), row), pl.BlockSpec((tm, W), row), pl.BlockSpec((tm, D), row),
                  pl.BlockSpec((2 * W, D), const), pl.BlockSpec((1, D), const),
                  pl.BlockSpec((D, 2 * LANES), const), pl.BlockSpec((1, LANES), const)],
        out_specs=[pl.BlockSpec((tm, D), row), pl.BlockSpec((tm, D), row),
                   pl.BlockSpec((tm, LANES), row), pl.BlockSpec((SUBLANES, LANES), const)],
        out_shape=[jax.ShapeDtypeStruct((T, D), F32), jax.ShapeDtypeStruct((T, D), F32),
                   jax.ShapeDtypeStruct((T, LANES), F32),
                   jax.ShapeDtypeStruct((SUBLANES, LANES), F32)],
        scratch_shapes=[pltpu.VMEM((SUBLANES, LANES), F32)],
        compiler_params=_cparams(("arbitrary",)),
        name="outproj_router",
    )(ym, yg, h2d, wo, gain, wr3, br)


def _dispatch_body(pos_ref, pad_ref, x_hbm, xs_hbm, xbuf, zbuf, load_sem, scat_sem, pad_sem,
                   *, td, n_tiles):
    i = pl.program_id(0)
    ns = xbuf.shape[0]
    slot = lax.rem(i, ns)

    def load(tile, s):
        rows = pl.ds(pl.multiple_of(tile * td, td), td)
        return pltpu.make_async_copy(x_hbm.at[rows], xbuf.at[s], load_sem.at[s])

    def wait_scatter(s):
        for _ in range(2):
            pltpu.make_async_copy(xbuf.at[s], xs_hbm.at[pl.ds(0, td)], scat_sem.at[s]).wait()

    def for_each_pad_copy(fn):
        def per_expert(e, carry):
            first = pad_ref[0, 0, e]
            count = pad_ref[0, 0, N_EXPERTS + e]
            lead = jnp.minimum(count, lax.rem(SUBLANES - lax.rem(first, SUBLANES), SUBLANES))

            def per_row(r, c):
                fn(pltpu.make_async_copy(zbuf.at[pl.ds(0, 1)], xs_hbm.at[pl.ds(first + r, 1)], pad_sem))
                return c

            def per_group(g, c):
                rows = pl.ds(pl.multiple_of(first + lead + g * SUBLANES, SUBLANES), SUBLANES)
                fn(pltpu.make_async_copy(zbuf.at[pl.ds(0, SUBLANES)], xs_hbm.at[rows], pad_sem))
                return c

            carry = lax.fori_loop(0, lead, per_row, carry)
            return lax.fori_loop(0, (count - lead) // SUBLANES, per_group, carry)

        lax.fori_loop(0, N_EXPERTS, per_expert, 0)

        def per_block(bk, carry):
            rows = pl.ds(pl.multiple_of(bk * MOE_BLOCK, MOE_BLOCK), MOE_BLOCK)
            fn(pltpu.make_async_copy(zbuf, xs_hbm.at[rows], pad_sem))
            return carry

        lax.fori_loop(pad_ref[0, 0, 2 * N_EXPERTS], xs_hbm.shape[0] // MOE_BLOCK, per_block, 0)

    @pl.when(i == 0)
    def _():
        load(0, 0).start()
        zbuf[...] = jnp.zeros(zbuf.shape, F32)
        for_each_pad_copy(lambda cp: cp.start())

    @pl.when(i + 1 < n_tiles)
    def _():
        load(i + 1, lax.rem(i + 1, ns)).start()

    load(i, slot).wait()

    def issue(jb, carry):
        for u in range(ROW_UNROLL):
            j = jb * ROW_UNROLL + u
            for k in range(2):
                pltpu.make_async_copy(xbuf.at[slot, pl.ds(j, 1)],
                                      xs_hbm.at[pl.ds(pos_ref[0, 0, 2 * j + k], 1)],
                                      scat_sem.at[slot]).start(priority=k)
        return carry

    lax.fori_loop(0, td // ROW_UNROLL, issue, 0)

    @pl.when(i > 0)
    def _():
        wait_scatter(lax.rem(i + ns - 1, ns))

    @pl.when(i == n_tiles - 1)
    def _():
        wait_scatter(slot)
        for_each_pad_copy(lambda cp: cp.wait())


def _dispatch(pos, pads, xn, n_rows, td):
    T, D = xn.shape
    n = T // td
    return pl.pallas_call(
        functools.partial(_dispatch_body, td=td, n_tiles=n),
        grid=(n,),
        in_specs=[pl.BlockSpec((1, 1, 2 * td), lambda i: (i, 0, 0), memory_space=pltpu.SMEM),
                  pl.BlockSpec((1, 1, 2 * N_EXPERTS + 1), lambda i: (0, 0, 0),
                               memory_space=pltpu.SMEM),
                  pl.BlockSpec(memory_space=pl.ANY)],
        out_specs=pl.BlockSpec(memory_space=pl.ANY),
        out_shape=jax.ShapeDtypeStruct((n_rows, D), F32),
        scratch_shapes=[pltpu.VMEM((3, td, D), F32), pltpu.VMEM((MOE_BLOCK, D), F32),
                        pltpu.SemaphoreType.DMA((3,)), pltpu.SemaphoreType.DMA((3,)),
                        pltpu.SemaphoreType.DMA(())],
        compiler_params=_cparams(("arbitrary",)),
        name="dispatch",
    )(pos.reshape(n, 1, 2 * td), pads.reshape(1, 1, 2 * N_EXPERTS + 1), xn)


def _ffn_body(blk_exp_ref, nused_ref, xs_ref, wgu_ref, wd_ref, ys_ref, wgu_bf, wd_bf):
    b = pl.program_id(0)
    de = wd_ref.shape[2]
    used = b < nused_ref[0]
    new_expert = (b == 0) | (blk_exp_ref[b] != blk_exp_ref[jnp.maximum(b - 1, 0)])

    @pl.when(used & new_expert)
    def _():
        wgu_bf[...] = wgu_ref[0, 0].astype(BF16)
        wd_bf[...] = wd_ref[0, 0].astype(BF16)

    @pl.when(used)
    def _():
        gu = jnp.dot(xs_ref[...].astype(BF16), wgu_bf[...], preferred_element_type=F32)
        gate, up = gu[:, 0:de], gu[:, de:2 * de]
        act = (gate * _sigmoid(gate) * up).astype(BF16)
        ys_ref[...] = jnp.dot(act, wd_bf[...], preferred_element_type=F32)

    @pl.when(jnp.logical_not(used))
    def _():
        ys_ref[...] = jnp.zeros(ys_ref.shape, F32)


def _ffn(blk_exp, nused, xs, wgu, wd, layer):
    NR, D = xs.shape
    nb = NR // MOE_BLOCK
    de = wd.shape[2]
    grid_spec = pltpu.PrefetchScalarGridSpec(
        num_scalar_prefetch=2,
        grid=(nb,),
        in_specs=[pl.BlockSpec((MOE_BLOCK, D), lambda b, be, nu: (jnp.minimum(b, nu[0] - 1), 0)),
                  pl.BlockSpec((1, 1, D, 2 * de), lambda b, be, nu: (layer, be[b], 0, 0)),
                  pl.BlockSpec((1, 1, de, D), lambda b, be, nu: (layer, be[b], 0, 0))],
        out_specs=pl.BlockSpec((MOE_BLOCK, D), lambda b, be, nu: (b, 0)),
        scratch_shapes=[pltpu.VMEM((D, 2 * de), BF16), pltpu.VMEM((de, D), BF16)],
    )
    return pl.pallas_call(
        _ffn_body,
        grid_spec=grid_spec,
        out_shape=jax.ShapeDtypeStruct((NR, D), F32),
        compiler_params=_cparams(("arbitrary",)),
        name="expert_ffn",
    )(blk_exp, nused, xs, wgu, wd)


def _combine_body(pos_ref, posn_ref, ys_hbm, h_ref, route_ref, gfin_ref, out_ref, gbuf, sem,
                  *, tc, n_tiles, final):
    i = pl.program_id(0)
    slot = lax.rem(i, 2)

    def issue_tile(p_ref, s):
        def issue(jb, carry):
            for u in range(ROW_UNROLL):
                j = jb * ROW_UNROLL + u
                for k in range(2):
                    pltpu.make_async_copy(ys_hbm.at[pl.ds(p_ref[0, 0, 2 * j + k], 1)],
                                          gbuf.at[s, k, pl.ds(j, 1)], sem.at[s]).start(priority=k)
            return carry

        lax.fori_loop(0, tc // ROW_UNROLL, issue, 0)

    @pl.when(i == 0)
    def _():
        issue_tile(pos_ref, 0)

    @pl.when(i + 1 < n_tiles)
    def _():
        issue_tile(posn_ref, lax.rem(i + 1, 2))

    for k in range(2):
        pltpu.make_async_copy(ys_hbm.at[pl.ds(0, tc)], gbuf.at[slot, k], sem.at[slot]).wait()

    w1 = route_ref[:, 2:3]
    w2 = route_ref[:, 3:4]
    out = h_ref[...] + (w1 * gbuf[slot, 0] + w2 * gbuf[slot, 1])
    if final:
        out = out * lax.rsqrt(jnp.mean(out * out, axis=-1, keepdims=True) + EPS) * gfin_ref[...]
    out_ref[...] = out


def _combine(pos, ys, h2d, route, gfin, tc, final):
    T, D = h2d.shape
    n = T // tc
    pos3 = pos.reshape(n, 1, 2 * tc)
    row = lambda i: (i, 0)
    smem = functools.partial(pl.BlockSpec, (1, 1, 2 * tc), memory_space=pltpu.SMEM)
    return pl.pallas_call(
        functools.partial(_combine_body, tc=tc, n_tiles=n, final=final),
        grid=(n,),
        in_specs=[smem(index_map=lambda i: (i, 0, 0)),
                  smem(index_map=lambda i: (jnp.minimum(i + 1, n - 1), 0, 0)),
                  pl.BlockSpec(memory_space=pl.ANY),
                  pl.BlockSpec((tc, D), row), pl.BlockSpec((tc, LANES), row),
                  pl.BlockSpec((1, D), lambda i: (0, 0))],
        out_specs=pl.BlockSpec((tc, D), row),
        out_shape=jax.ShapeDtypeStruct((T, D), F32),
        scratch_shapes=[pltpu.VMEM((2, 2, tc, D), F32), pltpu.SemaphoreType.DMA((2,))],
        compiler_params=_cparams(("arbitrary",)),
        name="combine",
    )(pos3, pos3, ys, h2d, route, gfin)


def _pick_tile(n, pref):
    t = min(pref, n)
    while n % t:
        t //= 2
    return t


def _prep_layer(w_in, b_mlstm_gate, w_gla_gate, b_gla_gate, w_out, w_group, b_group, w_expert,
                b_expert):
    W = N_HEADS * DH
    WK = N_HEADS * GLA_DK
    D = w_in.shape[0]
    o_gates = 4 * W
    o_gla = o_gates + 2 * N_HEADS
    o_r = o_gla + 2 * WK + 2 * W
    wm = w_in[:, 0:o_gates].astype(BF16)
    wg = w_in[:, o_gla:o_r].astype(BF16)
    ws = jnp.zeros((D, LANES), F32)
    ws = ws.at[:, 0:2 * N_HEADS].set(w_in[:, o_gates:o_gla])
    ws = ws.at[:, 2 * N_HEADS:2 * N_HEADS + GLA_RANK].set(w_in[:, o_r:o_r + GLA_RANK]).astype(BF16)
    bg = jnp.zeros((1, LANES), F32).at[0, 0:2 * N_HEADS].set(b_mlstm_gate)
    wgg = jnp.zeros((LANES, WK), F32).at[2 * N_HEADS:2 * N_HEADS + GLA_RANK, :].set(w_gla_gate)
    wr = jnp.zeros((D, LANES), F32)
    wr = wr.at[:, 0:N_GROUPS].set(w_group).at[:, EXP_LANE0:EXP_LANE0 + N_EXPERTS].set(w_expert)
    br = jnp.zeros((1, LANES), F32)
    br = br.at[0, 0:N_GROUPS].set(b_group).at[0, EXP_LANE0:EXP_LANE0 + N_EXPERTS].set(b_expert)
    return dict(wm=wm, wg=wg, ws=ws, bg=bg, wgg=wgg.astype(BF16), bgg=b_gla_gate[None, :],
                wo=w_out.astype(BF16), wr3=jnp.concatenate(_split3(wr)[0:2], axis=1), br=br)


def _routing_tables(route, counts_row, T):
    eid = route[:, 0:2].astype(I32)
    rank = route[:, 4:6].astype(I32)
    counts = counts_row[EXP_LANE0:EXP_LANE0 + N_EXPERTS].astype(I32)
    pcounts = (counts + MOE_BLOCK - 1) // MOE_BLOCK * MOE_BLOCK
    pend = jnp.cumsum(pcounts)
    pstart = pend - pcounts
    onehot = eid[:, :, None] == jnp.arange(N_EXPERTS, dtype=I32)[None, None, :]
    pos = jnp.sum(jnp.where(onehot, pstart[None, None, :], 0), axis=-1) + rank
    n_blocks = (T * 2 + MOE_BLOCK - 1) // MOE_BLOCK + N_EXPERTS
    blk_start = jnp.arange(n_blocks, dtype=I32) * MOE_BLOCK
    blk_exp = jnp.minimum(jnp.sum((pend[None, :] <= blk_start[:, None]).astype(I32), axis=1),
                          N_EXPERTS - 1)
    nused = (pend[-1] // MOE_BLOCK).astype(I32).reshape(1)
    pads = jnp.concatenate([pstart + counts, pcounts - counts, nused])
    return pos, pads, blk_exp, nused, n_blocks


def kernel(x, norm_mix, w_in, conv_w, conv_b, b_mlstm_gate, w_gla_gate, b_gla_gate, head_norm, w_out, norm_ffn, w_group, b_group, w_expert, b_expert, w_gu, w_down, norm_final):
    B, S, D = x.shape
    T = B * S
    depth = w_in.shape[0]
    W = N_HEADS * DH
    tm = _pick_tile(T, 512)
    td = _pick_tile(T, 512)
    tc = _pick_tile(T, 256)
    lt = _pick_tile(S, 256)
    nb = _pick_tile(B, 2)
    h = x.reshape(T, D)
    pending = None
    for l in range(depth):
        p = _prep_layer(w_in[l], b_mlstm_gate[l], w_gla_gate[l], b_gla_gate[l], w_out[l],
                        w_group[l], b_group[l], w_expert[l], b_expert[l])
        if pending is None:
            main, gla, gates = _inproj(h, norm_mix[l][None, :], p['wm'], p['wg'], p['ws'], tm)
        else:
            h, main, gla, gates = _inproj_combine(*pending, norm_mix[l][None, :], p['wm'], p['wg'],
                                                  p['ws'], tm)
        ym = _mlstm(main, gates, conv_w[l], conv_b[l][None, :], p['bg'], head_norm[l][None, 0:W], B, S,
                    nb)
        yg = _gla(gla, gates, p['wgg'], p['bgg'], head_norm[l][None, W:2 * W], B, S, lt, nb)
        h, xn, route, cnt = _outproj_router(ym, yg, h, p['wo'], norm_ffn[l][None, :], p['wr3'],
                                            p['br'], tm)
        pos, pads, blk_exp, nused, n_blocks = _routing_tables(route, cnt[0], T)
        xs = _dispatch(pos, pads, xn, n_blocks * MOE_BLOCK, td)
        ys = _ffn(blk_exp, nused, xs, w_gu, w_down, l)
        pending = (pos, ys, h, route)
    pos, ys, h, route = pending
    h = _combine(pos, ys, h, route, norm_final[None, :], tc, final=True)
    return h.reshape(B, S, D)
```

```python
import functools

import jax
import jax.numpy as jnp
from jax import lax
from jax.experimental import pallas as pl
from jax.experimental.pallas import tpu as pltpu

F32 = jnp.float32
BF16 = jnp.bfloat16
I32 = jnp.int32

EPS = 1e-6
LANES = 128
SUBLANES = 8
VMEM_LIMIT = 56 * 1024 * 1024

N_HEADS = 4
DH = 128
GLA_DK = 64
GLA_RANK = 16
GLA_GATE_NORM = 16.0
CONV_W = 4
MLSTM_L = 128
GLA_L = 64
GLA_R = 16
N_GROUPS = 4
EPG = 8
N_EXPERTS = N_GROUPS * EPG
MOE_BLOCK = 512
EXP_LANE0 = N_GROUPS

ROW_UNROLL = 8

NEG_INF = float("-inf")


def _cparams(sem):
    return pltpu.CompilerParams(dimension_semantics=sem, vmem_limit_bytes=VMEM_LIMIT)


def _split3(x):
    hi = x.astype(BF16)
    r1 = x - hi.astype(F32)
    mid = r1.astype(BF16)
    lo = (r1 - mid.astype(F32)).astype(BF16)
    return hi, mid, lo


def _cumsum_rows(tri_bf, x):
    hi, mid, lo = _split3(x)
    dot = functools.partial(jnp.dot, preferred_element_type=F32)
    return dot(tri_bf, hi) + dot(tri_bf, mid) + dot(tri_bf, lo)


def _sigmoid(x):
    return 1.0 / (1.0 + jnp.exp(-x))


def _log_sigmoid(x):
    return jnp.minimum(x, 0.0) - jnp.log(1.0 + jnp.exp(-jnp.abs(x)))


def _inproj_body(x_ref, g_ref, wm_ref, wg_ref, ws_ref, om_ref, og_ref, os_ref):
    x = x_ref[...]
    xn = x * lax.rsqrt(jnp.mean(x * x, axis=-1, keepdims=True) + EPS) * g_ref[...]
    xb = xn.astype(BF16)
    om_ref[...] = jnp.dot(xb, wm_ref[...], preferred_element_type=F32).astype(BF16)
    og_ref[...] = jnp.dot(xb, wg_ref[...], preferred_element_type=F32).astype(BF16)
    os_ref[...] = jnp.dot(xb, ws_ref[...], preferred_element_type=F32)


def _inproj(h2d, gain, wm, wg, ws, tm):
    T, D = h2d.shape
    nm, ng, ns = wm.shape[1], wg.shape[1], ws.shape[1]
    const = lambda i: (0, 0)
    row = lambda i: (i, 0)
    return pl.pallas_call(
        _inproj_body,
        grid=(T // tm,),
        in_specs=[pl.BlockSpec((tm, D), row), pl.BlockSpec((1, D), const),
                  pl.BlockSpec((D, nm), const), pl.BlockSpec((D, ng), const),
                  pl.BlockSpec((D, ns), const)],
        out_specs=[pl.BlockSpec((tm, nm), row), pl.BlockSpec((tm, ng), row),
                   pl.BlockSpec((tm, ns), row)],
        out_shape=[jax.ShapeDtypeStruct((T, nm), BF16), jax.ShapeDtypeStruct((T, ng), BF16),
                   jax.ShapeDtypeStruct((T, ns), F32)],
        compiler_params=_cparams(("parallel",)),
        name="inproj",
    )(h2d, gain, wm, wg, ws)


def _inproj_combine_body(pos_ref, posn_ref, ys_hbm, h_ref, route_ref, g_ref, wm_ref, wg_ref, ws_ref,
                         hout_ref, om_ref, og_ref, os_ref, gbuf, sem, *, n_tiles):
    tm = h_ref.shape[0]
    i = pl.program_id(0)
    slot = lax.rem(i, 2)

    def row_copy(p_ref, s, j, k):
        return pltpu.make_async_copy(ys_hbm.at[pl.ds(p_ref[0, 0, 2 * j + k], 1)],
                                     gbuf.at[s, k, pl.ds(j, 1)], sem.at[s])

    def wait_tile(s):
        for k in range(2):
            pltpu.make_async_copy(ys_hbm.at[pl.ds(0, tm)], gbuf.at[s, k], sem.at[s]).wait()

    @pl.when(i == 0)
    def _():
        def issue(jb, carry):
            for u in range(ROW_UNROLL):
                for k in range(2):
                    row_copy(pos_ref, 0, jb * ROW_UNROLL + u, k).start(priority=k)
            return carry

        lax.fori_loop(0, tm // ROW_UNROLL, issue, 0)

    wait_tile(slot)
    x = h_ref[...] + (route_ref[:, 2:3] * gbuf[slot, 0] + route_ref[:, 3:4] * gbuf[slot, 1])
    hout_ref[...] = x

    for j in range(tm):
        for k in range(2):
            row_copy(posn_ref, 1 - slot, j, k).start(priority=k)

    xn = x * lax.rsqrt(jnp.mean(x * x, axis=-1, keepdims=True) + EPS) * g_ref[...]
    xb = xn.astype(BF16)
    om_ref[...] = jnp.dot(xb, wm_ref[...], preferred_element_type=F32).astype(BF16)
    og_ref[...] = jnp.dot(xb, wg_ref[...], preferred_element_type=F32).astype(BF16)
    os_ref[...] = jnp.dot(xb, ws_ref[...], preferred_element_type=F32)

    @pl.when(i == n_tiles - 1)
    def _():
        wait_tile(1 - slot)


def _inproj_combine(pos, ys, h2d, route, gain, wm, wg, ws, tm):
    T, D = h2d.shape
    n = T // tm
    nm, ng, ns = wm.shape[1], wg.shape[1], ws.shape[1]
    const = lambda i: (0, 0)
    row = lambda i: (i, 0)
    pos3 = pos.reshape(n, 1, 2 * tm)
    smem = functools.partial(pl.BlockSpec, (1, 1, 2 * tm), memory_space=pltpu.SMEM)
    return pl.pallas_call(
        functools.partial(_inproj_combine_body, n_tiles=n),
        grid=(n,),
        in_specs=[smem(index_map=lambda i: (i, 0, 0)),
                  smem(index_map=lambda i: (jnp.minimum(i + 1, n - 1), 0, 0)),
                  pl.BlockSpec(memory_space=pl.ANY),
                  pl.BlockSpec((tm, D), row), pl.BlockSpec((tm, LANES), row), pl.BlockSpec((1, D), const),
                  pl.BlockSpec((D, nm), const), pl.BlockSpec((D, ng), const),
                  pl.BlockSpec((D, ns), const)],
        out_specs=[pl.BlockSpec((tm, D), row), pl.BlockSpec((tm, nm), row), pl.BlockSpec((tm, ng), row),
                   pl.BlockSpec((tm, ns), row)],
        out_shape=[jax.ShapeDtypeStruct((T, D), F32), jax.ShapeDtypeStruct((T, nm), BF16),
                   jax.ShapeDtypeStruct((T, ng), BF16), jax.ShapeDtypeStruct((T, ns), F32)],
        scratch_shapes=[pltpu.VMEM((2, 2, tm, D), F32), pltpu.SemaphoreType.DMA((2,))],
        compiler_params=_cparams(("arbitrary",)),
        name="inproj_combine",
    )(pos3, pos3, ys, h2d, route, gain, wm, wg, ws)


def _mlstm_body(main_ref, gate_ref, cw_ref, cb_ref, bg_ref, hn_ref, y_ref, *scratch):
    L = MLSTM_L
    W = N_HEADS * DH
    TAIL = 2 * SUBLANES
    nb = main_ref.shape[0]
    ubufs, qkbufs, csts, msts = (scratch[i * nb:(i + 1) * nb] for i in range(4))
    dot = functools.partial(jnp.dot, preferred_element_type=F32)

    @pl.when(pl.program_id(1) == 0)
    def _():
        for bb in range(nb):
            ubufs[bb][0:TAIL, :] = jnp.zeros((TAIL, 2 * W), BF16)
            csts[bb][...] = jnp.zeros(csts[bb].shape, F32)
            msts[bb][...] = jnp.zeros(msts[bb].shape, F32)

    row = lax.broadcasted_iota(I32, (L, L), 0)
    col = lax.broadcasted_iota(I32, (L, L), 1)
    tri = col <= row
    tri_bf = jnp.where(tri, 1.0, 0.0).astype(BF16)
    lane = lax.broadcasted_iota(I32, (L, LANES), 1)
    sh_r = lax.broadcasted_iota(I32, ((CONV_W - 1) * L, L + TAIL), 0)
    sh_c = lax.broadcasted_iota(I32, ((CONV_W - 1) * L, L + TAIL), 1)
    shift = jnp.where(sh_c == (sh_r % L) + TAIL - (CONV_W - 1) + sh_r // L, 1.0, 0.0).astype(BF16)
    rp_r = lax.broadcasted_iota(I32, (LANES, 2 * N_HEADS * LANES), 0)
    rp_c = lax.broadcasted_iota(I32, (LANES, 2 * N_HEADS * LANES), 1)
    rep = jnp.where(rp_r == rp_c // LANES, 1.0, 0.0).astype(BF16)

    gates = []
    for bb in range(nb):
        ubuf = ubufs[bb]
        u = main_ref[bb, :, 0:2 * W]
        ubuf[TAIL:TAIL + L, :] = u
        shifted = dot(shift, ubuf[...])
        ubuf[0:TAIL, :] = ubuf[L:L + TAIL, :]
        acc = cb_ref[...] + u.astype(F32) * cw_ref[CONV_W - 1:CONV_W, :]
        for i in range(CONV_W - 1):
            acc = acc + shifted[i * L:(i + 1) * L, :] * cw_ref[i:i + 1, :]
        qk = acc * _sigmoid(acc)
        qkbufs[bb][:, 0:W] = qk[:, 0:W]
        qkbufs[bb][:, W:2 * W] = qk[:, W:2 * W] * (DH ** -0.5)

        gpre = gate_ref[bb] + bg_ref[...]
        xg = jnp.where(lane < N_HEADS, gpre, _log_sigmoid(gpre))
        gm = jnp.where(lane < N_HEADS, xg, _cumsum_rows(tri_bf, xg))
        hi, mid, lo = _split3(gm)
        gates.append((dot(hi, rep) + dot(mid, rep) + dot(lo, rep), gm.T))

    chains = [_mlstm_head(h, main_ref.at[bb], hn_ref, y_ref.at[bb], qkbufs[bb], csts[bb], msts[bb],
                          gates[bb], tri)
              for h in range(N_HEADS) for bb in range(nb)]
    for _ in range(MLSTM_STAGES):
        for c in chains:
            next(c)


MLSTM_STAGES = 3


def _mlstm_head(h, main_ref, hn_ref, y_ref, qkbuf, cst, mst, gates, tri):
    L = MLSTM_L
    W = N_HEADS * DH
    dot = functools.partial(jnp.dot, preferred_element_type=F32)
    g_rep, g_t = gates
    hs = slice(h * DH, (h + 1) * DH)

    q = qkbuf[:, hs].astype(BF16)
    k = qkbuf[:, W + h * DH:W + (h + 1) * DH]
    v = main_ref[:, 2 * W + h * DH:2 * W + (h + 1) * DH]
    vext = jnp.concatenate([v, jnp.ones((L, DH), BF16)], axis=1)
    cext = cst[h]
    s_raw = lax.dot_general(q, k.astype(BF16), (((1,), (1,)), ((), ())), preferred_element_type=F32)
    qc = dot(q, cext.astype(BF16))
    yield

    ig_rep = g_rep[:, h * LANES:(h + 1) * LANES]
    b_rep = g_rep[:, (N_HEADS + h) * LANES:(N_HEADS + h + 1) * LANES]
    ig_row = g_t[h:h + 1, :]
    b_row = g_t[N_HEADS + h:N_HEADS + h + 1, :]
    m_prev = mst[h]
    logd = jnp.where(tri, b_rep - b_row + ig_row, NEG_INF)
    m_inter = b_rep + m_prev
    m_j = jnp.maximum(m_inter, jnp.max(logd, axis=-1, keepdims=True))
    s = s_raw * jnp.exp(logd - m_j)
    a = jnp.exp(m_inter - m_j)
    sv = dot(s.astype(BF16), vext)
    g = b_rep[L - 1:L, :]
    m_new = jnp.maximum(g + m_prev, jnp.max(g - b_row + ig_row, axis=-1, keepdims=True))
    kw = (k * jnp.exp(g - b_rep + ig_rep - m_new)).astype(BF16)
    decay = jnp.exp(g + m_prev - m_new)
    upd = lax.dot_general(kw, vext, (((0,), (0,)), ((), ())), preferred_element_type=F32)
    yield

    num = sv[:, 0:DH] + a * qc[:, 0:DH]
    den = sv[:, DH:2 * DH] + a * qc[:, DH:2 * DH]
    hh = num / jnp.maximum(jnp.abs(den), jnp.exp(-m_j))
    cst[h] = jnp.concatenate([decay, decay], axis=1) * cext + upd
    mst[h] = m_new
    hn = hh * lax.rsqrt(jnp.mean(hh * hh, axis=-1, keepdims=True) + EPS)
    o_gate = _sigmoid(main_ref[:, 3 * W + h * DH:3 * W + (h + 1) * DH].astype(F32))
    y_ref[:, hs] = (hn * hn_ref[:, hs] * o_gate).astype(BF16)
    yield


def _mlstm(main, gates, conv_w, conv_b, b_gate, head_norm, B, S, nb):
    L = MLSTM_L
    W = N_HEADS * DH
    tok = lambda b, c: (b, c, 0)
    const = lambda b, c: (0, 0)
    return pl.pallas_call(
        _mlstm_body,
        grid=(B // nb, S // L),
        in_specs=[pl.BlockSpec((nb, L, 4 * W), tok), pl.BlockSpec((nb, L, LANES), tok),
                  pl.BlockSpec((CONV_W, 2 * W), const), pl.BlockSpec((1, 2 * W), const),
                  pl.BlockSpec((1, LANES), const), pl.BlockSpec((1, W), const)],
        out_specs=pl.BlockSpec((nb, L, W), tok),
        out_shape=jax.ShapeDtypeStruct((B, S, W), BF16),
        scratch_shapes=([pltpu.VMEM((L + 2 * SUBLANES, 2 * W), BF16)] * nb
                        + [pltpu.VMEM((L, 2 * W), F32)] * nb
                        + [pltpu.VMEM((N_HEADS, DH, 2 * DH), F32)] * nb
                        + [pltpu.VMEM((N_HEADS, 1, LANES), F32)] * nb),
        compiler_params=_cparams(("arbitrary", "arbitrary")),
        name="mlstm",
    )(main.reshape(B, S, 4 * W), gates.reshape(B, S, LANES), conv_w, conv_b, b_gate,
      head_norm).reshape(B * S, W)


GLA_STAGES = 4


def _gla_consts():
    L, R = GLA_L, GLA_R
    row = lax.broadcasted_iota(I32, (L, L), 0)
    col = lax.broadcasted_iota(I32, (L, L), 1)
    return dict(
        tri_bf=jnp.where(col <= row, 1.0, 0.0).astype(BF16),
        lane=lax.broadcasted_iota(I32, (L, LANES), 1),
        rowl=lax.broadcasted_iota(I32, (L, LANES), 0),
        lane_r=lax.broadcasted_iota(I32, (R, LANES), 1),
        row_r=lax.broadcasted_iota(I32, (R, LANES), 0),
        wsel=jnp.where(lax.broadcasted_iota(I32, (LANES, 2 * LANES), 0) // GLA_DK
                       == lax.broadcasted_iota(I32, (LANES, 2 * LANES), 1) // LANES,
                       1.0, 0.0).astype(BF16))


def _gla_pair(p, la, rows, c, gla_ref, hn_ref, y_ref, sst):
    L, R = GLA_L, GLA_R
    WK = N_HEADS * GLA_DK
    WV = N_HEADS * DH
    dot = functools.partial(jnp.dot, preferred_element_type=F32)
    nt = (((1,), (1,)), ((), ()))
    tn = (((0,), (0,)), ((), ()))
    lane, rowl, lane_r, row_r = c['lane'], c['rowl'], c['lane_r'], c['row_r']

    bc = _cumsum_rows(c['tri_bf'], la[:, p * LANES:(p + 1) * LANES])
    yield

    q2 = gla_ref[rows, p * LANES:(p + 1) * LANES].astype(F32) * (GLA_DK ** -0.5)
    k2 = gla_ref[rows, WK + p * LANES:WK + (p + 1) * LANES].astype(F32)
    g_last = bc[L - 1:L, :]
    q_in = q2 * jnp.exp(bc)
    kd = k2 * jnp.exp(g_last - bc)
    decay = jnp.exp(g_last)

    k_off = [None]
    q_off = [None]
    for j in range(1, L // R):
        rj = bc[j * R:j * R + 1, :]
        k_off.append((k2 * jnp.exp(jnp.where(rowl < j * R, rj - bc, NEG_INF))).astype(BF16))
        q_off.append(q2[j * R:(j + 1) * R, :] * jnp.exp(bc[j * R:(j + 1) * R, :] - rj))

    diag = []
    for j in range(L // R):
        qb = q2[j * R:(j + 1) * R, :]
        bq = bc[j * R:(j + 1) * R, :]
        terms = []
        for s in range(R):
            krow = k2[j * R + s:j * R + s + 1, :]
            brow = bc[j * R + s:j * R + s + 1, :]
            d = jnp.where(row_r >= s, bq - brow, NEG_INF)
            terms.append((qb * krow * jnp.exp(d)).astype(BF16))
        diag.append(dot(jnp.concatenate(terms, axis=0), c['wsel']))

    heads = []
    for hh in range(2):
        h = 2 * p + hh
        mh = (lane // GLA_DK) == hh
        mq = (lane_r // GLA_DK) == hh
        v = gla_ref[rows, 2 * WK + h * DH:2 * WK + (h + 1) * DH]
        st = sst[h]
        o_inter = lax.dot_general(jnp.where(mh, q_in, 0.0).astype(BF16), st.astype(BF16), nt,
                                  preferred_element_type=F32)
        upd = lax.dot_general(v, jnp.where(mh, kd, 0.0).astype(BF16), tn, preferred_element_type=F32)
        offs = [None] + [lax.dot_general(jnp.where(mq, q_off[j], 0.0).astype(BF16), k_off[j], nt,
                                         preferred_element_type=F32) for j in range(1, L // R)]
        heads.append((h, v, st, o_inter, upd, offs))
    yield

    outs = []
    for hh, (h, v, st, o_inter, upd, offs) in enumerate(heads):
        sst[h] = st * decay + upd
        a_rows = []
        for j in range(L // R):
            res = diag[j][:, hh * LANES:(hh + 1) * LANES]
            blk = jnp.zeros((R, LANES), F32)
            for s in range(R):
                blk = jnp.where(lane_r == j * R + s, res[s * R:(s + 1) * R, :], blk)
            blk = blk[:, 0:L]
            if j > 0:
                blk = blk + offs[j]
            a_rows.append(blk)
        a_mat = jnp.concatenate(a_rows, axis=0).astype(BF16)
        outs.append((h, o_inter, dot(a_mat, v)))
    yield

    for h, o_inter, o_intra in outs:
        o = o_inter + o_intra
        on = o * lax.rsqrt(jnp.mean(o * o, axis=-1, keepdims=True) + EPS)
        z = gla_ref[rows, 2 * WK + WV + h * DH:2 * WK + WV + (h + 1) * DH].astype(F32)
        y_ref[rows, h * DH:(h + 1) * DH] = (
            on * hn_ref[:, h * DH:(h + 1) * DH] * (z * _sigmoid(z))).astype(BF16)
    yield


def _gla_body(gla_ref, gate_ref, wgg_ref, bgg_ref, hn_ref, y_ref, *ssts, n_chunks):
    nb = gla_ref.shape[0]

    @pl.when(pl.program_id(1) == 0)
    def _():
        for bb in range(nb):
            ssts[bb][...] = jnp.zeros(ssts[bb].shape, F32)

    consts = _gla_consts()

    def step(i, carry):
        rows = pl.ds(pl.multiple_of(i * GLA_L, GLA_L), GLA_L)
        chains = []
        for bb in range(nb):
            la = _log_sigmoid(jnp.dot(gate_ref[bb, rows, :].astype(BF16), wgg_ref[...],
                                      preferred_element_type=F32) + bgg_ref[...]) / GLA_GATE_NORM
            chains += [_gla_pair(p, la, rows, consts, gla_ref.at[bb], hn_ref, y_ref.at[bb], ssts[bb])
                       for p in range(N_HEADS // 2)]
        for _ in range(GLA_STAGES):
            for ch in chains:
                next(ch)
        return carry

    lax.fori_loop(0, n_chunks, step, 0)


def _gla(gla, gates, wgg, bgg, head_norm, B, S, lt, nb):
    WK = N_HEADS * GLA_DK
    WV = N_HEADS * DH
    tok = lambda b, c: (b, c, 0)
    const = lambda b, c: (0, 0)
    return pl.pallas_call(
        functools.partial(_gla_body, n_chunks=lt // GLA_L),
        grid=(B // nb, S // lt),
        in_specs=[pl.BlockSpec((nb, lt, 2 * WK + 2 * WV), tok), pl.BlockSpec((nb, lt, LANES), tok),
                  pl.BlockSpec((LANES, WK), const), pl.BlockSpec((1, WK), const),
                  pl.BlockSpec((1, WV), const)],
        out_specs=pl.BlockSpec((nb, lt, WV), tok),
        out_shape=jax.ShapeDtypeStruct((B, S, WV), BF16),
        scratch_shapes=[pltpu.VMEM((N_HEADS, DH, LANES), F32)] * nb,
        compiler_params=_cparams(("arbitrary", "arbitrary")),
        name="gla",
    )(gla.reshape(B, S, 2 * WK + 2 * WV), gates.reshape(B, S, LANES), wgg, bgg,
      head_norm).reshape(B * S, WV)


def _outproj_router_body(ym_ref, yg_ref, h_ref, wo_ref, g_ref, wr_ref, br_ref,
                         hout_ref, xn_ref, route_ref, cnt_ref, carry):
    tm = h_ref.shape[0]
    W = ym_ref.shape[1]
    dot = functools.partial(jnp.dot, preferred_element_type=F32)

    @pl.when(pl.program_id(0) == 0)
    def _():
        carry[...] = jnp.zeros(carry.shape, F32)

    hnew = h_ref[...] + dot(ym_ref[...], wo_ref[0:W, :]) + dot(yg_ref[...], wo_ref[W:2 * W, :])
    hout_ref[...] = hnew
    xn = hnew * lax.rsqrt(jnp.mean(hnew * hnew, axis=-1, keepdims=True) + EPS) * g_ref[...]
    xn_ref[...] = xn

    xh = xn.astype(BF16)
    xm = (xn - xh.astype(F32)).astype(BF16)
    l2 = dot(xh, wr_ref[...])
    logits = l2[:, 0:LANES] + (l2[:, LANES:2 * LANES] + dot(xm, wr_ref[:, 0:LANES])) + br_ref[...]

    lane = lax.broadcasted_iota(I32, (tm, LANES), 1)
    lane_f = lane.astype(F32)
    big = float(LANES)
    gl = jnp.where(lane < N_GROUPS, logits, NEG_INF)
    gmax = jnp.max(gl, axis=-1, keepdims=True)
    gsel = jnp.min(jnp.where(gl == gmax, lane_f, big), axis=-1, keepdims=True)
    g_gate = 1.0 / jnp.sum(jnp.where(lane < N_GROUPS, jnp.exp(logits - gmax), 0.0),
                           axis=-1, keepdims=True)
    in_grp = ((lane >= EXP_LANE0) & (lane < EXP_LANE0 + N_EXPERTS)
              & (((lane - EXP_LANE0) // EPG).astype(F32) == gsel))
    el = jnp.where(in_grp, logits, NEG_INF)
    v1 = jnp.max(el, axis=-1, keepdims=True)
    i1 = jnp.min(jnp.where(el == v1, lane_f, big), axis=-1, keepdims=True)
    el2 = jnp.where(lane_f == i1, NEG_INF, el)
    v2 = jnp.max(el2, axis=-1, keepdims=True)
    i2 = jnp.min(jnp.where(el2 == v2, lane_f, big), axis=-1, keepdims=True)
    e2 = jnp.exp(v2 - v1)
    w1 = g_gate / (1.0 + e2)
    w2 = g_gate * e2 / (1.0 + e2)

    oh1 = lane_f == i1
    oh2 = lane_f == i2
    oh = jnp.where(oh1, 1.0, 0.0) + jnp.where(oh2, 1.0, 0.0)
    r_i = lax.broadcasted_iota(I32, (tm, tm), 0)
    c_i = lax.broadcasted_iota(I32, (tm, tm), 1)
    strict = jnp.where(c_i < r_i, 1.0, 0.0).astype(BF16)
    before = dot(strict, oh.astype(BF16)) + carry[0:1, :]
    rank1 = jnp.sum(jnp.where(oh1, before, 0.0), axis=-1, keepdims=True)
    rank2 = jnp.sum(jnp.where(oh2, before, 0.0), axis=-1, keepdims=True)
    total = carry[0:1, :] + jnp.sum(oh, axis=0, keepdims=True)
    carry[...] = jnp.broadcast_to(total, carry.shape)
    cnt_ref[...] = jnp.broadcast_to(total, cnt_ref.shape)

    rec = jnp.where(lane == 0, i1 - EXP_LANE0, 0.0)
    rec = jnp.where(lane == 1, i2 - EXP_LANE0, rec)
    rec = jnp.where(lane == 2, w1, rec)
    rec = jnp.where(lane == 3, w2, rec)
    rec = jnp.where(lane == 4, rank1, rec)
    rec = jnp.where(lane == 5, rank2, rec)
    route_ref[...] = rec


def _outproj_router(ym, yg, h2d, wo, gain, wr3, br, tm):
    T, D = h2d.shape
    W = ym.shape[1]
    row = lambda i: (i, 0)
    const = lambda i: (0, 0)
    return pl.pallas_call(
        _outproj_router_body,
        grid=(T // tm,),
        in_specs=[pl.BlockSpec((tm, W), row), pl.BlockSpec((tm, W), row), pl.BlockSpec((tm, D), row),
                  pl.BlockSpec((2 * W, D), const), pl.BlockSpec((1, D), const),
                  pl.BlockSpec((D, 2 * LANES), const), pl.BlockSpec((1, LANES), const)],
        out_specs=[pl.BlockSpec((tm, D), row), pl.BlockSpec((tm, D), row),
                   pl.BlockSpec((tm, LANES), row), pl.BlockSpec((SUBLANES, LANES), const)],
        out_shape=[jax.ShapeDtypeStruct((T, D), F32), jax.ShapeDtypeStruct((T, D), F32),
                   jax.ShapeDtypeStruct((T, LANES), F32),
                   jax.ShapeDtypeStruct((SUBLANES, LANES), F32)],
        scratch_shapes=[pltpu.VMEM((SUBLANES, LANES), F32)],
        compiler_params=_cparams(("arbitrary",)),
        name="outproj_router",
    )(ym, yg, h2d, wo, gain, wr3, br)


def _ffn_body(blk_exp_ref, nused_ref, tok_ref, tokn_ref, x_hbm, wgu_ref, wd_ref, ys_ref,
              xg, sem, wgu_bf, wd_bf):
    b = pl.program_id(0)
    de = wd_ref.shape[2]
    slot = lax.rem(b, 2)
    used = b < nused_ref[0]
    new_expert = (b == 0) | (blk_exp_ref[b] != blk_exp_ref[jnp.maximum(b - 1, 0)])

    def row_copy(t_ref, s, j):
        return pltpu.make_async_copy(x_hbm.at[pl.ds(t_ref[0, 0, j], 1)], xg.at[s, pl.ds(j, 1)], sem.at[s])

    @pl.when(b == 0)
    def _():
        def issue(jb, carry):
            for u in range(ROW_UNROLL):
                row_copy(tok_ref, 0, jb * ROW_UNROLL + u).start(priority=u % 2)
            return carry

        lax.fori_loop(0, MOE_BLOCK // ROW_UNROLL, issue, 0)

    @pl.when(b <= nused_ref[0])
    def _():
        pltpu.make_async_copy(x_hbm.at[pl.ds(0, MOE_BLOCK)], xg.at[slot], sem.at[slot]).wait()

    @pl.when(used & new_expert)
    def _():
        wgu_bf[...] = wgu_ref[0, 0].astype(BF16)
        wd_bf[...] = wd_ref[0, 0].astype(BF16)

    @pl.when(used)
    def _():
        xb = xg[slot].astype(BF16)
        for j in range(MOE_BLOCK):
            row_copy(tokn_ref, 1 - slot, j).start(priority=j % 2)
        gu = jnp.dot(xb, wgu_bf[...], preferred_element_type=F32)
        gate, up = gu[:, 0:de], gu[:, de:2 * de]
        act = (gate * _sigmoid(gate) * up).astype(BF16)
        ys_ref[...] = jnp.dot(act, wd_bf[...], preferred_element_type=F32)

    @pl.when(jnp.logical_not(used))
    def _():
        ys_ref[...] = jnp.zeros(ys_ref.shape, F32)


def _ffn(blk_exp, nused, tok, xn, wgu, wd, layer):
    T, D = xn.shape
    nb = tok.shape[0] // MOE_BLOCK
    de = wd.shape[2]
    tok3 = tok.reshape(nb, 1, MOE_BLOCK)
    smem = functools.partial(pl.BlockSpec, (1, 1, MOE_BLOCK), memory_space=pltpu.SMEM)
    grid_spec = pltpu.PrefetchScalarGridSpec(
        num_scalar_prefetch=2,
        grid=(nb,),
        in_specs=[smem(index_map=lambda b, be, nu: (b, 0, 0)),
                  smem(index_map=lambda b, be, nu: (jnp.minimum(b + 1, nb - 1), 0, 0)),
                  pl.BlockSpec(memory_space=pl.ANY),
                  pl.BlockSpec((1, 1, D, 2 * de), lambda b, be, nu: (layer, be[b], 0, 0)),
                  pl.BlockSpec((1, 1, de, D), lambda b, be, nu: (layer, be[b], 0, 0))],
        out_specs=pl.BlockSpec((MOE_BLOCK, D), lambda b, be, nu: (b, 0)),
        scratch_shapes=[pltpu.VMEM((2, MOE_BLOCK, D), F32), pltpu.SemaphoreType.DMA((2,)),
                        pltpu.VMEM((D, 2 * de), BF16), pltpu.VMEM((de, D), BF16)],
    )
    return pl.pallas_call(
        _ffn_body,
        grid_spec=grid_spec,
        out_shape=jax.ShapeDtypeStruct((nb * MOE_BLOCK, D), F32),
        compiler_params=_cparams(("arbitrary",)),
        name="expert_ffn",
    )(blk_exp, nused, tok3, tok3, xn, wgu, wd)


def _combine_body(pos_ref, posn_ref, ys_hbm, h_ref, route_ref, gfin_ref, out_ref, gbuf, sem,
                  *, tc, n_tiles, final):
    i = pl.program_id(0)
    slot = lax.rem(i, 2)

    def issue_tile(p_ref, s):
        def issue(jb, carry):
            for u in range(ROW_UNROLL):
                j = jb * ROW_UNROLL + u
                for k in range(2):
                    pltpu.make_async_copy(ys_hbm.at[pl.ds(p_ref[0, 0, 2 * j + k], 1)],
                                          gbuf.at[s, k, pl.ds(j, 1)], sem.at[s]).start(priority=k)
            return carry

        lax.fori_loop(0, tc // ROW_UNROLL, issue, 0)

    @pl.when(i == 0)
    def _():
        issue_tile(pos_ref, 0)

    @pl.when(i + 1 < n_tiles)
    def _():
        issue_tile(posn_ref, lax.rem(i + 1, 2))

    for k in range(2):
        pltpu.make_async_copy(ys_hbm.at[pl.ds(0, tc)], gbuf.at[slot, k], sem.at[slot]).wait()

    w1 = route_ref[:, 2:3]
    w2 = route_ref[:, 3:4]
    out = h_ref[...] + (w1 * gbuf[slot, 0] + w2 * gbuf[slot, 1])
    if final:
        out = out * lax.rsqrt(jnp.mean(out * out, axis=-1, keepdims=True) + EPS) * gfin_ref[...]
    out_ref[...] = out


def _combine(pos, ys, h2d, route, gfin, tc, final):
    T, D = h2d.shape
    n = T // tc
    pos3 = pos.reshape(n, 1, 2 * tc)
    row = lambda i: (i, 0)
    smem = functools.partial(pl.BlockSpec, (1, 1, 2 * tc), memory_space=pltpu.SMEM)
    return pl.pallas_call(
        functools.partial(_combine_body, tc=tc, n_tiles=n, final=final),
        grid=(n,),
        in_specs=[smem(index_map=lambda i: (i, 0, 0)),
                  smem(index_map=lambda i: (jnp.minimum(i + 1, n - 1), 0, 0)),
                  pl.BlockSpec(memory_space=pl.ANY),
                  pl.BlockSpec((tc, D), row), pl.BlockSpec((tc, LANES), row),
                  pl.BlockSpec((1, D), lambda i: (0, 0))],
        out_specs=pl.BlockSpec((tc, D), row),
        out_shape=jax.ShapeDtypeStruct((T, D), F32),
        scratch_shapes=[pltpu.VMEM((2, 2, tc, D), F32), pltpu.SemaphoreType.DMA((2,))],
        compiler_params=_cparams(("arbitrary",)),
        name="combine",
    )(pos3, pos3, ys, h2d, route, gfin)


def _pick_tile(n, pref):
    t = min(pref, n)
    while n % t:
        t //= 2
    return t


def _prep_layer(w_in, b_mlstm_gate, w_gla_gate, b_gla_gate, w_out, w_group, b_group, w_expert,
                b_expert):
    W = N_HEADS * DH
    WK = N_HEADS * GLA_DK
    D = w_in.shape[0]
    o_gates = 4 * W
    o_gla = o_gates + 2 * N_HEADS
    o_r = o_gla + 2 * WK + 2 * W
    wm = w_in[:, 0:o_gates].astype(BF16)
    wg = w_in[:, o_gla:o_r].astype(BF16)
    ws = jnp.zeros((D, LANES), F32)
    ws = ws.at[:, 0:2 * N_HEADS].set(w_in[:, o_gates:o_gla])
    ws = ws.at[:, 2 * N_HEADS:2 * N_HEADS + GLA_RANK].set(w_in[:, o_r:o_r + GLA_RANK]).astype(BF16)
    bg = jnp.zeros((1, LANES), F32).at[0, 0:2 * N_HEADS].set(b_mlstm_gate)
    wgg = jnp.zeros((LANES, WK), F32).at[2 * N_HEADS:2 * N_HEADS + GLA_RANK, :].set(w_gla_gate)
    wr = jnp.zeros((D, LANES), F32)
    wr = wr.at[:, 0:N_GROUPS].set(w_group).at[:, EXP_LANE0:EXP_LANE0 + N_EXPERTS].set(w_expert)
    br = jnp.zeros((1, LANES), F32)
    br = br.at[0, 0:N_GROUPS].set(b_group).at[0, EXP_LANE0:EXP_LANE0 + N_EXPERTS].set(b_expert)
    return dict(wm=wm, wg=wg, ws=ws, bg=bg, wgg=wgg.astype(BF16), bgg=b_gla_gate[None, :],
                wo=w_out.astype(BF16), wr3=jnp.concatenate(_split3(wr)[0:2], axis=1), br=br)


def _routing_tables(route, counts_row, T):
    eid = route[:, 0:2].astype(I32)
    rank = route[:, 4:6].astype(I32)
    counts = counts_row[EXP_LANE0:EXP_LANE0 + N_EXPERTS].astype(I32)
    pcounts = (counts + MOE_BLOCK - 1) // MOE_BLOCK * MOE_BLOCK
    pend = jnp.cumsum(pcounts)
    pstart = pend - pcounts
    onehot = eid[:, :, None] == jnp.arange(N_EXPERTS, dtype=I32)[None, None, :]
    pos = jnp.sum(jnp.where(onehot, pstart[None, None, :], 0), axis=-1) + rank
    n_blocks = (T * 2 + MOE_BLOCK - 1) // MOE_BLOCK + N_EXPERTS
    blk_start = jnp.arange(n_blocks, dtype=I32) * MOE_BLOCK
    blk_exp = jnp.minimum(jnp.sum((pend[None, :] <= blk_start[:, None]).astype(I32), axis=1),
                          N_EXPERTS - 1)
    nused = (pend[-1] // MOE_BLOCK).astype(I32).reshape(1)
    tok = jnp.zeros((n_blocks * MOE_BLOCK,), I32).at[pos.reshape(-1)].set(
        jnp.arange(2 * T, dtype=I32) // 2, unique_indices=True)
    return pos, tok, blk_exp, nused


def kernel(x, norm_mix, w_in, conv_w, conv_b, b_mlstm_gate, w_gla_gate, b_gla_gate, head_norm, w_out, norm_ffn, w_group, b_group, w_expert, b_expert, w_gu, w_down, norm_final):
    B, S, D = x.shape
    T = B * S
    depth = w_in.shape[0]
    W = N_HEADS * DH
    tm = _pick_tile(T, 512)
    tc = _pick_tile(T, 256)
    lt = _pick_tile(S, 256)
    nb = _pick_tile(B, 2)
    h = x.reshape(T, D)
    pending = None
    for l in range(depth):
        p = _prep_layer(w_in[l], b_mlstm_gate[l], w_gla_gate[l], b_gla_gate[l], w_out[l],
                        w_group[l], b_group[l], w_expert[l], b_expert[l])
        if pending is None:
            main, gla, gates = _inproj(h, norm_mix[l][None, :], p['wm'], p['wg'], p['ws'], tm)
        else:
            h, main, gla, gates = _inproj_combine(*pending, norm_mix[l][None, :], p['wm'], p['wg'],
                                                  p['ws'], tm)
        ym = _mlstm(main, gates, conv_w[l], conv_b[l][None, :], p['bg'], head_norm[l][None, 0:W], B, S,
                    nb)
        yg = _gla(gla, gates, p['wgg'], p['bgg'], head_norm[l][None, W:2 * W], B, S, lt, nb)
        h, xn, route, cnt = _outproj_router(ym, yg, h, p['wo'], norm_ffn[l][None, :], p['wr3'],
                                            p['br'], tm)
        pos, tok, blk_exp, nused = _routing_tables(route, cnt[0], T)
        ys = _ffn(blk_exp, nused, tok, xn, w_gu, w_down, l)
        pending = (pos, ys, h, route)
    pos, ys, h, route = pending
    h = _combine(pos, ys, h, route, norm_final[None, :], tc, final=True)
    return h.reshape(B, S, D)
```

```python
import functools

import jax
import jax.numpy as jnp
from jax import lax
from jax.experimental import pallas as pl
from jax.experimental.pallas import tpu as pltpu

F32 = jnp.float32
BF16 = jnp.bfloat16
I32 = jnp.int32

EPS = 1e-6
LANES = 128
SUBLANES = 8
VMEM_LIMIT = 56 * 1024 * 1024

N_HEADS = 4
DH = 128
GLA_DK = 64
GLA_RANK = 16
GLA_GATE_NORM = 16.0
CONV_W = 4
MLSTM_L = 128
GLA_L = 64
GLA_R = 16
N_GROUPS = 4
EPG = 8
N_EXPERTS = N_GROUPS * EPG
MOE_BLOCK = 512
EXP_LANE0 = N_GROUPS

ROW_UNROLL = 8

NEG_INF = float("-inf")


def _cparams(sem):
    return pltpu.CompilerParams(dimension_semantics=sem, vmem_limit_bytes=VMEM_LIMIT)


def _split3(x):
    hi = x.astype(BF16)
    r1 = x - hi.astype(F32)
    mid = r1.astype(BF16)
    lo = (r1 - mid.astype(F32)).astype(BF16)
    return hi, mid, lo


def _cumsum_rows(tri_bf, x):
    hi, mid, lo = _split3(x)
    dot = functools.partial(jnp.dot, preferred_element_type=F32)
    return dot(tri_bf, hi) + dot(tri_bf, mid) + dot(tri_bf, lo)


def _sigmoid(x):
    return 1.0 / (1.0 + jnp.exp(-x))


def _log_sigmoid(x):
    return jnp.minimum(x, 0.0) - jnp.log(1.0 + jnp.exp(-jnp.abs(x)))


def _inproj_body(x_ref, g_ref, wm_ref, wg_ref, ws_ref, om_ref, og_ref, os_ref):
    x = x_ref[...]
    xn = x * lax.rsqrt(jnp.mean(x * x, axis=-1, keepdims=True) + EPS) * g_ref[...]
    xb = xn.astype(BF16)
    om_ref[...] = jnp.dot(xb, wm_ref[...], preferred_element_type=F32).astype(BF16)
    og_ref[...] = jnp.dot(xb, wg_ref[...], preferred_element_type=F32).astype(BF16)
    os_ref[...] = jnp.dot(xb, ws_ref[...], preferred_element_type=F32)


def _inproj(h2d, gain, wm, wg, ws, tm):
    T, D = h2d.shape
    nm, ng, ns = wm.shape[1], wg.shape[1], ws.shape[1]
    const = lambda i: (0, 0)
    row = lambda i: (i, 0)
    return pl.pallas_call(
        _inproj_body,
        grid=(T // tm,),
        in_specs=[pl.BlockSpec((tm, D), row), pl.BlockSpec((1, D), const),
                  pl.BlockSpec((D, nm), const), pl.BlockSpec((D, ng), const),
                  pl.BlockSpec((D, ns), const)],
        out_specs=[pl.BlockSpec((tm, nm), row), pl.BlockSpec((tm, ng), row),
                   pl.BlockSpec((tm, ns), row)],
        out_shape=[jax.ShapeDtypeStruct((T, nm), BF16), jax.ShapeDtypeStruct((T, ng), BF16),
                   jax.ShapeDtypeStruct((T, ns), F32)],
        compiler_params=_cparams(("parallel",)),
        name="inproj",
    )(h2d, gain, wm, wg, ws)


def _inproj_combine_body(pos_ref, posn_ref, ys_hbm, h_ref, route_ref, g_ref, wm_ref, wg_ref, ws_ref,
                         hout_ref, om_ref, og_ref, os_ref, gbuf, sem, *, n_tiles):
    tm = h_ref.shape[0]
    i = pl.program_id(0)
    slot = lax.rem(i, 2)

    def row_copy(p_ref, s, j, k):
        return pltpu.make_async_copy(ys_hbm.at[pl.ds(p_ref[0, 0, 2 * j + k], 1)],
                                     gbuf.at[s, k, pl.ds(j, 1)], sem.at[s])

    def wait_tile(s):
        for k in range(2):
            pltpu.make_async_copy(ys_hbm.at[pl.ds(0, tm)], gbuf.at[s, k], sem.at[s]).wait()

    @pl.when(i == 0)
    def _():
        def issue(jb, carry):
            for u in range(ROW_UNROLL):
                for k in range(2):
                    row_copy(pos_ref, 0, jb * ROW_UNROLL + u, k).start(priority=k)
            return carry

        lax.fori_loop(0, tm // ROW_UNROLL, issue, 0)

    wait_tile(slot)
    x = h_ref[...] + (route_ref[:, 2:3] * gbuf[slot, 0] + route_ref[:, 3:4] * gbuf[slot, 1])
    hout_ref[...] = x

    for j in range(tm):
        for k in range(2):
            row_copy(posn_ref, 1 - slot, j, k).start(priority=k)

    xn = x * lax.rsqrt(jnp.mean(x * x, axis=-1, keepdims=True) + EPS) * g_ref[...]
    xb = xn.astype(BF16)
    om_ref[...] = jnp.dot(xb, wm_ref[...], preferred_element_type=F32).astype(BF16)
    og_ref[...] = jnp.dot(xb, wg_ref[...], preferred_element_type=F32).astype(BF16)
    os_ref[...] = jnp.dot(xb, ws_ref[...], preferred_element_type=F32)

    @pl.when(i == n_tiles - 1)
    def _():
        wait_tile(1 - slot)


def _inproj_combine(pos, ys, h2d, route, gain, wm, wg, ws, tm):
    T, D = h2d.shape
    n = T // tm
    nm, ng, ns = wm.shape[1], wg.shape[1], ws.shape[1]
    const = lambda i: (0, 0)
    row = lambda i: (i, 0)
    pos3 = pos.reshape(n, 1, 2 * tm)
    smem = functools.partial(pl.BlockSpec, (1, 1, 2 * tm), memory_space=pltpu.SMEM)
    return pl.pallas_call(
        functools.partial(_inproj_combine_body, n_tiles=n),
        grid=(n,),
        in_specs=[smem(index_map=lambda i: (i, 0, 0)),
                  smem(index_map=lambda i: (jnp.minimum(i + 1, n - 1), 0, 0)),
                  pl.BlockSpec(memory_space=pl.ANY),
                  pl.BlockSpec((tm, D), row), pl.BlockSpec((tm, LANES), row), pl.BlockSpec((1, D), const),
                  pl.BlockSpec((D, nm), const), pl.BlockSpec((D, ng), const),
                  pl.BlockSpec((D, ns), const)],
        out_specs=[pl.BlockSpec((tm, D), row), pl.BlockSpec((tm, nm), row), pl.BlockSpec((tm, ng), row),
                   pl.BlockSpec((tm, ns), row)],
        out_shape=[jax.ShapeDtypeStruct((T, D), F32), jax.ShapeDtypeStruct((T, nm), BF16),
                   jax.ShapeDtypeStruct((T, ng), BF16), jax.ShapeDtypeStruct((T, ns), F32)],
        scratch_shapes=[pltpu.VMEM((2, 2, tm, D), F32), pltpu.SemaphoreType.DMA((2,))],
        compiler_params=_cparams(("arbitrary",)),
        name="inproj_combine",
    )(pos3, pos3, ys, h2d, route, gain, wm, wg, ws)


def _mlstm_body(main_ref, gate_ref, cw_ref, cb_ref, bg_ref, hn_ref, y_ref, *scratch):
    L = MLSTM_L
    W = N_HEADS * DH
    TAIL = 2 * SUBLANES
    nb = main_ref.shape[0]
    ubufs, qkbufs, csts, msts = (scratch[i * nb:(i + 1) * nb] for i in range(4))
    dot = functools.partial(jnp.dot, preferred_element_type=F32)

    @pl.when(pl.program_id(1) == 0)
    def _():
        for bb in range(nb):
            ubufs[bb][0:TAIL, :] = jnp.zeros((TAIL, 2 * W), BF16)
            csts[bb][...] = jnp.zeros(csts[bb].shape, F32)
            msts[bb][...] = jnp.zeros(msts[bb].shape, F32)

    row = lax.broadcasted_iota(I32, (L, L), 0)
    col = lax.broadcasted_iota(I32, (L, L), 1)
    tri = col <= row
    tri_bf = jnp.where(tri, 1.0, 0.0).astype(BF16)
    lane = lax.broadcasted_iota(I32, (L, LANES), 1)
    sh_r = lax.broadcasted_iota(I32, ((CONV_W - 1) * L, L + TAIL), 0)
    sh_c = lax.broadcasted_iota(I32, ((CONV_W - 1) * L, L + TAIL), 1)
    shift = jnp.where(sh_c == (sh_r % L) + TAIL - (CONV_W - 1) + sh_r // L, 1.0, 0.0).astype(BF16)
    rp_r = lax.broadcasted_iota(I32, (LANES, 2 * N_HEADS * LANES), 0)
    rp_c = lax.broadcasted_iota(I32, (LANES, 2 * N_HEADS * LANES), 1)
    rep = jnp.where(rp_r == rp_c // LANES, 1.0, 0.0).astype(BF16)

    gates = []
    for bb in range(nb):
        ubuf = ubufs[bb]
        u = main_ref[bb, :, 0:2 * W]
        ubuf[TAIL:TAIL + L, :] = u
        shifted = dot(shift, ubuf[...])
        ubuf[0:TAIL, :] = ubuf[L:L + TAIL, :]
        acc = cb_ref[...] + u.astype(F32) * cw_ref[CONV_W - 1:CONV_W, :]
        for i in range(CONV_W - 1):
            acc = acc + shifted[i * L:(i + 1) * L, :] * cw_ref[i:i + 1, :]
        qk = acc * _sigmoid(acc)
        qkbufs[bb][:, 0:W] = qk[:, 0:W]
        qkbufs[bb][:, W:2 * W] = qk[:, W:2 * W] * (DH ** -0.5)

        gpre = gate_ref[bb] + bg_ref[...]
        xg = jnp.where(lane < N_HEADS, gpre, _log_sigmoid(gpre))
        gm = jnp.where(lane < N_HEADS, xg, _cumsum_rows(tri_bf, xg))
        hi, mid, lo = _split3(gm)
        gates.append((dot(hi, rep) + dot(mid, rep) + dot(lo, rep), gm.T))

    chains = [_mlstm_head(h, main_ref.at[bb], hn_ref, y_ref.at[bb], qkbufs[bb], csts[bb], msts[bb],
                          gates[bb], tri)
              for h in range(N_HEADS) for bb in range(nb)]
    for _ in range(MLSTM_STAGES):
        for c in chains:
            next(c)


MLSTM_STAGES = 3


def _mlstm_head(h, main_ref, hn_ref, y_ref, qkbuf, cst, mst, gates, tri):
    L = MLSTM_L
    W = N_HEADS * DH
    dot = functools.partial(jnp.dot, preferred_element_type=F32)
    g_rep, g_t = gates
    hs = slice(h * DH, (h + 1) * DH)

    q = qkbuf[:, hs].astype(BF16)
    k = qkbuf[:, W + h * DH:W + (h + 1) * DH]
    v = main_ref[:, 2 * W + h * DH:2 * W + (h + 1) * DH]
    vext = jnp.concatenate([v, jnp.ones((L, DH), BF16)], axis=1)
    cext = cst[h]
    s_raw = lax.dot_general(q, k.astype(BF16), (((1,), (1,)), ((), ())), preferred_element_type=F32)
    qc = dot(q, cext.astype(BF16))
    yield

    ig_rep = g_rep[:, h * LANES:(h + 1) * LANES]
    b_rep = g_rep[:, (N_HEADS + h) * LANES:(N_HEADS + h + 1) * LANES]
    ig_row = g_t[h:h + 1, :]
    b_row = g_t[N_HEADS + h:N_HEADS + h + 1, :]
    m_prev = mst[h]
    logd = jnp.where(tri, b_rep - b_row + ig_row, NEG_INF)
    m_inter = b_rep + m_prev
    m_j = jnp.maximum(m_inter, jnp.max(logd, axis=-1, keepdims=True))
    s = s_raw * jnp.exp(logd - m_j)
    a = jnp.exp(m_inter - m_j)
    sv = dot(s.astype(BF16), vext)
    g = b_rep[L - 1:L, :]
    m_new = jnp.maximum(g + m_prev, jnp.max(g - b_row + ig_row, axis=-1, keepdims=True))
    kw = (k * jnp.exp(g - b_rep + ig_rep - m_new)).astype(BF16)
    decay = jnp.exp(g + m_prev - m_new)
    upd = lax.dot_general(kw, vext, (((0,), (0,)), ((), ())), preferred_element_type=F32)
    yield

    num = sv[:, 0:DH] + a * qc[:, 0:DH]
    den = sv[:, DH:2 * DH] + a * qc[:, DH:2 * DH]
    hh = num / jnp.maximum(jnp.abs(den), jnp.exp(-m_j))
    cst[h] = jnp.concatenate([decay, decay], axis=1) * cext + upd
    mst[h] = m_new
    hn = hh * lax.rsqrt(jnp.mean(hh * hh, axis=-1, keepdims=True) + EPS)
    o_gate = _sigmoid(main_ref[:, 3 * W + h * DH:3 * W + (h + 1) * DH].astype(F32))
    y_ref[:, hs] = (hn * hn_ref[:, hs] * o_gate).astype(BF16)
    yield


def _mlstm(main, gates, conv_w, conv_b, b_gate, head_norm, B, S, nb):
    L = MLSTM_L
    W = N_HEADS * DH
    tok = lambda b, c: (b, c, 0)
    const = lambda b, c: (0, 0)
    return pl.pallas_call(
        _mlstm_body,
        grid=(B // nb, S // L),
        in_specs=[pl.BlockSpec((nb, L, 4 * W), tok), pl.BlockSpec((nb, L, LANES), tok),
                  pl.BlockSpec((CONV_W, 2 * W), const), pl.BlockSpec((1, 2 * W), const),
                  pl.BlockSpec((1, LANES), const), pl.BlockSpec((1, W), const)],
        out_specs=pl.BlockSpec((nb, L, W), tok),
        out_shape=jax.ShapeDtypeStruct((B, S, W), BF16),
        scratch_shapes=([pltpu.VMEM((L + 2 * SUBLANES, 2 * W), BF16)] * nb
                        + [pltpu.VMEM((L, 2 * W), F32)] * nb
                        + [pltpu.VMEM((N_HEADS, DH, 2 * DH), F32)] * nb
                        + [pltpu.VMEM((N_HEADS, 1, LANES), F32)] * nb),
        compiler_params=_cparams(("arbitrary", "arbitrary")),
        name="mlstm",
    )(main.reshape(B, S, 4 * W), gates.reshape(B, S, LANES), conv_w, conv_b, b_gate,
      head_norm).reshape(B * S, W)


GLA_STAGES = 4


def _gla_consts():
    L, R = GLA_L, GLA_R
    row = lax.broadcasted_iota(I32, (L, L), 0)
    col = lax.broadcasted_iota(I32, (L, L), 1)
    return dict(
        tri_bf=jnp.where(col <= row, 1.0, 0.0).astype(BF16),
        lane=lax.broadcasted_iota(I32, (L, LANES), 1),
        rowl=lax.broadcasted_iota(I32, (L, LANES), 0),
        lane_r=lax.broadcasted_iota(I32, (R, LANES), 1),
        row_r=lax.broadcasted_iota(I32, (R, LANES), 0),
        wsel=jnp.where(lax.broadcasted_iota(I32, (LANES, 2 * LANES), 0) // GLA_DK
                       == lax.broadcasted_iota(I32, (LANES, 2 * LANES), 1) // LANES,
                       1.0, 0.0).astype(BF16))


def _gla_pair(p, la, rows, c, gla_ref, hn_ref, y_ref, sst):
    L, R = GLA_L, GLA_R
    WK = N_HEADS * GLA_DK
    WV = N_HEADS * DH
    dot = functools.partial(jnp.dot, preferred_element_type=F32)
    nt = (((1,), (1,)), ((), ()))
    tn = (((0,), (0,)), ((), ()))
    lane, rowl, lane_r, row_r = c['lane'], c['rowl'], c['lane_r'], c['row_r']

    bc = _cumsum_rows(c['tri_bf'], la[:, p * LANES:(p + 1) * LANES])
    yield

    q2 = gla_ref[rows, p * LANES:(p + 1) * LANES].astype(F32) * (GLA_DK ** -0.5)
    k2 = gla_ref[rows, WK + p * LANES:WK + (p + 1) * LANES].astype(F32)
    g_last = bc[L - 1:L, :]
    q_in = q2 * jnp.exp(bc)
    kd = k2 * jnp.exp(g_last - bc)
    decay = jnp.exp(g_last)

    k_off = [None]
    q_off = [None]
    for j in range(1, L // R):
        rj = bc[j * R:j * R + 1, :]
        k_off.append((k2 * jnp.exp(jnp.where(rowl < j * R, rj - bc, NEG_INF))).astype(BF16))
        q_off.append(q2[j * R:(j + 1) * R, :] * jnp.exp(bc[j * R:(j + 1) * R, :] - rj))

    diag = []
    for j in range(L // R):
        qb = q2[j * R:(j + 1) * R, :]
        bq = bc[j * R:(j + 1) * R, :]
        terms = []
        for s in range(R):
            krow = k2[j * R + s:j * R + s + 1, :]
            brow = bc[j * R + s:j * R + s + 1, :]
            d = jnp.where(row_r >= s, bq - brow, NEG_INF)
            terms.append((qb * krow * jnp.exp(d)).astype(BF16))
        diag.append(dot(jnp.concatenate(terms, axis=0), c['wsel']))

    heads = []
    for hh in range(2):
        h = 2 * p + hh
        mh = (lane // GLA_DK) == hh
        mq = (lane_r // GLA_DK) == hh
        v = gla_ref[rows, 2 * WK + h * DH:2 * WK + (h + 1) * DH]
        st = sst[h]
        o_inter = lax.dot_general(jnp.where(mh, q_in, 0.0).astype(BF16), st.astype(BF16), nt,
                                  preferred_element_type=F32)
        upd = lax.dot_general(v, jnp.where(mh, kd, 0.0).astype(BF16), tn, preferred_element_type=F32)
        offs = [None] + [lax.dot_general(jnp.where(mq, q_off[j], 0.0).astype(BF16), k_off[j], nt,
                                         preferred_element_type=F32) for j in range(1, L // R)]
        heads.append((h, v, st, o_inter, upd, offs))
    yield

    outs = []
    for hh, (h, v, st, o_inter, upd, offs) in enumerate(heads):
        sst[h] = st * decay + upd
        a_rows = []
        for j in range(L // R):
            res = diag[j][:, hh * LANES:(hh + 1) * LANES]
            blk = jnp.zeros((R, LANES), F32)
            for s in range(R):
                blk = jnp.where(lane_r == j * R + s, res[s * R:(s + 1) * R, :], blk)
            blk = blk[:, 0:L]
            if j > 0:
                blk = blk + offs[j]
            a_rows.append(blk)
        a_mat = jnp.concatenate(a_rows, axis=0).astype(BF16)
        outs.append((h, o_inter, dot(a_mat, v)))
    yield

    for h, o_inter, o_intra in outs:
        o = o_inter + o_intra
        on = o * lax.rsqrt(jnp.mean(o * o, axis=-1, keepdims=True) + EPS)
        z = gla_ref[rows, 2 * WK + WV + h * DH:2 * WK + WV + (h + 1) * DH].astype(F32)
        y_ref[rows, h * DH:(h + 1) * DH] = (
            on * hn_ref[:, h * DH:(h + 1) * DH] * (z * _sigmoid(z))).astype(BF16)
    yield


def _gla_body(gla_ref, gate_ref, wgg_ref, bgg_ref, hn_ref, y_ref, *ssts, n_chunks):
    nb = gla_ref.shape[0]

    @pl.when(pl.program_id(1) == 0)
    def _():
        for bb in range(nb):
            ssts[bb][...] = jnp.zeros(ssts[bb].shape, F32)

    consts = _gla_consts()

    def step(i, carry):
        rows = pl.ds(pl.multiple_of(i * GLA_L, GLA_L), GLA_L)
        chains = []
        for bb in range(nb):
            la = _log_sigmoid(jnp.dot(gate_ref[bb, rows, :].astype(BF16), wgg_ref[...],
                                      preferred_element_type=F32) + bgg_ref[...]) / GLA_GATE_NORM
            chains += [_gla_pair(p, la, rows, consts, gla_ref.at[bb], hn_ref, y_ref.at[bb], ssts[bb])
                       for p in range(N_HEADS // 2)]
        for _ in range(GLA_STAGES):
            for ch in chains:
                next(ch)
        return carry

    lax.fori_loop(0, n_chunks, step, 0)


def _gla(gla, gates, wgg, bgg, head_norm, B, S, lt, nb):
    WK = N_HEADS * GLA_DK
    WV = N_HEADS * DH
    tok = lambda b, c: (b, c, 0)
    const = lambda b, c: (0, 0)
    return pl.pallas_call(
        functools.partial(_gla_body, n_chunks=lt // GLA_L),
        grid=(B // nb, S // lt),
        in_specs=[pl.BlockSpec((nb, lt, 2 * WK + 2 * WV), tok), pl.BlockSpec((nb, lt, LANES), tok),
                  pl.BlockSpec((LANES, WK), const), pl.BlockSpec((1, WK), const),
                  pl.BlockSpec((1, WV), const)],
        out_specs=pl.BlockSpec((nb, lt, WV), tok),
        out_shape=jax.ShapeDtypeStruct((B, S, WV), BF16),
        scratch_shapes=[pltpu.VMEM((N_HEADS, DH, LANES), F32)] * nb,
        compiler_params=_cparams(("arbitrary", "arbitrary")),
        name="gla",
    )(gla.reshape(B, S, 2 * WK + 2 * WV), gates.reshape(B, S, LANES), wgg, bgg,
      head_norm).reshape(B * S, WV)


def _outproj_router_body(ym_ref, yg_ref, h_ref, wo_ref, g_ref, wr_ref, br_ref,
                         hout_ref, xn_ref, route_ref, cnt_ref, carry):
    tm = h_ref.shape[0]
    W = ym_ref.shape[1]
    dot = functools.partial(jnp.dot, preferred_element_type=F32)

    @pl.when(pl.program_id(0) == 0)
    def _():
        carry[...] = jnp.zeros(carry.shape, F32)

    hnew = h_ref[...] + dot(ym_ref[...], wo_ref[0:W, :]) + dot(yg_ref[...], wo_ref[W:2 * W, :])
    hout_ref[...] = hnew
    xn = hnew * lax.rsqrt(jnp.mean(hnew * hnew, axis=-1, keepdims=True) + EPS) * g_ref[...]
    xn_ref[...] = xn

    xh = xn.astype(BF16)
    xm = (xn - xh.astype(F32)).astype(BF16)
    l2 = dot(xh, wr_ref[...])
    logits = l2[:, 0:LANES] + (l2[:, LANES:2 * LANES] + dot(xm, wr_ref[:, 0:LANES])) + br_ref[...]

    lane = lax.broadcasted_iota(I32, (tm, LANES), 1)
    lane_f = lane.astype(F32)
    big = float(LANES)
    gl = jnp.where(lane < N_GROUPS, logits, NEG_INF)
    gmax = jnp.max(gl, axis=-1, keepdims=True)
    gsel = jnp.min(jnp.where(gl == gmax, lane_f, big), axis=-1, keepdims=True)
    g_gate = 1.0 / jnp.sum(jnp.where(lane < N_GROUPS, jnp.exp(logits - gmax), 0.0),
                           axis=-1, keepdims=True)
    in_grp = ((lane >= EXP_LANE0) & (lane < EXP_LANE0 + N_EXPERTS)
              & (((lane - EXP_LANE0) // EPG).astype(F32) == gsel))
    el = jnp.where(in_grp, logits, NEG_INF)
    v1 = jnp.max(el, axis=-1, keepdims=True)
    i1 = jnp.min(jnp.where(el == v1, lane_f, big), axis=-1, keepdims=True)
    el2 = jnp.where(lane_f == i1, NEG_INF, el)
    v2 = jnp.max(el2, axis=-1, keepdims=True)
    i2 = jnp.min(jnp.where(el2 == v2, lane_f, big), axis=-1, keepdims=True)
    e2 = jnp.exp(v2 - v1)
    w1 = g_gate / (1.0 + e2)
    w2 = g_gate * e2 / (1.0 + e2)

    oh1 = lane_f == i1
    oh2 = lane_f == i2
    oh = jnp.where(oh1, 1.0, 0.0) + jnp.where(oh2, 1.0, 0.0)
    r_i = lax.broadcasted_iota(I32, (tm, tm), 0)
    c_i = lax.broadcasted_iota(I32, (tm, tm), 1)
    strict = jnp.where(c_i < r_i, 1.0, 0.0).astype(BF16)
    before = dot(strict, oh.astype(BF16)) + carry[0:1, :]
    rank1 = jnp.sum(jnp.where(oh1, before, 0.0), axis=-1, keepdims=True)
    rank2 = jnp.sum(jnp.where(oh2, before, 0.0), axis=-1, keepdims=True)
    total = carry[0:1, :] + jnp.sum(oh, axis=0, keepdims=True)
    carry[...] = jnp.broadcast_to(total, carry.shape)
    cnt_ref[...] = jnp.broadcast_to(total, cnt_ref.shape)

    rec = jnp.where(lane == 0, i1 - EXP_LANE0, 0.0)
    rec = jnp.where(lane == 1, i2 - EXP_LANE0, rec)
    rec = jnp.where(lane == 2, w1, rec)
    rec = jnp.where(lane == 3, w2, rec)
    rec = jnp.where(lane == 4, rank1, rec)
    rec = jnp.where(lane == 5, rank2, rec)
    route_ref[...] = rec


def _outproj_router(ym, yg, h2d, wo, gain, wr3, br, tm):
    T, D = h2d.shape
    W = ym.shape[1]
    row = lambda i: (i, 0)
    const = lambda i: (0, 0)
    return pl.pallas_call(
        _outproj_router_body,
        grid=(T // tm,),
        in_specs=[pl.BlockSpec((tm, W), row), pl.BlockSpec((tm, W), row), pl.BlockSpec((tm, D), row),
                  pl.BlockSpec((2 * W, D), const), pl.BlockSpec((1, D), const),
                  pl.BlockSpec((D, 2 * LANES), const), pl.BlockSpec((1, LANES), const)],
        out_specs=[pl.BlockSpec((tm, D), row), pl.BlockSpec((tm, D), row),
                   pl.BlockSpec((tm, LANES), row), pl.BlockSpec((SUBLANES, LANES), const)],
        out_shape=[jax.ShapeDtypeStruct((T, D), F32), jax.ShapeDtypeStruct((T, D), F32),
                   jax.ShapeDtypeStruct((T, LANES), F32),
                   jax.ShapeDtypeStruct((SUBLANES, LANES), F32)],
        scratch_shapes=[pltpu.VMEM((SUBLANES, LANES), F32)],
        compiler_params=_cparams(("arbitrary",)),
        name="outproj_router",
    )(ym, yg, h2d, wo, gain, wr3, br)


def _dispatch_body(pos_ref, pad_ref, x_hbm, xs_hbm, xbuf, zbuf, load_sem, scat_sem, pad_sem,
                   *, td, n_tiles):
    i = pl.program_id(0)
    ns = xbuf.shape[0]
    slot = lax.rem(i, ns)

    def load(tile, s):
        rows = pl.ds(pl.multiple_of(tile * td, td), td)
        return pltpu.make_async_copy(x_hbm.at[rows], xbuf.at[s], load_sem.at[s])

    def wait_scatter(s):
        for _ in range(2):
            pltpu.make_async_copy(xbuf.at[s], xs_hbm.at[pl.ds(0, td)], scat_sem.at[s]).wait()

    def for_each_pad_copy(fn):
        def per_expert(e, carry):
            first = pad_ref[0, 0, e]
            count = pad_ref[0, 0, N_EXPERTS + e]
            lead = jnp.minimum(count, lax.rem(SUBLANES - lax.rem(first, SUBLANES), SUBLANES))

            def per_row(r, c):
                fn(pltpu.make_async_copy(zbuf.at[pl.ds(0, 1)], xs_hbm.at[pl.ds(first + r, 1)], pad_sem))
                return c

            def per_group(g, c):
                rows = pl.ds(pl.multiple_of(first + lead + g * SUBLANES, SUBLANES), SUBLANES)
                fn(pltpu.make_async_copy(zbuf.at[pl.ds(0, SUBLANES)], xs_hbm.at[rows], pad_sem))
                return c

            carry = lax.fori_loop(0, lead, per_row, carry)
            return lax.fori_loop(0, (count - lead) // SUBLANES, per_group, carry)

        lax.fori_loop(0, N_EXPERTS, per_expert, 0)

        def per_block(bk, carry):
            rows = pl.ds(pl.multiple_of(bk * MOE_BLOCK, MOE_BLOCK), MOE_BLOCK)
            fn(pltpu.make_async_copy(zbuf, xs_hbm.at[rows], pad_sem))
            return carry

        lax.fori_loop(pad_ref[0, 0, 2 * N_EXPERTS], xs_hbm.shape[0] // MOE_BLOCK, per_block, 0)

    @pl.when(i == 0)
    def _():
        load(0, 0).start()
        zbuf[...] = jnp.zeros(zbuf.shape, F32)
        for_each_pad_copy(lambda cp: cp.start())

    @pl.when(i + 1 < n_tiles)
    def _():
        load(i + 1, lax.rem(i + 1, ns)).start()

    load(i, slot).wait()

    for j in range(td):
        for k in range(2):
            pltpu.make_async_copy(xbuf.at[slot, pl.ds(j, 1)],
                                  xs_hbm.at[pl.ds(pos_ref[0, 0, 2 * j + k], 1)],
                                  scat_sem.at[slot]).start(priority=k)

    @pl.when(i > 0)
    def _():
        wait_scatter(lax.rem(i + ns - 1, ns))

    @pl.when(i == n_tiles - 1)
    def _():
        wait_scatter(slot)
        for_each_pad_copy(lambda cp: cp.wait())


def _dispatch(pos, pads, xn, n_rows, td):
    T, D = xn.shape
    n = T // td
    return pl.pallas_call(
        functools.partial(_dispatch_body, td=td, n_tiles=n),
        grid=(n,),
        in_specs=[pl.BlockSpec((1, 1, 2 * td), lambda i: (i, 0, 0), memory_space=pltpu.SMEM),
                  pl.BlockSpec((1, 1, 2 * N_EXPERTS + 1), lambda i: (0, 0, 0),
                               memory_space=pltpu.SMEM),
                  pl.BlockSpec(memory_space=pl.ANY)],
        out_specs=pl.BlockSpec(memory_space=pl.ANY),
        out_shape=jax.ShapeDtypeStruct((n_rows, D), F32),
        scratch_shapes=[pltpu.VMEM((3, td, D), F32), pltpu.VMEM((MOE_BLOCK, D), F32),
                        pltpu.SemaphoreType.DMA((3,)), pltpu.SemaphoreType.DMA((3,)),
                        pltpu.SemaphoreType.DMA(())],
        compiler_params=_cparams(("arbitrary",)),
        name="dispatch",
    )(pos.reshape(n, 1, 2 * td), pads.reshape(1, 1, 2 * N_EXPERTS + 1), xn)


def _ffn_body(blk_exp_ref, nused_ref, xs_ref, wgu_ref, wd_ref, ys_ref, wgu_bf, wd_bf):
    b = pl.program_id(0)
    de = wd_ref.shape[2]
    used = b < nused_ref[0]
    new_expert = (b == 0) | (blk_exp_ref[b] != blk_exp_ref[jnp.maximum(b - 1, 0)])

    @pl.when(used & new_expert)
    def _():
        wgu_bf[...] = wgu_ref[0, 0].astype(BF16)
        wd_bf[...] = wd_ref[0, 0].astype(BF16)

    @pl.when(used)
    def _():
        gu = jnp.dot(xs_ref[...].astype(BF16), wgu_bf[...], preferred_element_type=F32)
        gate, up = gu[:, 0:de], gu[:, de:2 * de]
        act = (gate * _sigmoid(gate) * up).astype(BF16)
        ys_ref[...] = jnp.dot(act, wd_bf[...], preferred_element_type=F32)

    @pl.when(jnp.logical_not(used))
    def _():
        ys_ref[...] = jnp.zeros(ys_ref.shape, F32)


def _ffn(blk_exp, nused, xs, wgu, wd, layer):
    NR, D = xs.shape
    nb = NR // MOE_BLOCK
    de = wd.shape[2]
    grid_spec = pltpu.PrefetchScalarGridSpec(
        num_scalar_prefetch=2,
        grid=(nb,),
        in_specs=[pl.BlockSpec((MOE_BLOCK, D), lambda b, be, nu: (jnp.minimum(b, nu[0] - 1), 0)),
                  pl.BlockSpec((1, 1, D, 2 * de), lambda b, be, nu: (layer, be[b], 0, 0)),
                  pl.BlockSpec((1, 1, de, D), lambda b, be, nu: (layer, be[b], 0, 0))],
        out_specs=pl.BlockSpec((MOE_BLOCK, D), lambda b, be, nu: (b, 0)),
        scratch_shapes=[pltpu.VMEM((D, 2 * de), BF16), pltpu.VMEM((de, D), BF16)],
    )
    return pl.pallas_call(
        _ffn_body,
        grid_spec=grid_spec,
        out_shape=jax.ShapeDtypeStruct((NR, D), F32),
        compiler_params=_cparams(("arbitrary",)),
        name="expert_ffn",
    )(blk_exp, nused, xs, wgu, wd)


def _combine_body(pos_ref, posn_ref, ys_hbm, h_ref, route_ref, gfin_ref, out_ref, gbuf, sem,
                  *, tc, n_tiles, final):
    i = pl.program_id(0)
    slot = lax.rem(i, 2)

    def issue_tile(p_ref, s):
        def issue(jb, carry):
            for u in range(ROW_UNROLL):
                j = jb * ROW_UNROLL + u
                for k in range(2):
                    pltpu.make_async_copy(ys_hbm.at[pl.ds(p_ref[0, 0, 2 * j + k], 1)],
                                          gbuf.at[s, k, pl.ds(j, 1)], sem.at[s]).start(priority=k)
            return carry

        lax.fori_loop(0, tc // ROW_UNROLL, issue, 0)

    @pl.when(i == 0)
    def _():
        issue_tile(pos_ref, 0)

    @pl.when(i + 1 < n_tiles)
    def _():
        for j in range(tc):
            for k in range(2):
                pltpu.make_async_copy(ys_hbm.at[pl.ds(posn_ref[0, 0, 2 * j + k], 1)],
                                      gbuf.at[1 - slot, k, pl.ds(j, 1)],
                                      sem.at[1 - slot]).start(priority=k)

    for k in range(2):
        pltpu.make_async_copy(ys_hbm.at[pl.ds(0, tc)], gbuf.at[slot, k], sem.at[slot]).wait()

    w1 = route_ref[:, 2:3]
    w2 = route_ref[:, 3:4]
    out = h_ref[...] + (w1 * gbuf[slot, 0] + w2 * gbuf[slot, 1])
    if final:
        out = out * lax.rsqrt(jnp.mean(out * out, axis=-1, keepdims=True) + EPS) * gfin_ref[...]
    out_ref[...] = out


def _combine(pos, ys, h2d, route, gfin, tc, final):
    T, D = h2d.shape
    n = T // tc
    pos3 = pos.reshape(n, 1, 2 * tc)
    row = lambda i: (i, 0)
    smem = functools.partial(pl.BlockSpec, (1, 1, 2 * tc), memory_space=pltpu.SMEM)
    return pl.pallas_call(
        functools.partial(_combine_body, tc=tc, n_tiles=n, final=final),
        grid=(n,),
        in_specs=[smem(index_map=lambda i: (i, 0, 0)),
                  smem(index_map=lambda i: (jnp.minimum(i + 1, n - 1), 0, 0)),
                  pl.BlockSpec(memory_space=pl.ANY),
                  pl.BlockSpec((tc, D), row), pl.BlockSpec((tc, LANES), row),
                  pl.BlockSpec((1, D), lambda i: (0, 0))],
        out_specs=pl.BlockSpec((tc, D), row),
        out_shape=jax.ShapeDtypeStruct((T, D), F32),
        scratch_shapes=[pltpu.VMEM((2, 2, tc, D), F32), pltpu.SemaphoreType.DMA((2,))],
        compiler_params=_cparams(("arbitrary",)),
        name="combine",
    )(pos3, pos3, ys, h2d, route, gfin)


def _pick_tile(n, pref):
    t = min(pref, n)
    while n % t:
        t //= 2
    return t


def _prep_layer(w_in, b_mlstm_gate, w_gla_gate, b_gla_gate, w_out, w_group, b_group, w_expert,
                b_expert):
    W = N_HEADS * DH
    WK = N_HEADS * GLA_DK
    D = w_in.shape[0]
    o_gates = 4 * W
    o_gla = o_gates + 2 * N_HEADS
    o_r = o_gla + 2 * WK + 2 * W
    wm = w_in[:, 0:o_gates].astype(BF16)
    wg = w_in[:, o_gla:o_r].astype(BF16)
    ws = jnp.zeros((D, LANES), F32)
    ws = ws.at[:, 0:2 * N_HEADS].set(w_in[:, o_gates:o_gla])
    ws = ws.at[:, 2 * N_HEADS:2 * N_HEADS + GLA_RANK].set(w_in[:, o_r:o_r + GLA_RANK]).astype(BF16)
    bg = jnp.zeros((1, LANES), F32).at[0, 0:2 * N_HEADS].set(b_mlstm_gate)
    wgg = jnp.zeros((LANES, WK), F32).at[2 * N_HEADS:2 * N_HEADS + GLA_RANK, :].set(w_gla_gate)
    wr = jnp.zeros((D, LANES), F32)
    wr = wr.at[:, 0:N_GROUPS].set(w_group).at[:, EXP_LANE0:EXP_LANE0 + N_EXPERTS].set(w_expert)
    br = jnp.zeros((1, LANES), F32)
    br = br.at[0, 0:N_GROUPS].set(b_group).at[0, EXP_LANE0:EXP_LANE0 + N_EXPERTS].set(b_expert)
    return dict(wm=wm, wg=wg, ws=ws, bg=bg, wgg=wgg.astype(BF16), bgg=b_gla_gate[None, :],
                wo=w_out.astype(BF16), wr3=jnp.concatenate(_split3(wr)[0:2], axis=1), br=br)


def _routing_tables(route, counts_row, T):
    eid = route[:, 0:2].astype(I32)
    rank = route[:, 4:6].astype(I32)
    counts = counts_row[EXP_LANE0:EXP_LANE0 + N_EXPERTS].astype(I32)
    pcounts = (counts + MOE_BLOCK - 1) // MOE_BLOCK * MOE_BLOCK
    pend = jnp.cumsum(pcounts)
    pstart = pend - pcounts
    onehot = eid[:, :, None] == jnp.arange(N_EXPERTS, dtype=I32)[None, None, :]
    pos = jnp.sum(jnp.where(onehot, pstart[None, None, :], 0), axis=-1) + rank
    n_blocks = (T * 2 + MOE_BLOCK - 1) // MOE_BLOCK + N_EXPERTS
    blk_start = jnp.arange(n_blocks, dtype=I32) * MOE_BLOCK
    blk_exp = jnp.minimum(jnp.sum((pend[None, :] <= blk_start[:, None]).astype(I32), axis=1),
                          N_EXPERTS - 1)
    nused = (pend[-1] // MOE_BLOCK).astype(I32).reshape(1)
    pads = jnp.concatenate([pstart + counts, pcounts - counts, nused])
    return pos, pads, blk_exp, nused, n_blocks


def kernel(x, norm_mix, w_in, conv_w, conv_b, b_mlstm_gate, w_gla_gate, b_gla_gate, head_norm, w_out, norm_ffn, w_group, b_group, w_expert, b_expert, w_gu, w_down, norm_final):
    B, S, D = x.shape
    T = B * S
    depth = w_in.shape[0]
    W = N_HEADS * DH
    tm = _pick_tile(T, 512)
    td = _pick_tile(T, 512)
    tc = _pick_tile(T, 256)
    lt = _pick_tile(S, 256)
    nb = _pick_tile(B, 2)
    h = x.reshape(T, D)
    pending = None
    for l in range(depth):
        p = _prep_layer(w_in[l], b_mlstm_gate[l], w_gla_gate[l], b_gla_gate[l], w_out[l],
                        w_group[l], b_group[l], w_expert[l], b_expert[l])
        if pending is None:
            main, gla, gates = _inproj(h, norm_mix[l][None, :], p['wm'], p['wg'], p['ws'], tm)
        else:
            h, main, gla, gates = _inproj_combine(*pending, norm_mix[l][None, :], p['wm'], p['wg'],
                                                  p['ws'], tm)
        ym = _mlstm(main, gates, conv_w[l], conv_b[l][None, :], p['bg'], head_norm[l][None, 0:W], B, S,
                    nb)
        yg = _gla(gla, gates, p['wgg'], p['bgg'], head_norm[l][None, W:2 * W], B, S, lt, nb)
        h, xn, route, cnt = _outproj_router(ym, yg, h, p['wo'], norm_ffn[l][None, :], p['wr3'],
                                            p['br'], tm)
        pos, pads, blk_exp, nused, n_blocks = _routing_tables(route, cnt[0], T)
        xs = _dispatch(pos, pads, xn, n_blocks * MOE_BLOCK, td)
        ys = _ffn(blk_exp, nused, xs, w_gu, w_down, l)
        pending = (pos, ys, h, route)
    pos, ys, h, route = pending
    h = _combine(pos, ys, h, route, norm_final[None, :], tc, final=True)
    return h.reshape(B, S, D)
```

```python
import functools

import jax
import jax.numpy as jnp
from jax import lax
from jax.experimental import pallas as pl
from jax.experimental.pallas import tpu as pltpu

F32 = jnp.float32
BF16 = jnp.bfloat16
I32 = jnp.int32

EPS = 1e-6
LANES = 128
SUBLANES = 8
VMEM_LIMIT = 56 * 1024 * 1024

N_HEADS = 4
DH = 128
GLA_DK = 64
GLA_RANK = 16
GLA_GATE_NORM = 16.0
CONV_W = 4
MLSTM_L = 128
GLA_L = 64
GLA_R = 16
N_GROUPS = 4
EPG = 8
N_EXPERTS = N_GROUPS * EPG
MOE_BLOCK = 512
EXP_LANE0 = N_GROUPS

ROW_UNROLL = 8

NEG_INF = float("-inf")


def _cparams(sem):
    return pltpu.CompilerParams(dimension_semantics=sem, vmem_limit_bytes=VMEM_LIMIT)


def _split3(x):
    hi = x.astype(BF16)
    r1 = x - hi.astype(F32)
    mid = r1.astype(BF16)
    lo = (r1 - mid.astype(F32)).astype(BF16)
    return hi, mid, lo


def _cumsum_rows(tri_bf, x):
    hi, mid, lo = _split3(x)
    dot = functools.partial(jnp.dot, preferred_element_type=F32)
    return dot(tri_bf, hi) + dot(tri_bf, mid) + dot(tri_bf, lo)


def _sigmoid(x):
    return 0.5 * jnp.tanh(0.5 * x) + 0.5


def _log_sigmoid(x):
    return jnp.minimum(x, 0.0) - jnp.log(1.0 + jnp.exp(-jnp.abs(x)))


def _inproj_body(x_ref, g_ref, wm_ref, wg_ref, ws_ref, om_ref, og_ref, os_ref):
    x = x_ref[...]
    xn = x * lax.rsqrt(jnp.mean(x * x, axis=-1, keepdims=True) + EPS) * g_ref[...]
    xb = xn.astype(BF16)
    om_ref[...] = jnp.dot(xb, wm_ref[...], preferred_element_type=F32).astype(BF16)
    og_ref[...] = jnp.dot(xb, wg_ref[...], preferred_element_type=F32).astype(BF16)
    os_ref[...] = jnp.dot(xb, ws_ref[...], preferred_element_type=F32)


def _inproj(h2d, gain, wm, wg, ws, tm):
    T, D = h2d.shape
    nm, ng, ns = wm.shape[1], wg.shape[1], ws.shape[1]
    const = lambda i: (0, 0)
    row = lambda i: (i, 0)
    return pl.pallas_call(
        _inproj_body,
        grid=(T // tm,),
        in_specs=[pl.BlockSpec((tm, D), row), pl.BlockSpec((1, D), const),
                  pl.BlockSpec((D, nm), const), pl.BlockSpec((D, ng), const),
                  pl.BlockSpec((D, ns), const)],
        out_specs=[pl.BlockSpec((tm, nm), row), pl.BlockSpec((tm, ng), row),
                   pl.BlockSpec((tm, ns), row)],
        out_shape=[jax.ShapeDtypeStruct((T, nm), BF16), jax.ShapeDtypeStruct((T, ng), BF16),
                   jax.ShapeDtypeStruct((T, ns), F32)],
        compiler_params=_cparams(("parallel",)),
        name="inproj",
    )(h2d, gain, wm, wg, ws)


def _inproj_combine_body(pos_ref, posn_ref, ys_hbm, h_ref, route_ref, g_ref, wm_ref, wg_ref, ws_ref,
                         hout_ref, om_ref, og_ref, os_ref, gbuf, sem, *, n_tiles):
    tm = h_ref.shape[0]
    i = pl.program_id(0)
    slot = lax.rem(i, 2)

    def row_copy(p_ref, s, j, k):
        return pltpu.make_async_copy(ys_hbm.at[pl.ds(p_ref[0, 0, 2 * j + k], 1)],
                                     gbuf.at[s, k, pl.ds(j, 1)], sem.at[s])

    def wait_tile(s):
        for k in range(2):
            pltpu.make_async_copy(ys_hbm.at[pl.ds(0, tm)], gbuf.at[s, k], sem.at[s]).wait()

    @pl.when(i == 0)
    def _():
        def issue(jb, carry):
            for u in range(ROW_UNROLL):
                for k in range(2):
                    row_copy(pos_ref, 0, jb * ROW_UNROLL + u, k).start(priority=k)
            return carry

        lax.fori_loop(0, tm // ROW_UNROLL, issue, 0)

    wait_tile(slot)
    x = h_ref[...] + (route_ref[:, 2:3] * gbuf[slot, 0] + route_ref[:, 3:4] * gbuf[slot, 1])
    hout_ref[...] = x

    for j in range(tm):
        for k in range(2):
            row_copy(posn_ref, 1 - slot, j, k).start(priority=k)

    xn = x * lax.rsqrt(jnp.mean(x * x, axis=-1, keepdims=True) + EPS) * g_ref[...]
    xb = xn.astype(BF16)
    om_ref[...] = jnp.dot(xb, wm_ref[...], preferred_element_type=F32).astype(BF16)
    og_ref[...] = jnp.dot(xb, wg_ref[...], preferred_element_type=F32).astype(BF16)
    os_ref[...] = jnp.dot(xb, ws_ref[...], preferred_element_type=F32)

    @pl.when(i == n_tiles - 1)
    def _():
        wait_tile(1 - slot)


def _inproj_combine(pos, ys, h2d, route, gain, wm, wg, ws, tm):
    T, D = h2d.shape
    n = T // tm
    nm, ng, ns = wm.shape[1], wg.shape[1], ws.shape[1]
    const = lambda i: (0, 0)
    row = lambda i: (i, 0)
    pos3 = pos.reshape(n, 1, 2 * tm)
    smem = functools.partial(pl.BlockSpec, (1, 1, 2 * tm), memory_space=pltpu.SMEM)
    return pl.pallas_call(
        functools.partial(_inproj_combine_body, n_tiles=n),
        grid=(n,),
        in_specs=[smem(index_map=lambda i: (i, 0, 0)),
                  smem(index_map=lambda i: (jnp.minimum(i + 1, n - 1), 0, 0)),
                  pl.BlockSpec(memory_space=pl.ANY),
                  pl.BlockSpec((tm, D), row), pl.BlockSpec((tm, LANES), row), pl.BlockSpec((1, D), const),
                  pl.BlockSpec((D, nm), const), pl.BlockSpec((D, ng), const),
                  pl.BlockSpec((D, ns), const)],
        out_specs=[pl.BlockSpec((tm, D), row), pl.BlockSpec((tm, nm), row), pl.BlockSpec((tm, ng), row),
                   pl.BlockSpec((tm, ns), row)],
        out_shape=[jax.ShapeDtypeStruct((T, D), F32), jax.ShapeDtypeStruct((T, nm), BF16),
                   jax.ShapeDtypeStruct((T, ng), BF16), jax.ShapeDtypeStruct((T, ns), F32)],
        scratch_shapes=[pltpu.VMEM((2, 2, tm, D), F32), pltpu.SemaphoreType.DMA((2,))],
        compiler_params=_cparams(("arbitrary",)),
        name="inproj_combine",
    )(pos3, pos3, ys, h2d, route, gain, wm, wg, ws)


def _mlstm_body(main_ref, gate_ref, cw_ref, cb_ref, bg_ref, hn_ref, y_ref, *scratch):
    L = MLSTM_L
    W = N_HEADS * DH
    TAIL = 2 * SUBLANES
    nb = main_ref.shape[0]
    ubufs, qkbufs, csts, msts = (scratch[i * nb:(i + 1) * nb] for i in range(4))
    dot = functools.partial(jnp.dot, preferred_element_type=F32)

    @pl.when(pl.program_id(1) == 0)
    def _():
        for bb in range(nb):
            ubufs[bb][0:TAIL, :] = jnp.zeros((TAIL, 2 * W), BF16)
            csts[bb][...] = jnp.zeros(csts[bb].shape, F32)
            msts[bb][...] = jnp.zeros(msts[bb].shape, F32)

    row = lax.broadcasted_iota(I32, (L, L), 0)
    col = lax.broadcasted_iota(I32, (L, L), 1)
    tri = col <= row
    tri_bf = jnp.where(tri, 1.0, 0.0).astype(BF16)
    lane = lax.broadcasted_iota(I32, (L, LANES), 1)
    sh_r = lax.broadcasted_iota(I32, ((CONV_W - 1) * L, L + TAIL), 0)
    sh_c = lax.broadcasted_iota(I32, ((CONV_W - 1) * L, L + TAIL), 1)
    shift = jnp.where(sh_c == (sh_r % L) + TAIL - (CONV_W - 1) + sh_r // L, 1.0, 0.0).astype(BF16)
    rp_r = lax.broadcasted_iota(I32, (LANES, 2 * N_HEADS * LANES), 0)
    rp_c = lax.broadcasted_iota(I32, (LANES, 2 * N_HEADS * LANES), 1)
    rep = jnp.where(rp_r == rp_c // LANES, 1.0, 0.0).astype(BF16)

    gates = []
    for bb in range(nb):
        ubuf = ubufs[bb]
        u = main_ref[bb, :, 0:2 * W]
        ubuf[TAIL:TAIL + L, :] = u
        shifted = dot(shift, ubuf[...])
        ubuf[0:TAIL, :] = ubuf[L:L + TAIL, :]
        acc = cb_ref[...] + u.astype(F32) * cw_ref[CONV_W - 1:CONV_W, :]
        for i in range(CONV_W - 1):
            acc = acc + shifted[i * L:(i + 1) * L, :] * cw_ref[i:i + 1, :]
        qk = acc * _sigmoid(acc)
        qkbufs[bb][:, 0:W] = qk[:, 0:W]
        qkbufs[bb][:, W:2 * W] = qk[:, W:2 * W] * (DH ** -0.5)

        gpre = gate_ref[bb] + bg_ref[...]
        xg = jnp.where(lane < N_HEADS, gpre, _log_sigmoid(gpre))
        gm = jnp.where(lane < N_HEADS, xg, _cumsum_rows(tri_bf, xg))
        hi, mid, lo = _split3(gm)
        gates.append((dot(hi, rep) + dot(mid, rep) + dot(lo, rep), gm.T))

    chains = [_mlstm_head(h, main_ref.at[bb], hn_ref, y_ref.at[bb], qkbufs[bb], csts[bb], msts[bb],
                          gates[bb], tri)
              for h in range(N_HEADS) for bb in range(nb)]
    for _ in range(MLSTM_STAGES):
        for c in chains:
            next(c)


MLSTM_STAGES = 3


def _mlstm_head(h, main_ref, hn_ref, y_ref, qkbuf, cst, mst, gates, tri):
    L = MLSTM_L
    W = N_HEADS * DH
    dot = functools.partial(jnp.dot, preferred_element_type=F32)
    g_rep, g_t = gates
    hs = slice(h * DH, (h + 1) * DH)

    q = qkbuf[:, hs].astype(BF16)
    k = qkbuf[:, W + h * DH:W + (h + 1) * DH]
    v = main_ref[:, 2 * W + h * DH:2 * W + (h + 1) * DH]
    vext = jnp.concatenate([v, jnp.ones((L, DH), BF16)], axis=1)
    cext = cst[h]
    s_raw = lax.dot_general(q, k.astype(BF16), (((1,), (1,)), ((), ())), preferred_element_type=F32)
    qc = dot(q, cext.astype(BF16))
    yield

    ig_rep = g_rep[:, h * LANES:(h + 1) * LANES]
    b_rep = g_rep[:, (N_HEADS + h) * LANES:(N_HEADS + h + 1) * LANES]
    ig_row = g_t[h:h + 1, :]
    b_row = g_t[N_HEADS + h:N_HEADS + h + 1, :]
    m_prev = mst[h]
    logd = jnp.where(tri, b_rep - b_row + ig_row, NEG_INF)
    m_inter = b_rep + m_prev
    m_j = jnp.maximum(m_inter, jnp.max(logd, axis=-1, keepdims=True))
    s = s_raw * jnp.exp(logd - m_j)
    a = jnp.exp(m_inter - m_j)
    sv = dot(s.astype(BF16), vext)
    g = b_rep[L - 1:L, :]
    m_new = jnp.maximum(g + m_prev, jnp.max(g - b_row + ig_row, axis=-1, keepdims=True))
    kw = (k * jnp.exp(g - b_rep + ig_rep - m_new)).astype(BF16)
    decay = jnp.exp(g + m_prev - m_new)
    upd = lax.dot_general(kw, vext, (((0,), (0,)), ((), ())), preferred_element_type=F32)
    yield

    num = sv[:, 0:DH] + a * qc[:, 0:DH]
    den = sv[:, DH:2 * DH] + a * qc[:, DH:2 * DH]
    hh = num / jnp.maximum(jnp.abs(den), jnp.exp(-m_j))
    cst[h] = jnp.concatenate([decay, decay], axis=1) * cext + upd
    mst[h] = m_new
    hn = hh * lax.rsqrt(jnp.mean(hh * hh, axis=-1, keepdims=True) + EPS)
    o_gate = _sigmoid(main_ref[:, 3 * W + h * DH:3 * W + (h + 1) * DH].astype(F32))
    y_ref[:, hs] = (hn * hn_ref[:, hs] * o_gate).astype(BF16)
    yield


def _mlstm(main, gates, conv_w, conv_b, b_gate, head_norm, B, S, nb):
    L = MLSTM_L
    W = N_HEADS * DH
    tok = lambda b, c: (b, c, 0)
    const = lambda b, c: (0, 0)
    return pl.pallas_call(
        _mlstm_body,
        grid=(B // nb, S // L),
        in_specs=[pl.BlockSpec((nb, L, 4 * W), tok), pl.BlockSpec((nb, L, LANES), tok),
                  pl.BlockSpec((CONV_W, 2 * W), const), pl.BlockSpec((1, 2 * W), const),
                  pl.BlockSpec((1, LANES), const), pl.BlockSpec((1, W), const)],
        out_specs=pl.BlockSpec((nb, L, W), tok),
        out_shape=jax.ShapeDtypeStruct((B, S, W), BF16),
        scratch_shapes=([pltpu.VMEM((L + 2 * SUBLANES, 2 * W), BF16)] * nb
                        + [pltpu.VMEM((L, 2 * W), F32)] * nb
                        + [pltpu.VMEM((N_HEADS, DH, 2 * DH), F32)] * nb
                        + [pltpu.VMEM((N_HEADS, 1, LANES), F32)] * nb),
        compiler_params=_cparams(("arbitrary", "arbitrary")),
        name="mlstm",
    )(main.reshape(B, S, 4 * W), gates.reshape(B, S, LANES), conv_w, conv_b, b_gate,
      head_norm).reshape(B * S, W)


GLA_STAGES = 4


def _gla_consts():
    L, R = GLA_L, GLA_R
    row = lax.broadcasted_iota(I32, (L, L), 0)
    col = lax.broadcasted_iota(I32, (L, L), 1)
    return dict(
        tri_bf=jnp.where(col <= row, 1.0, 0.0).astype(BF16),
        lane=lax.broadcasted_iota(I32, (L, LANES), 1),
        rowl=lax.broadcasted_iota(I32, (L, LANES), 0),
        lane_r=lax.broadcasted_iota(I32, (R, LANES), 1),
        row_r=lax.broadcasted_iota(I32, (R, LANES), 0),
        wsel=jnp.where(lax.broadcasted_iota(I32, (LANES, 2 * LANES), 0) // GLA_DK
                       == lax.broadcasted_iota(I32, (LANES, 2 * LANES), 1) // LANES,
                       1.0, 0.0).astype(BF16))


def _gla_pair(p, la, rows, c, gla_ref, hn_ref, y_ref, sst):
    L, R = GLA_L, GLA_R
    WK = N_HEADS * GLA_DK
    WV = N_HEADS * DH
    dot = functools.partial(jnp.dot, preferred_element_type=F32)
    nt = (((1,), (1,)), ((), ()))
    tn = (((0,), (0,)), ((), ()))
    lane, rowl, lane_r, row_r = c['lane'], c['rowl'], c['lane_r'], c['row_r']

    bc = _cumsum_rows(c['tri_bf'], la[:, p * LANES:(p + 1) * LANES])
    yield

    q2 = gla_ref[rows, p * LANES:(p + 1) * LANES].astype(F32) * (GLA_DK ** -0.5)
    k2 = gla_ref[rows, WK + p * LANES:WK + (p + 1) * LANES].astype(F32)
    g_last = bc[L - 1:L, :]
    q_in = q2 * jnp.exp(bc)
    kd = k2 * jnp.exp(g_last - bc)
    decay = jnp.exp(g_last)

    k_off = [None]
    q_off = [None]
    for j in range(1, L // R):
        rj = bc[j * R:j * R + 1, :]
        k_off.append((k2 * jnp.exp(jnp.where(rowl < j * R, rj - bc, NEG_INF))).astype(BF16))
        q_off.append(q2[j * R:(j + 1) * R, :] * jnp.exp(bc[j * R:(j + 1) * R, :] - rj))

    diag = []
    for j in range(L // R):
        qb = q2[j * R:(j + 1) * R, :]
        bq = bc[j * R:(j + 1) * R, :]
        terms = []
        for s in range(R):
            krow = k2[j * R + s:j * R + s + 1, :]
            brow = bc[j * R + s:j * R + s + 1, :]
            d = jnp.where(row_r >= s, bq - brow, NEG_INF)
            terms.append((qb * krow * jnp.exp(d)).astype(BF16))
        diag.append(dot(jnp.concatenate(terms, axis=0), c['wsel']))

    heads = []
    for hh in range(2):
        h = 2 * p + hh
        mh = (lane // GLA_DK) == hh
        mq = (lane_r // GLA_DK) == hh
        v = gla_ref[rows, 2 * WK + h * DH:2 * WK + (h + 1) * DH]
        st = sst[h]
        o_inter = lax.dot_general(jnp.where(mh, q_in, 0.0).astype(BF16), st.astype(BF16), nt,
                                  preferred_element_type=F32)
        upd = lax.dot_general(v, jnp.where(mh, kd, 0.0).astype(BF16), tn, preferred_element_type=F32)
        offs = [None] + [lax.dot_general(jnp.where(mq, q_off[j], 0.0).astype(BF16), k_off[j], nt,
                                         preferred_element_type=F32) for j in range(1, L // R)]
        heads.append((h, v, st, o_inter, upd, offs))
    yield

    outs = []
    for hh, (h, v, st, o_inter, upd, offs) in enumerate(heads):
        sst[h] = st * decay + upd
        a_rows = []
        for j in range(L // R):
            res = diag[j][:, hh * LANES:(hh + 1) * LANES]
            blk = jnp.zeros((R, LANES), F32)
            for s in range(R):
                blk = jnp.where(lane_r == j * R + s, res[s * R:(s + 1) * R, :], blk)
            blk = blk[:, 0:L]
            if j > 0:
                blk = blk + offs[j]
            a_rows.append(blk)
        a_mat = jnp.concatenate(a_rows, axis=0).astype(BF16)
        outs.append((h, o_inter, dot(a_mat, v)))
    yield

    for h, o_inter, o_intra in outs:
        o = o_inter + o_intra
        on = o * lax.rsqrt(jnp.mean(o * o, axis=-1, keepdims=True) + EPS)
        z = gla_ref[rows, 2 * WK + WV + h * DH:2 * WK + WV + (h + 1) * DH].astype(F32)
        y_ref[rows, h * DH:(h + 1) * DH] = (
            on * hn_ref[:, h * DH:(h + 1) * DH] * (z * _sigmoid(z))).astype(BF16)
    yield


def _gla_body(gla_ref, gate_ref, wgg_ref, bgg_ref, hn_ref, y_ref, *ssts, n_chunks):
    nb = gla_ref.shape[0]

    @pl.when(pl.program_id(1) == 0)
    def _():
        for bb in range(nb):
            ssts[bb][...] = jnp.zeros(ssts[bb].shape, F32)

    consts = _gla_consts()

    def step(i, carry):
        rows = pl.ds(pl.multiple_of(i * GLA_L, GLA_L), GLA_L)
        chains = []
        for bb in range(nb):
            la = _log_sigmoid(jnp.dot(gate_ref[bb, rows, :].astype(BF16), wgg_ref[...],
                                      preferred_element_type=F32) + bgg_ref[...]) / GLA_GATE_NORM
            chains += [_gla_pair(p, la, rows, consts, gla_ref.at[bb], hn_ref, y_ref.at[bb], ssts[bb])
                       for p in range(N_HEADS // 2)]
        for _ in range(GLA_STAGES):
            for ch in chains:
                next(ch)
        return carry

    lax.fori_loop(0, n_chunks, step, 0)


def _gla(gla, gates, wgg, bgg, head_norm, B, S, lt, nb):
    WK = N_HEADS * GLA_DK
    WV = N_HEADS * DH
    tok = lambda b, c: (b, c, 0)
    const = lambda b, c: (0, 0)
    return pl.pallas_call(
        functools.partial(_gla_body, n_chunks=lt // GLA_L),
        grid=(B // nb, S // lt),
        in_specs=[pl.BlockSpec((nb, lt, 2 * WK + 2 * WV), tok), pl.BlockSpec((nb, lt, LANES), tok),
                  pl.BlockSpec((LANES, WK), const), pl.BlockSpec((1, WK), const),
                  pl.BlockSpec((1, WV), const)],
        out_specs=pl.BlockSpec((nb, lt, WV), tok),
        out_shape=jax.ShapeDtypeStruct((B, S, WV), BF16),
        scratch_shapes=[pltpu.VMEM((N_HEADS, DH, LANES), F32)] * nb,
        compiler_params=_cparams(("arbitrary", "arbitrary")),
        name="gla",
    )(gla.reshape(B, S, 2 * WK + 2 * WV), gates.reshape(B, S, LANES), wgg, bgg,
      head_norm).reshape(B * S, WV)


def _outproj_router_body(ym_ref, yg_ref, h_ref, wo_ref, g_ref, wr_ref, br_ref,
                         hout_ref, xn_ref, route_ref, cnt_ref, carry):
    tm = h_ref.shape[0]
    W = ym_ref.shape[1]
    dot = functools.partial(jnp.dot, preferred_element_type=F32)

    @pl.when(pl.program_id(0) == 0)
    def _():
        carry[...] = jnp.zeros(carry.shape, F32)

    hnew = h_ref[...] + dot(ym_ref[...], wo_ref[0:W, :]) + dot(yg_ref[...], wo_ref[W:2 * W, :])
    hout_ref[...] = hnew
    xn = hnew * lax.rsqrt(jnp.mean(hnew * hnew, axis=-1, keepdims=True) + EPS) * g_ref[...]
    xn_ref[...] = xn

    xh = xn.astype(BF16)
    xm = (xn - xh.astype(F32)).astype(BF16)
    l2 = dot(xh, wr_ref[...])
    logits = l2[:, 0:LANES] + (l2[:, LANES:2 * LANES] + dot(xm, wr_ref[:, 0:LANES])) + br_ref[...]

    lane = lax.broadcasted_iota(I32, (tm, LANES), 1)
    lane_f = lane.astype(F32)
    big = float(LANES)
    gl = jnp.where(lane < N_GROUPS, logits, NEG_INF)
    gmax = jnp.max(gl, axis=-1, keepdims=True)
    gsel = jnp.min(jnp.where(gl == gmax, lane_f, big), axis=-1, keepdims=True)
    g_gate = 1.0 / jnp.sum(jnp.where(lane < N_GROUPS, jnp.exp(logits - gmax), 0.0),
                           axis=-1, keepdims=True)
    in_grp = ((lane >= EXP_LANE0) & (lane < EXP_LANE0 + N_EXPERTS)
              & (((lane - EXP_LANE0) // EPG).astype(F32) == gsel))
    el = jnp.where(in_grp, logits, NEG_INF)
    v1 = jnp.max(el, axis=-1, keepdims=True)
    i1 = jnp.min(jnp.where(el == v1, lane_f, big), axis=-1, keepdims=True)
    el2 = jnp.where(lane_f == i1, NEG_INF, el)
    v2 = jnp.max(el2, axis=-1, keepdims=True)
    i2 = jnp.min(jnp.where(el2 == v2, lane_f, big), axis=-1, keepdims=True)
    e2 = jnp.exp(v2 - v1)
    w1 = g_gate / (1.0 + e2)
    w2 = g_gate * e2 / (1.0 + e2)

    oh1 = lane_f == i1
    oh2 = lane_f == i2
    oh = jnp.where(oh1, 1.0, 0.0) + jnp.where(oh2, 1.0, 0.0)
    r_i = lax.broadcasted_iota(I32, (tm, tm), 0)
    c_i = lax.broadcasted_iota(I32, (tm, tm), 1)
    strict = jnp.where(c_i < r_i, 1.0, 0.0).astype(BF16)
    before = dot(strict, oh.astype(BF16)) + carry[0:1, :]
    rank1 = jnp.sum(jnp.where(oh1, before, 0.0), axis=-1, keepdims=True)
    rank2 = jnp.sum(jnp.where(oh2, before, 0.0), axis=-1, keepdims=True)
    total = carry[0:1, :] + jnp.sum(oh, axis=0, keepdims=True)
    carry[...] = jnp.broadcast_to(total, carry.shape)
    cnt_ref[...] = jnp.broadcast_to(total, cnt_ref.shape)

    rec = jnp.where(lane == 0, i1 - EXP_LANE0, 0.0)
    rec = jnp.where(lane == 1, i2 - EXP_LANE0, rec)
    rec = jnp.where(lane == 2, w1, rec)
    rec = jnp.where(lane == 3, w2, rec)
    rec = jnp.where(lane == 4, rank1, rec)
    rec = jnp.where(lane == 5, rank2, rec)
    route_ref[...] = rec


def _outproj_router(ym, yg, h2d, wo, gain, wr3, br, tm):
    T, D = h2d.shape
    W = ym.shape[1]
    row = lambda i: (i, 0)
    const = lambda i: (0, 0)
    return pl.pallas_call(
        _outproj_router_body,
        grid=(T // tm,),
        in_specs=[pl.BlockSpec((tm, W), row), pl.BlockSpec((tm, W), row), pl.BlockSpec((tm, D), row),
                  pl.BlockSpec((2 * W, D), const), pl.BlockSpec((1, D), const),
                  pl.BlockSpec((D, 2 * LANES), const), pl.BlockSpec((1, LANES), const)],
        out_specs=[pl.BlockSpec((tm, D), row), pl.BlockSpec((tm, D), row),
                   pl.BlockSpec((tm, LANES), row), pl.BlockSpec((SUBLANES, LANES), const)],
        out_shape=[jax.ShapeDtypeStruct((T, D), F32), jax.ShapeDtypeStruct((T, D), F32),
                   jax.ShapeDtypeStruct((T, LANES), F32),
                   jax.ShapeDtypeStruct((SUBLANES, LANES), F32)],
        scratch_shapes=[pltpu.VMEM((SUBLANES, LANES), F32)],
        compiler_params=_cparams(("arbitrary",)),
        name="outproj_router",
    )(ym, yg, h2d, wo, gain, wr3, br)


def _dispatch_body(pos_ref, pad_ref, x_hbm, xs_hbm, xbuf, zbuf, load_sem, scat_sem, pad_sem,
                   *, td, n_tiles):
    i = pl.program_id(0)
    ns = xbuf.shape[0]
    slot = lax.rem(i, ns)

    def load(tile, s):
        rows = pl.ds(pl.multiple_of(tile * td, td), td)
        return pltpu.make_async_copy(x_hbm.at[rows], xbuf.at[s], load_sem.at[s])

    def wait_scatter(s):
        for _ in range(2):
            pltpu.make_async_copy(xbuf.at[s], xs_hbm.at[pl.ds(0, td)], scat_sem.at[s]).wait()

    def for_each_pad_copy(fn):
        def per_expert(e, carry):
            first = pad_ref[0, 0, e]
            count = pad_ref[0, 0, N_EXPERTS + e]
            lead = jnp.minimum(count, lax.rem(SUBLANES - lax.rem(first, SUBLANES), SUBLANES))

            def per_row(r, c):
                fn(pltpu.make_async_copy(zbuf.at[pl.ds(0, 1)], xs_hbm.at[pl.ds(first + r, 1)], pad_sem))
                return c

            def per_group(g, c):
                rows = pl.ds(pl.multiple_of(first + lead + g * SUBLANES, SUBLANES), SUBLANES)
                fn(pltpu.make_async_copy(zbuf.at[pl.ds(0, SUBLANES)], xs_hbm.at[rows], pad_sem))
                return c

            carry = lax.fori_loop(0, lead, per_row, carry)
            return lax.fori_loop(0, (count - lead) // SUBLANES, per_group, carry)

        lax.fori_loop(0, N_EXPERTS, per_expert, 0)

        def per_block(bk, carry):
            rows = pl.ds(pl.multiple_of(bk * MOE_BLOCK, MOE_BLOCK), MOE_BLOCK)
            fn(pltpu.make_async_copy(zbuf, xs_hbm.at[rows], pad_sem))
            return carry

        lax.fori_loop(pad_ref[0, 0, 2 * N_EXPERTS], xs_hbm.shape[0] // MOE_BLOCK, per_block, 0)

    @pl.when(i == 0)
    def _():
        load(0, 0).start()
        zbuf[...] = jnp.zeros(zbuf.shape, F32)
        for_each_pad_copy(lambda cp: cp.start())

    @pl.when(i + 1 < n_tiles)
    def _():
        load(i + 1, lax.rem(i + 1, ns)).start()

    load(i, slot).wait()

    for j in range(td):
        for k in range(2):
            pltpu.make_async_copy(xbuf.at[slot, pl.ds(j, 1)],
                                  xs_hbm.at[pl.ds(pos_ref[0, 0, 2 * j + k], 1)],
                                  scat_sem.at[slot]).start(priority=k)

    @pl.when(i > 0)
    def _():
        wait_scatter(lax.rem(i + ns - 1, ns))

    @pl.when(i == n_tiles - 1)
    def _():
        wait_scatter(slot)
        for_each_pad_copy(lambda cp: cp.wait())


def _dispatch(pos, pads, xn, n_rows, td):
    T, D = xn.shape
    n = T // td
    return pl.pallas_call(
        functools.partial(_dispatch_body, td=td, n_tiles=n),
        grid=(n,),
        in_specs=[pl.BlockSpec((1, 1, 2 * td), lambda i: (i, 0, 0), memory_space=pltpu.SMEM),
                  pl.BlockSpec((1, 1, 2 * N_EXPERTS + 1), lambda i: (0, 0, 0),
                               memory_space=pltpu.SMEM),
                  pl.BlockSpec(memory_space=pl.ANY)],
        out_specs=pl.BlockSpec(memory_space=pl.ANY),
        out_shape=jax.ShapeDtypeStruct((n_rows, D), F32),
        scratch_shapes=[pltpu.VMEM((3, td, D), F32), pltpu.VMEM((MOE_BLOCK, D), F32),
                        pltpu.SemaphoreType.DMA((3,)), pltpu.SemaphoreType.DMA((3,)),
                        pltpu.SemaphoreType.DMA(())],
        compiler_params=_cparams(("arbitrary",)),
        name="dispatch",
    )(pos.reshape(n, 1, 2 * td), pads.reshape(1, 1, 2 * N_EXPERTS + 1), xn)


def _ffn_body(blk_exp_ref, nused_ref, xs_ref, wgu_ref, wd_ref, ys_ref, wgu_bf, wd_bf):
    b = pl.program_id(0)
    de = wd_ref.shape[2]
    used = b < nused_ref[0]
    new_expert = (b == 0) | (blk_exp_ref[b] != blk_exp_ref[jnp.maximum(b - 1, 0)])

    @pl.when(used & new_expert)
    def _():
        wgu_bf[...] = wgu_ref[0, 0].astype(BF16)
        wd_bf[...] = wd_ref[0, 0].astype(BF16)

    @pl.when(used)
    def _():
        gu = jnp.dot(xs_ref[...].astype(BF16), wgu_bf[...], preferred_element_type=F32)
        gate, up = gu[:, 0:de], gu[:, de:2 * de]
        act = (gate * _sigmoid(gate) * up).astype(BF16)
        ys_ref[...] = jnp.dot(act, wd_bf[...], preferred_element_type=F32)

    @pl.when(jnp.logical_not(used))
    def _():
        ys_ref[...] = jnp.zeros(ys_ref.shape, F32)


def _ffn(blk_exp, nused, xs, wgu, wd, layer):
    NR, D = xs.shape
    nb = NR // MOE_BLOCK
    de = wd.shape[2]
    grid_spec = pltpu.PrefetchScalarGridSpec(
        num_scalar_prefetch=2,
        grid=(nb,),
        in_specs=[pl.BlockSpec((MOE_BLOCK, D), lambda b, be, nu: (jnp.minimum(b, nu[0] - 1), 0)),
                  pl.BlockSpec((1, 1, D, 2 * de), lambda b, be, nu: (layer, be[b], 0, 0)),
                  pl.BlockSpec((1, 1, de, D), lambda b, be, nu: (layer, be[b], 0, 0))],
        out_specs=pl.BlockSpec((MOE_BLOCK, D), lambda b, be, nu: (b, 0)),
        scratch_shapes=[pltpu.VMEM((D, 2 * de), BF16), pltpu.VMEM((de, D), BF16)],
    )
    return pl.pallas_call(
        _ffn_body,
        grid_spec=grid_spec,
        out_shape=jax.ShapeDtypeStruct((NR, D), F32),
        compiler_params=_cparams(("arbitrary",)),
        name="expert_ffn",
    )(blk_exp, nused, xs, wgu, wd)


def _combine_body(pos_ref, posn_ref, ys_hbm, h_ref, route_ref, gfin_ref, out_ref, gbuf, sem,
                  *, tc, n_tiles, final):
    i = pl.program_id(0)
    slot = lax.rem(i, 2)

    def issue_tile(p_ref, s):
        def issue(jb, carry):
            for u in range(ROW_UNROLL):
                j = jb * ROW_UNROLL + u
                for k in range(2):
                    pltpu.make_async_copy(ys_hbm.at[pl.ds(p_ref[0, 0, 2 * j + k], 1)],
                                          gbuf.at[s, k, pl.ds(j, 1)], sem.at[s]).start(priority=k)
            return carry

        lax.fori_loop(0, tc // ROW_UNROLL, issue, 0)

    @pl.when(i == 0)
    def _():
        issue_tile(pos_ref, 0)

    @pl.when(i + 1 < n_tiles)
    def _():
        for j in range(tc):
            for k in range(2):
                pltpu.make_async_copy(ys_hbm.at[pl.ds(posn_ref[0, 0, 2 * j + k], 1)],
                                      gbuf.at[1 - slot, k, pl.ds(j, 1)],
                                      sem.at[1 - slot]).start(priority=k)

    for k in range(2):
        pltpu.make_async_copy(ys_hbm.at[pl.ds(0, tc)], gbuf.at[slot, k], sem.at[slot]).wait()

    w1 = route_ref[:, 2:3]
    w2 = route_ref[:, 3:4]
    out = h_ref[...] + (w1 * gbuf[slot, 0] + w2 * gbuf[slot, 1])
    if final:
        out = out * lax.rsqrt(jnp.mean(out * out, axis=-1, keepdims=True) + EPS) * gfin_ref[...]
    out_ref[...] = out


def _combine(pos, ys, h2d, route, gfin, tc, final):
    T, D = h2d.shape
    n = T // tc
    pos3 = pos.reshape(n, 1, 2 * tc)
    row = lambda i: (i, 0)
    smem = functools.partial(pl.BlockSpec, (1, 1, 2 * tc), memory_space=pltpu.SMEM)
    return pl.pallas_call(
        functools.partial(_combine_body, tc=tc, n_tiles=n, final=final),
        grid=(n,),
        in_specs=[smem(index_map=lambda i: (i, 0, 0)),
                  smem(index_map=lambda i: (jnp.minimum(i + 1, n - 1), 0, 0)),
                  pl.BlockSpec(memory_space=pl.ANY),
                  pl.BlockSpec((tc, D), row), pl.BlockSpec((tc, LANES), row),
                  pl.BlockSpec((1, D), lambda i: (0, 0))],
        out_specs=pl.BlockSpec((tc, D), row),
        out_shape=jax.ShapeDtypeStruct((T, D), F32),
        scratch_shapes=[pltpu.VMEM((2, 2, tc, D), F32), pltpu.SemaphoreType.DMA((2,))],
        compiler_params=_cparams(("arbitrary",)),
        name="combine",
    )(pos3, pos3, ys, h2d, route, gfin)


def _pick_tile(n, pref):
    t = min(pref, n)
    while n % t:
        t //= 2
    return t


def _prep_layer(w_in, b_mlstm_gate, w_gla_gate, b_gla_gate, w_out, w_group, b_group, w_expert,
                b_expert):
    W = N_HEADS * DH
    WK = N_HEADS * GLA_DK
    D = w_in.shape[0]
    o_gates = 4 * W
    o_gla = o_gates + 2 * N_HEADS
    o_r = o_gla + 2 * WK + 2 * W
    wm = w_in[:, 0:o_gates].astype(BF16)
    wg = w_in[:, o_gla:o_r].astype(BF16)
    ws = jnp.zeros((D, LANES), F32)
    ws = ws.at[:, 0:2 * N_HEADS].set(w_in[:, o_gates:o_gla])
    ws = ws.at[:, 2 * N_HEADS:2 * N_HEADS + GLA_RANK].set(w_in[:, o_r:o_r + GLA_RANK]).astype(BF16)
    bg = jnp.zeros((1, LANES), F32).at[0, 0:2 * N_HEADS].set(b_mlstm_gate)
    wgg = jnp.zeros((LANES, WK), F32).at[2 * N_HEADS:2 * N_HEADS + GLA_RANK, :].set(w_gla_gate)
    wr = jnp.zeros((D, LANES), F32)
    wr = wr.at[:, 0:N_GROUPS].set(w_group).at[:, EXP_LANE0:EXP_LANE0 + N_EXPERTS].set(w_expert)
    br = jnp.zeros((1, LANES), F32)
    br = br.at[0, 0:N_GROUPS].set(b_group).at[0, EXP_LANE0:EXP_LANE0 + N_EXPERTS].set(b_expert)
    return dict(wm=wm, wg=wg, ws=ws, bg=bg, wgg=wgg.astype(BF16), bgg=b_gla_gate[None, :],
                wo=w_out.astype(BF16), wr3=jnp.concatenate(_split3(wr)[0:2], axis=1), br=br)


def _routing_tables(route, counts_row, T):
    eid = route[:, 0:2].astype(I32)
    rank = route[:, 4:6].astype(I32)
    counts = counts_row[EXP_LANE0:EXP_LANE0 + N_EXPERTS].astype(I32)
    pcounts = (counts + MOE_BLOCK - 1) // MOE_BLOCK * MOE_BLOCK
    pend = jnp.cumsum(pcounts)
    pstart = pend - pcounts
    onehot = eid[:, :, None] == jnp.arange(N_EXPERTS, dtype=I32)[None, None, :]
    pos = jnp.sum(jnp.where(onehot, pstart[None, None, :], 0), axis=-1) + rank
    n_blocks = (T * 2 + MOE_BLOCK - 1) // MOE_BLOCK + N_EXPERTS
    blk_start = jnp.arange(n_blocks, dtype=I32) * MOE_BLOCK
    blk_exp = jnp.minimum(jnp.sum((pend[None, :] <= blk_start[:, None]).astype(I32), axis=1),
                          N_EXPERTS - 1)
    nused = (pend[-1] // MOE_BLOCK).astype(I32).reshape(1)
    pads = jnp.concatenate([pstart + counts, pcounts - counts, nused])
    return pos, pads, blk_exp, nused, n_blocks


def kernel(x, norm_mix, w_in, conv_w, conv_b, b_mlstm_gate, w_gla_gate, b_gla_gate, head_norm, w_out, norm_ffn, w_group, b_group, w_expert, b_expert, w_gu, w_down, norm_final):
    B, S, D = x.shape
    T = B * S
    depth = w_in.shape[0]
    W = N_HEADS * DH
    tm = _pick_tile(T, 512)
    td = _pick_tile(T, 512)
    tc = _pick_tile(T, 512)
    lt = _pick_tile(S, 256)
    nb = _pick_tile(B, 4)
    h = x.reshape(T, D)
    pending = None
    for l in range(depth):
        p = _prep_layer(w_in[l], b_mlstm_gate[l], w_gla_gate[l], b_gla_gate[l], w_out[l],
                        w_group[l], b_group[l], w_expert[l], b_expert[l])
        if pending is None:
            main, gla, gates = _inproj(h, norm_mix[l][None, :], p['wm'], p['wg'], p['ws'], tm)
        else:
            h, main, gla, gates = _inproj_combine(*pending, norm_mix[l][None, :], p['wm'], p['wg'],
                                                  p['ws'], tm)
        ym = _mlstm(main, gates, conv_w[l], conv_b[l][None, :], p['bg'], head_norm[l][None, 0:W], B, S,
                    nb)
        yg = _gla(gla, gates, p['wgg'], p['bgg'], head_norm[l][None, W:2 * W], B, S, lt, nb)
        h, xn, route, cnt = _outproj_router(ym, yg, h, p['wo'], norm_ffn[l][None, :], p['wr3'],
                                            p['br'], tm)
        pos, pads, blk_exp, nused, n_blocks = _routing_tables(route, cnt[0], T)
        xs = _dispatch(pos, pads, xn, n_blocks * MOE_BLOCK, td)
        ys = _ffn(blk_exp, nused, xs, w_gu, w_down, l)
        pending = (pos, ys, h, route)
    pos, ys, h, route = pending
    h = _combine(pos, ys, h, route, norm_final[None, :], tc, final=True)
    return h.reshape(B, S, D)
```

```python
import functools

import jax
import jax.numpy as jnp
from jax import lax
from jax.experimental import pallas as pl
from jax.experimental.pallas import tpu as pltpu

F32 = jnp.float32
BF16 = jnp.bfloat16
I32 = jnp.int32

EPS = 1e-6
LANES = 128
SUBLANES = 8
VMEM_LIMIT = 56 * 1024 * 1024

N_HEADS = 4
DH = 128
GLA_DK = 64
GLA_RANK = 16
GLA_GATE_NORM = 16.0
CONV_W = 4
MLSTM_L = 128
GLA_L = 64
GLA_R = 16
N_GROUPS = 4
EPG = 8
N_EXPERTS = N_GROUPS * EPG
MOE_BLOCK = 512
EXP_LANE0 = N_GROUPS

ROW_UNROLL = 8

NEG_INF = float("-inf")


def _cparams(sem):
    return pltpu.CompilerParams(dimension_semantics=sem, vmem_limit_bytes=VMEM_LIMIT)


def _split3(x):
    hi = x.astype(BF16)
    r1 = x - hi.astype(F32)
    mid = r1.astype(BF16)
    lo = (r1 - mid.astype(F32)).astype(BF16)
    return hi, mid, lo


def _cumsum_rows(tri_bf, x):
    hi, mid, lo = _split3(x)
    dot = functools.partial(jnp.dot, preferred_element_type=F32)
    return dot(tri_bf, hi) + dot(tri_bf, mid) + dot(tri_bf, lo)


def _sigmoid(x):
    return 0.5 * jnp.tanh(0.5 * x) + 0.5


def _log_sigmoid(x):
    return jnp.minimum(x, 0.0) - jnp.log(1.0 + jnp.exp(-jnp.abs(x)))


def _inproj_body(x_ref, g_ref, wm_ref, wg_ref, ws_ref, om_ref, og_ref, os_ref):
    x = x_ref[...]
    xn = x * lax.rsqrt(jnp.mean(x * x, axis=-1, keepdims=True) + EPS) * g_ref[...]
    xb = xn.astype(BF16)
    om_ref[...] = jnp.dot(xb, wm_ref[...], preferred_element_type=F32).astype(BF16)
    og_ref[...] = jnp.dot(xb, wg_ref[...], preferred_element_type=F32).astype(BF16)
    os_ref[...] = jnp.dot(xb, ws_ref[...], preferred_element_type=F32)


def _inproj(h2d, gain, wm, wg, ws, tm):
    T, D = h2d.shape
    nm, ng, ns = wm.shape[1], wg.shape[1], ws.shape[1]
    const = lambda i: (0, 0)
    row = lambda i: (i, 0)
    return pl.pallas_call(
        _inproj_body,
        grid=(T // tm,),
        in_specs=[pl.BlockSpec((tm, D), row), pl.BlockSpec((1, D), const),
                  pl.BlockSpec((D, nm), const), pl.BlockSpec((D, ng), const),
                  pl.BlockSpec((D, ns), const)],
        out_specs=[pl.BlockSpec((tm, nm), row), pl.BlockSpec((tm, ng), row),
                   pl.BlockSpec((tm, ns), row)],
        out_shape=[jax.ShapeDtypeStruct((T, nm), BF16), jax.ShapeDtypeStruct((T, ng), BF16),
                   jax.ShapeDtypeStruct((T, ns), F32)],
        compiler_params=_cparams(("parallel",)),
        name="inproj",
    )(h2d, gain, wm, wg, ws)


def _inproj_combine_body(pos_ref, pos1_ref, pos2_ref, ys_hbm, h_ref, route_ref, g_ref, wm_ref, wg_ref,
                         ws_ref, hout_ref, om_ref, og_ref, os_ref, gbuf, sem, *, n_tiles):
    tm = h_ref.shape[0]
    ns = gbuf.shape[0]
    i = pl.program_id(0)
    slot = lax.rem(i, ns)

    def row_copy(p_ref, s, j, k):
        return pltpu.make_async_copy(ys_hbm.at[pl.ds(p_ref[0, 0, 2 * j + k], 1)],
                                     gbuf.at[s, k, pl.ds(j, 1)], sem.at[s])

    def wait_tile(s):
        for k in range(2):
            pltpu.make_async_copy(ys_hbm.at[pl.ds(0, tm)], gbuf.at[s, k], sem.at[s]).wait()

    @pl.when(i == 0)
    def _():
        def issue(jb, carry):
            for u in range(ROW_UNROLL):
                for k in range(2):
                    row_copy(pos_ref, 0, jb * ROW_UNROLL + u, k).start(priority=k)
                    row_copy(pos1_ref, 1, jb * ROW_UNROLL + u, k).start(priority=k)
            return carry

        lax.fori_loop(0, tm // ROW_UNROLL, issue, 0)

    wait_tile(slot)
    x = h_ref[...] + (route_ref[:, 2:3] * gbuf[slot, 0] + route_ref[:, 3:4] * gbuf[slot, 1])
    hout_ref[...] = x

    ahead = lax.rem(i + 2, ns)
    for j in range(tm):
        for k in range(2):
            row_copy(pos2_ref, ahead, j, k).start(priority=k)

    xn = x * lax.rsqrt(jnp.mean(x * x, axis=-1, keepdims=True) + EPS) * g_ref[...]
    xb = xn.astype(BF16)
    om_ref[...] = jnp.dot(xb, wm_ref[...], preferred_element_type=F32).astype(BF16)
    og_ref[...] = jnp.dot(xb, wg_ref[...], preferred_element_type=F32).astype(BF16)
    os_ref[...] = jnp.dot(xb, ws_ref[...], preferred_element_type=F32)

    @pl.when(i == n_tiles - 1)
    def _():
        wait_tile(lax.rem(i + 1, ns))
        wait_tile(ahead)


def _inproj_combine(pos, ys, h2d, route, gain, wm, wg, ws, tm):
    T, D = h2d.shape
    n = T // tm
    nm, ng, ns = wm.shape[1], wg.shape[1], ws.shape[1]
    const = lambda i: (0, 0)
    row = lambda i: (i, 0)
    pos3 = pos.reshape(n, 1, 2 * tm)
    smem = functools.partial(pl.BlockSpec, (1, 1, 2 * tm), memory_space=pltpu.SMEM)
    return pl.pallas_call(
        functools.partial(_inproj_combine_body, n_tiles=n),
        grid=(n,),
        in_specs=[smem(index_map=lambda i: (i, 0, 0)),
                  smem(index_map=lambda i: (jnp.minimum(i + 1, n - 1), 0, 0)),
                  smem(index_map=lambda i: (jnp.minimum(i + 2, n - 1), 0, 0)),
                  pl.BlockSpec(memory_space=pl.ANY),
                  pl.BlockSpec((tm, D), row), pl.BlockSpec((tm, LANES), row), pl.BlockSpec((1, D), const),
                  pl.BlockSpec((D, nm), const), pl.BlockSpec((D, ng), const),
                  pl.BlockSpec((D, ns), const)],
        out_specs=[pl.BlockSpec((tm, D), row), pl.BlockSpec((tm, nm), row), pl.BlockSpec((tm, ng), row),
                   pl.BlockSpec((tm, ns), row)],
        out_shape=[jax.ShapeDtypeStruct((T, D), F32), jax.ShapeDtypeStruct((T, nm), BF16),
                   jax.ShapeDtypeStruct((T, ng), BF16), jax.ShapeDtypeStruct((T, ns), F32)],
        scratch_shapes=[pltpu.VMEM((3, 2, tm, D), F32), pltpu.SemaphoreType.DMA((3,))],
        compiler_params=_cparams(("arbitrary",)),
        name="inproj_combine",
    )(pos3, pos3, pos3, ys, h2d, route, gain, wm, wg, ws)


def _mlstm_body(main_ref, gate_ref, cw_ref, cb_ref, bg_ref, hn_ref, y_ref, *scratch):
    L = MLSTM_L
    W = N_HEADS * DH
    TAIL = 2 * SUBLANES
    nb = main_ref.shape[0]
    ubufs, qkbufs, csts, msts = (scratch[i * nb:(i + 1) * nb] for i in range(4))
    dot = functools.partial(jnp.dot, preferred_element_type=F32)

    @pl.when(pl.program_id(1) == 0)
    def _():
        for bb in range(nb):
            ubufs[bb][0:TAIL, :] = jnp.zeros((TAIL, 2 * W), BF16)
            csts[bb][...] = jnp.zeros(csts[bb].shape, F32)
            msts[bb][...] = jnp.zeros(msts[bb].shape, F32)

    row = lax.broadcasted_iota(I32, (L, L), 0)
    col = lax.broadcasted_iota(I32, (L, L), 1)
    tri = col <= row
    tri_bf = jnp.where(tri, 1.0, 0.0).astype(BF16)
    lane = lax.broadcasted_iota(I32, (L, LANES), 1)
    sh_r = lax.broadcasted_iota(I32, ((CONV_W - 1) * L, L + TAIL), 0)
    sh_c = lax.broadcasted_iota(I32, ((CONV_W - 1) * L, L + TAIL), 1)
    shift = jnp.where(sh_c == (sh_r % L) + TAIL - (CONV_W - 1) + sh_r // L, 1.0, 0.0).astype(BF16)
    rp_r = lax.broadcasted_iota(I32, (LANES, 2 * N_HEADS * LANES), 0)
    rp_c = lax.broadcasted_iota(I32, (LANES, 2 * N_HEADS * LANES), 1)
    rep = jnp.where(rp_r == rp_c // LANES, 1.0, 0.0).astype(BF16)

    gates = []
    for bb in range(nb):
        ubuf = ubufs[bb]
        u = main_ref[bb, :, 0:2 * W]
        ubuf[TAIL:TAIL + L, :] = u
        shifted = dot(shift, ubuf[...])
        ubuf[0:TAIL, :] = ubuf[L:L + TAIL, :]
        acc = cb_ref[...] + u.astype(F32) * cw_ref[CONV_W - 1:CONV_W, :]
        for i in range(CONV_W - 1):
            acc = acc + shifted[i * L:(i + 1) * L, :] * cw_ref[i:i + 1, :]
        qk = acc * _sigmoid(acc)
        qkbufs[bb][:, 0:W] = qk[:, 0:W]
        qkbufs[bb][:, W:2 * W] = qk[:, W:2 * W] * (DH ** -0.5)

        gpre = gate_ref[bb] + bg_ref[...]
        xg = jnp.where(lane < N_HEADS, gpre, _log_sigmoid(gpre))
        gm = jnp.where(lane < N_HEADS, xg, _cumsum_rows(tri_bf, xg))
        hi, mid, lo = _split3(gm)
        gates.append((dot(hi, rep) + dot(mid, rep) + dot(lo, rep), gm.T))

    chains = [_mlstm_head(h, main_ref.at[bb], hn_ref, y_ref.at[bb], qkbufs[bb], csts[bb], msts[bb],
                          gates[bb], tri)
              for h in range(N_HEADS) for bb in range(nb)]
    for _ in range(MLSTM_STAGES):
        for c in chains:
            next(c)


MLSTM_STAGES = 3


def _mlstm_head(h, main_ref, hn_ref, y_ref, qkbuf, cst, mst, gates, tri):
    L = MLSTM_L
    W = N_HEADS * DH
    dot = functools.partial(jnp.dot, preferred_element_type=F32)
    g_rep, g_t = gates
    hs = slice(h * DH, (h + 1) * DH)

    q = qkbuf[:, hs].astype(BF16)
    k = qkbuf[:, W + h * DH:W + (h + 1) * DH]
    v = main_ref[:, 2 * W + h * DH:2 * W + (h + 1) * DH]
    vext = jnp.concatenate([v, jnp.ones((L, DH), BF16)], axis=1)
    cext = cst[h]
    s_raw = lax.dot_general(q, k.astype(BF16), (((1,), (1,)), ((), ())), preferred_element_type=F32)
    qc = dot(q, cext.astype(BF16))
    yield

    ig_rep = g_rep[:, h * LANES:(h + 1) * LANES]
    b_rep = g_rep[:, (N_HEADS + h) * LANES:(N_HEADS + h + 1) * LANES]
    ig_row = g_t[h:h + 1, :]
    b_row = g_t[N_HEADS + h:N_HEADS + h + 1, :]
    m_prev = mst[h]
    logd = jnp.where(tri, b_rep - b_row + ig_row, NEG_INF)
    m_inter = b_rep + m_prev
    m_j = jnp.maximum(m_inter, jnp.max(logd, axis=-1, keepdims=True))
    s = s_raw * jnp.exp(logd - m_j)
    a = jnp.exp(m_inter - m_j)
    sv = dot(s.astype(BF16), vext)
    g = b_rep[L - 1:L, :]
    m_new = jnp.maximum(g + m_prev, jnp.max(g - b_row + ig_row, axis=-1, keepdims=True))
    kw = (k * jnp.exp(g - b_rep + ig_rep - m_new)).astype(BF16)
    decay = jnp.exp(g + m_prev - m_new)
    upd = lax.dot_general(kw, vext, (((0,), (0,)), ((), ())), preferred_element_type=F32)
    yield

    num = sv[:, 0:DH] + a * qc[:, 0:DH]
    den = sv[:, DH:2 * DH] + a * qc[:, DH:2 * DH]
    hh = num / jnp.maximum(jnp.abs(den), jnp.exp(-m_j))
    cst[h] = jnp.concatenate([decay, decay], axis=1) * cext + upd
    mst[h] = m_new
    hn = hh * lax.rsqrt(jnp.mean(hh * hh, axis=-1, keepdims=True) + EPS)
    o_gate = _sigmoid(main_ref[:, 3 * W + h * DH:3 * W + (h + 1) * DH].astype(F32))
    y_ref[:, hs] = (hn * hn_ref[:, hs] * o_gate).astype(BF16)
    yield


def _mlstm(main, gates, conv_w, conv_b, b_gate, head_norm, B, S, nb):
    L = MLSTM_L
    W = N_HEADS * DH
    tok = lambda b, c: (b, c, 0)
    const = lambda b, c: (0, 0)
    return pl.pallas_call(
        _mlstm_body,
        grid=(B // nb, S // L),
        in_specs=[pl.BlockSpec((nb, L, 4 * W), tok), pl.BlockSpec((nb, L, LANES), tok),
                  pl.BlockSpec((CONV_W, 2 * W), const), pl.BlockSpec((1, 2 * W), const),
                  pl.BlockSpec((1, LANES), const), pl.BlockSpec((1, W), const)],
        out_specs=pl.BlockSpec((nb, L, W), tok),
        out_shape=jax.ShapeDtypeStruct((B, S, W), BF16),
        scratch_shapes=([pltpu.VMEM((L + 2 * SUBLANES, 2 * W), BF16)] * nb
                        + [pltpu.VMEM((L, 2 * W), F32)] * nb
                        + [pltpu.VMEM((N_HEADS, DH, 2 * DH), F32)] * nb
                        + [pltpu.VMEM((N_HEADS, 1, LANES), F32)] * nb),
        compiler_params=_cparams(("arbitrary", "arbitrary")),
        name="mlstm",
    )(main.reshape(B, S, 4 * W), gates.reshape(B, S, LANES), conv_w, conv_b, b_gate,
      head_norm).reshape(B * S, W)


GLA_STAGES = 4


def _gla_consts():
    L, R = GLA_L, GLA_R
    row = lax.broadcasted_iota(I32, (L, L), 0)
    col = lax.broadcasted_iota(I32, (L, L), 1)
    return dict(
        tri_bf=jnp.where(col <= row, 1.0, 0.0).astype(BF16),
        lane=lax.broadcasted_iota(I32, (L, LANES), 1),
        lane_r=lax.broadcasted_iota(I32, (R, LANES), 1),
        row_r=lax.broadcasted_iota(I32, (R, LANES), 0),
        wsel=jnp.where(lax.broadcasted_iota(I32, (LANES, 2 * LANES), 0) // GLA_DK
                       == lax.broadcasted_iota(I32, (LANES, 2 * LANES), 1) // LANES,
                       1.0, 0.0).astype(BF16))


def _gla_pair(p, la, rows, c, gla_ref, hn_ref, y_ref, sst):
    L, R = GLA_L, GLA_R
    WK = N_HEADS * GLA_DK
    WV = N_HEADS * DH
    dot = functools.partial(jnp.dot, preferred_element_type=F32)
    nt = (((1,), (1,)), ((), ()))
    tn = (((0,), (0,)), ((), ()))
    lane, lane_r, row_r = c['lane'], c['lane_r'], c['row_r']

    bc = _cumsum_rows(c['tri_bf'], la[:, p * LANES:(p + 1) * LANES])
    yield

    q2 = gla_ref[rows, p * LANES:(p + 1) * LANES].astype(F32) * (GLA_DK ** -0.5)
    k2 = gla_ref[rows, WK + p * LANES:WK + (p + 1) * LANES].astype(F32)
    g_last = bc[L - 1:L, :]
    q_in = q2 * jnp.exp(bc)
    kd = k2 * jnp.exp(g_last - bc)
    decay = jnp.exp(g_last)

    k_off = [None]
    q_off = [None]
    for j in range(1, L // R):
        rj = bc[j * R:j * R + 1, :]
        part = (k2[0:j * R, :] * jnp.exp(rj - bc[0:j * R, :])).astype(BF16)
        k_off.append(jnp.concatenate([part, jnp.zeros((L - j * R, LANES), BF16)], axis=0))
        q_off.append(q2[j * R:(j + 1) * R, :] * jnp.exp(bc[j * R:(j + 1) * R, :] - rj))

    half = SUBLANES
    row_h = row_r[0:half, :]
    zeros_h = jnp.zeros((half, LANES), F32)
    diag = []
    for j in range(L // R):
        qb = q2[j * R:(j + 1) * R, :]
        bq = bc[j * R:(j + 1) * R, :]
        terms = []
        for s in range(R):
            krow = k2[j * R + s:j * R + s + 1, :]
            brow = bc[j * R + s:j * R + s + 1, :]
            if s < half:
                d_top = jnp.where(row_h >= s, bq[0:half, :] - brow, NEG_INF)
                top = qb[0:half, :] * krow * jnp.exp(d_top)
                bot = qb[half:R, :] * krow * jnp.exp(bq[half:R, :] - brow)
            else:
                d_bot = jnp.where(row_h >= s - half, bq[half:R, :] - brow, NEG_INF)
                top = zeros_h
                bot = qb[half:R, :] * krow * jnp.exp(d_bot)
            terms.append(jnp.concatenate([top, bot], axis=0).astype(BF16))
        diag.append(dot(jnp.concatenate(terms, axis=0), c['wsel']))

    heads = []
    for hh in range(2):
        h = 2 * p + hh
        mh = (lane // GLA_DK) == hh
        mq = (lane_r // GLA_DK) == hh
        v = gla_ref[rows, 2 * WK + h * DH:2 * WK + (h + 1) * DH]
        st = sst[h]
        o_inter = lax.dot_general(jnp.where(mh, q_in, 0.0).astype(BF16), st.astype(BF16), nt,
                                  preferred_element_type=F32)
        upd = lax.dot_general(v, jnp.where(mh, kd, 0.0).astype(BF16), tn, preferred_element_type=F32)
        offs = [None] + [lax.dot_general(jnp.where(mq, q_off[j], 0.0).astype(BF16), k_off[j], nt,
                                         preferred_element_type=F32) for j in range(1, L // R)]
        heads.append((h, v, st, o_inter, upd, offs))
    yield

    outs = []
    for hh, (h, v, st, o_inter, upd, offs) in enumerate(heads):
        sst[h] = st * decay + upd
        a_rows = []
        for j in range(L // R):
            res = diag[j][:, hh * LANES:(hh + 1) * LANES]
            blk = jnp.zeros((R, LANES), F32)
            for s in range(R):
                blk = jnp.where(lane_r == j * R + s, res[s * R:(s + 1) * R, :], blk)
            blk = blk[:, 0:L]
            if j > 0:
                blk = blk + offs[j]
            a_rows.append(blk)
        a_mat = jnp.concatenate(a_rows, axis=0).astype(BF16)
        outs.append((h, o_inter, dot(a_mat, v)))
    yield

    for h, o_inter, o_intra in outs:
        o = o_inter + o_intra
        on = o * lax.rsqrt(jnp.mean(o * o, axis=-1, keepdims=True) + EPS)
        z = gla_ref[rows, 2 * WK + WV + h * DH:2 * WK + WV + (h + 1) * DH].astype(F32)
        y_ref[rows, h * DH:(h + 1) * DH] = (
            on * hn_ref[:, h * DH:(h + 1) * DH] * (z * _sigmoid(z))).astype(BF16)
    yield


def _gla_body(gla_ref, gate_ref, wgg_ref, bgg_ref, hn_ref, y_ref, *ssts, n_chunks):
    nb = gla_ref.shape[0]

    @pl.when(pl.program_id(1) == 0)
    def _():
        for bb in range(nb):
            ssts[bb][...] = jnp.zeros(ssts[bb].shape, F32)

    consts = _gla_consts()

    def step(i, carry):
        rows = pl.ds(pl.multiple_of(i * GLA_L, GLA_L), GLA_L)
        chains = []
        for bb in range(nb):
            la = _log_sigmoid(jnp.dot(gate_ref[bb, rows, :].astype(BF16), wgg_ref[...],
                                      preferred_element_type=F32) + bgg_ref[...]) / GLA_GATE_NORM
            chains += [_gla_pair(p, la, rows, consts, gla_ref.at[bb], hn_ref, y_ref.at[bb], ssts[bb])
                       for p in range(N_HEADS // 2)]
        for _ in range(GLA_STAGES):
            for ch in chains:
                next(ch)
        return carry

    lax.fori_loop(0, n_chunks, step, 0)


def _gla(gla, gates, wgg, bgg, head_norm, B, S, lt, nb):
    WK = N_HEADS * GLA_DK
    WV = N_HEADS * DH
    tok = lambda b, c: (b, c, 0)
    const = lambda b, c: (0, 0)
    return pl.pallas_call(
        functools.partial(_gla_body, n_chunks=lt // GLA_L),
        grid=(B // nb, S // lt),
        in_specs=[pl.BlockSpec((nb, lt, 2 * WK + 2 * WV), tok), pl.BlockSpec((nb, lt, LANES), tok),
                  pl.BlockSpec((LANES, WK), const), pl.BlockSpec((1, WK), const),
                  pl.BlockSpec((1, WV), const)],
        out_specs=pl.BlockSpec((nb, lt, WV), tok),
        out_shape=jax.ShapeDtypeStruct((B, S, WV), BF16),
        scratch_shapes=[pltpu.VMEM((N_HEADS, DH, LANES), F32)] * nb,
        compiler_params=_cparams(("arbitrary", "arbitrary")),
        name="gla",
    )(gla.reshape(B, S, 2 * WK + 2 * WV), gates.reshape(B, S, LANES), wgg, bgg,
      head_norm).reshape(B * S, WV)


def _outproj_router_body(ym_ref, yg_ref, h_ref, wo_ref, g_ref, wr_ref, br_ref,
                         hout_ref, xn_ref, route_ref, cnt_ref, carry):
    tm = h_ref.shape[0]
    W = ym_ref.shape[1]
    dot = functools.partial(jnp.dot, preferred_element_type=F32)

    @pl.when(pl.program_id(0) == 0)
    def _():
        carry[...] = jnp.zeros(carry.shape, F32)

    hnew = h_ref[...] + dot(ym_ref[...], wo_ref[0:W, :]) + dot(yg_ref[...], wo_ref[W:2 * W, :])
    hout_ref[...] = hnew
    xn = hnew * lax.rsqrt(jnp.mean(hnew * hnew, axis=-1, keepdims=True) + EPS) * g_ref[...]
    xn_ref[...] = xn

    xh = xn.astype(BF16)
    xm = (xn - xh.astype(F32)).astype(BF16)
    l2 = dot(xh, wr_ref[...])
    logits = l2[:, 0:LANES] + (l2[:, LANES:2 * LANES] + dot(xm, wr_ref[:, 0:LANES])) + br_ref[...]

    lane = lax.broadcasted_iota(I32, (tm, LANES), 1)
    lane_f = lane.astype(F32)
    big = float(LANES)
    gl = jnp.where(lane < N_GROUPS, logits, NEG_INF)
    gmax = jnp.max(gl, axis=-1, keepdims=True)
    gsel = jnp.min(jnp.where(gl == gmax, lane_f, big), axis=-1, keepdims=True)
    g_gate = 1.0 / jnp.sum(jnp.where(lane < N_GROUPS, jnp.exp(logits - gmax), 0.0),
                           axis=-1, keepdims=True)
    in_grp = ((lane >= EXP_LANE0) & (lane < EXP_LANE0 + N_EXPERTS)
              & (((lane - EXP_LANE0) // EPG).astype(F32) == gsel))
    el = jnp.where(in_grp, logits, NEG_INF)
    v1 = jnp.max(el, axis=-1, keepdims=True)
    i1 = jnp.min(jnp.where(el == v1, lane_f, big), axis=-1, keepdims=True)
    el2 = jnp.where(lane_f == i1, NEG_INF, el)
    v2 = jnp.max(el2, axis=-1, keepdims=True)
    i2 = jnp.min(jnp.where(el2 == v2, lane_f, big), axis=-1, keepdims=True)
    e2 = jnp.exp(v2 - v1)
    w1 = g_gate / (1.0 + e2)
    w2 = g_gate * e2 / (1.0 + e2)

    oh1 = lane_f == i1
    oh2 = lane_f == i2
    oh = jnp.where(oh1, 1.0, 0.0) + jnp.where(oh2, 1.0, 0.0)
    r_i = lax.broadcasted_iota(I32, (tm, tm), 0)
    c_i = lax.broadcasted_iota(I32, (tm, tm), 1)
    strict = jnp.where(c_i < r_i, 1.0, 0.0).astype(BF16)
    before = dot(strict, oh.astype(BF16)) + carry[0:1, :]
    rank1 = jnp.sum(jnp.where(oh1, before, 0.0), axis=-1, keepdims=True)
    rank2 = jnp.sum(jnp.where(oh2, before, 0.0), axis=-1, keepdims=True)
    total = carry[0:1, :] + jnp.sum(oh, axis=0, keepdims=True)
    carry[...] = jnp.broadcast_to(total, carry.shape)
    cnt_ref[...] = jnp.broadcast_to(total, cnt_ref.shape)

    rec = jnp.where(lane == 0, i1 - EXP_LANE0, 0.0)
    rec = jnp.where(lane == 1, i2 - EXP_LANE0, rec)
    rec = jnp.where(lane == 2, w1, rec)
    rec = jnp.where(lane == 3, w2, rec)
    rec = jnp.where(lane == 4, rank1, rec)
    rec = jnp.where(lane == 5, rank2, rec)
    route_ref[...] = rec


def _outproj_router(ym, yg, h2d, wo, gain, wr3, br, tm):
    T, D = h2d.shape
    W = ym.shape[1]
    row = lambda i: (i, 0)
    const = lambda i: (0, 0)
    return pl.pallas_call(
        _outproj_router_body,
        grid=(T // tm,),
        in_specs=[pl.BlockSpec((tm, W), row), pl.BlockSpec((tm, W), row), pl.BlockSpec((tm, D), row),
                  pl.BlockSpec((2 * W, D), const), pl.BlockSpec((1, D), const),
                  pl.BlockSpec((D, 2 * LANES), const), pl.BlockSpec((1, LANES), const)],
        out_specs=[pl.BlockSpec((tm, D), row), pl.BlockSpec((tm, D), row),
                   pl.BlockSpec((tm, LANES), row), pl.BlockSpec((SUBLANES, LANES), const)],
        out_shape=[jax.ShapeDtypeStruct((T, D), F32), jax.ShapeDtypeStruct((T, D), F32),
                   jax.ShapeDtypeStruct((T, LANES), F32),
                   jax.ShapeDtypeStruct((SUBLANES, LANES), F32)],
        scratch_shapes=[pltpu.VMEM((SUBLANES, LANES), F32)],
        compiler_params=_cparams(("arbitrary",)),
        name="outproj_router",
    )(ym, yg, h2d, wo, gain, wr3, br)


def _dispatch_body(pos_ref, pad_ref, x_hbm, xs_hbm, xbuf, zbuf, load_sem, scat_sem, pad_sem,
                   *, td, n_tiles):
    i = pl.program_id(0)
    ns = xbuf.shape[0]
    slot = lax.rem(i, ns)

    def load(tile, s):
        rows = pl.ds(pl.multiple_of(tile * td, td), td)
        return pltpu.make_async_copy(x_hbm.at[rows], xbuf.at[s], load_sem.at[s])

    def wait_scatter(s):
        for _ in range(2):
            pltpu.make_async_copy(xbuf.at[s], xs_hbm.at[pl.ds(0, td)], scat_sem.at[s]).wait()

    def for_each_pad_copy(fn):
        def per_expert(e, carry):
            first = pad_ref[0, 0, e]
            count = pad_ref[0, 0, N_EXPERTS + e]
            lead = jnp.minimum(count, lax.rem(SUBLANES - lax.rem(first, SUBLANES), SUBLANES))

            def per_row(r, c):
                fn(pltpu.make_async_copy(zbuf.at[pl.ds(0, 1)], xs_hbm.at[pl.ds(first + r, 1)], pad_sem))
                return c

            def per_group(g, c):
                rows = pl.ds(pl.multiple_of(first + lead + g * SUBLANES, SUBLANES), SUBLANES)
                fn(pltpu.make_async_copy(zbuf.at[pl.ds(0, SUBLANES)], xs_hbm.at[rows], pad_sem))
                return c

            carry = lax.fori_loop(0, lead, per_row, carry)
            return lax.fori_loop(0, (count - lead) // SUBLANES, per_group, carry)

        lax.fori_loop(0, N_EXPERTS, per_expert, 0)

        def per_block(bk, carry):
            rows = pl.ds(pl.multiple_of(bk * MOE_BLOCK, MOE_BLOCK), MOE_BLOCK)
            fn(pltpu.make_async_copy(zbuf, xs_hbm.at[rows], pad_sem))
            return carry

        lax.fori_loop(pad_ref[0, 0, 2 * N_EXPERTS], xs_hbm.shape[0] // MOE_BLOCK, per_block, 0)

    @pl.when(i == 0)
    def _():
        load(0, 0).start()
        zbuf[...] = jnp.zeros(zbuf.shape, F32)
        for_each_pad_copy(lambda cp: cp.start())

    @pl.when(i + 1 < n_tiles)
    def _():
        load(i + 1, lax.rem(i + 1, ns)).start()

    load(i, slot).wait()

    for j in range(td):
        for k in range(2):
            pltpu.make_async_copy(xbuf.at[slot, pl.ds(j, 1)],
                                  xs_hbm.at[pl.ds(pos_ref[0, 0, 2 * j + k], 1)],
                                  scat_sem.at[slot]).start(priority=k)

    @pl.when(i > 0)
    def _():
        wait_scatter(lax.rem(i + ns - 1, ns))

    @pl.when(i == n_tiles - 1)
    def _():
        wait_scatter(slot)
        for_each_pad_copy(lambda cp: cp.wait())


def _dispatch(pos, pads, xn, n_rows, td):
    T, D = xn.shape
    n = T // td
    return pl.pallas_call(
        functools.partial(_dispatch_body, td=td, n_tiles=n),
        grid=(n,),
        in_specs=[pl.BlockSpec((1, 1, 2 * td), lambda i: (i, 0, 0), memory_space=pltpu.SMEM),
                  pl.BlockSpec((1, 1, 2 * N_EXPERTS + 1), lambda i: (0, 0, 0),
                               memory_space=pltpu.SMEM),
                  pl.BlockSpec(memory_space=pl.ANY)],
        out_specs=pl.BlockSpec(memory_space=pl.ANY),
        out_shape=jax.ShapeDtypeStruct((n_rows, D), F32),
        scratch_shapes=[pltpu.VMEM((3, td, D), F32), pltpu.VMEM((MOE_BLOCK, D), F32),
                        pltpu.SemaphoreType.DMA((3,)), pltpu.SemaphoreType.DMA((3,)),
                        pltpu.SemaphoreType.DMA(())],
        compiler_params=_cparams(("arbitrary",)),
        name="dispatch",
    )(pos.reshape(n, 1, 2 * td), pads.reshape(1, 1, 2 * N_EXPERTS + 1), xn)


def _ffn_body(blk_exp_ref, nused_ref, xs_ref, wgu_ref, wd_ref, ys_ref, wgu_bf, wd_bf):
    b = pl.program_id(0)
    de = wd_ref.shape[2]
    used = b < nused_ref[0]
    new_expert = (b == 0) | (blk_exp_ref[b] != blk_exp_ref[jnp.maximum(b - 1, 0)])

    @pl.when(used & new_expert)
    def _():
        wgu_bf[...] = wgu_ref[0, 0].astype(BF16)
        wd_bf[...] = wd_ref[0, 0].astype(BF16)

    @pl.when(used)
    def _():
        gu = jnp.dot(xs_ref[...].astype(BF16), wgu_bf[...], preferred_element_type=F32)
        gate, up = gu[:, 0:de], gu[:, de:2 * de]
        act = (gate * _sigmoid(gate) * up).astype(BF16)
        ys_ref[...] = jnp.dot(act, wd_bf[...], preferred_element_type=F32)

    @pl.when(jnp.logical_not(used))
    def _():
        ys_ref[...] = jnp.zeros(ys_ref.shape, F32)


def _ffn(blk_exp, nused, xs, wgu, wd, layer):
    NR, D = xs.shape
    nb = NR // MOE_BLOCK
    de = wd.shape[2]
    grid_spec = pltpu.PrefetchScalarGridSpec(
        num_scalar_prefetch=2,
        grid=(nb,),
        in_specs=[pl.BlockSpec((MOE_BLOCK, D), lambda b, be, nu: (jnp.minimum(b, nu[0] - 1), 0)),
                  pl.BlockSpec((1, 1, D, 2 * de), lambda b, be, nu: (layer, be[b], 0, 0)),
                  pl.BlockSpec((1, 1, de, D), lambda b, be, nu: (layer, be[b], 0, 0))],
        out_specs=pl.BlockSpec((MOE_BLOCK, D), lambda b, be, nu: (b, 0)),
        scratch_shapes=[pltpu.VMEM((D, 2 * de), BF16), pltpu.VMEM((de, D), BF16)],
    )
    return pl.pallas_call(
        _ffn_body,
        grid_spec=grid_spec,
        out_shape=jax.ShapeDtypeStruct((NR, D), F32),
        compiler_params=_cparams(("arbitrary",)),
        name="expert_ffn",
    )(blk_exp, nused, xs, wgu, wd)


def _combine_body(pos_ref, posn_ref, ys_hbm, h_ref, route_ref, gfin_ref, out_ref, gbuf, sem,
                  *, tc, n_tiles, final):
    i = pl.program_id(0)
    slot = lax.rem(i, 2)

    def issue_tile(p_ref, s):
        def issue(jb, carry):
            for u in range(ROW_UNROLL):
                j = jb * ROW_UNROLL + u
                for k in range(2):
                    pltpu.make_async_copy(ys_hbm.at[pl.ds(p_ref[0, 0, 2 * j + k], 1)],
                                          gbuf.at[s, k, pl.ds(j, 1)], sem.at[s]).start(priority=k)
            return carry

        lax.fori_loop(0, tc // ROW_UNROLL, issue, 0)

    @pl.when(i == 0)
    def _():
        issue_tile(pos_ref, 0)

    @pl.when(i + 1 < n_tiles)
    def _():
        for j in range(tc):
            for k in range(2):
                pltpu.make_async_copy(ys_hbm.at[pl.ds(posn_ref[0, 0, 2 * j + k], 1)],
                                      gbuf.at[1 - slot, k, pl.ds(j, 1)],
                                      sem.at[1 - slot]).start(priority=k)

    for k in range(2):
        pltpu.make_async_copy(ys_hbm.at[pl.ds(0, tc)], gbuf.at[slot, k], sem.at[slot]).wait()

    w1 = route_ref[:, 2:3]
    w2 = route_ref[:, 3:4]
    out = h_ref[...] + (w1 * gbuf[slot, 0] + w2 * gbuf[slot, 1])
    if final:
        out = out * lax.rsqrt(jnp.mean(out * out, axis=-1, keepdims=True) + EPS) * gfin_ref[...]
    out_ref[...] = out


def _combine(pos, ys, h2d, route, gfin, tc, final):
    T, D = h2d.shape
    n = T // tc
    pos3 = pos.reshape(n, 1, 2 * tc)
    row = lambda i: (i, 0)
    smem = functools.partial(pl.BlockSpec, (1, 1, 2 * tc), memory_space=pltpu.SMEM)
    return pl.pallas_call(
        functools.partial(_combine_body, tc=tc, n_tiles=n, final=final),
        grid=(n,),
        in_specs=[smem(index_map=lambda i: (i, 0, 0)),
                  smem(index_map=lambda i: (jnp.minimum(i + 1, n - 1), 0, 0)),
                  pl.BlockSpec(memory_space=pl.ANY),
                  pl.BlockSpec((tc, D), row), pl.BlockSpec((tc, LANES), row),
                  pl.BlockSpec((1, D), lambda i: (0, 0))],
        out_specs=pl.BlockSpec((tc, D), row),
        out_shape=jax.ShapeDtypeStruct((T, D), F32),
        scratch_shapes=[pltpu.VMEM((2, 2, tc, D), F32), pltpu.SemaphoreType.DMA((2,))],
        compiler_params=_cparams(("arbitrary",)),
        name="combine",
    )(pos3, pos3, ys, h2d, route, gfin)


def _pick_tile(n, pref):
    t = min(pref, n)
    while n % t:
        t //= 2
    return t


def _prep_layer(w_in, b_mlstm_gate, w_gla_gate, b_gla_gate, w_out, w_group, b_group, w_expert,
                b_expert):
    W = N_HEADS * DH
    WK = N_HEADS * GLA_DK
    D = w_in.shape[0]
    o_gates = 4 * W
    o_gla = o_gates + 2 * N_HEADS
    o_r = o_gla + 2 * WK + 2 * W
    wm = w_in[:, 0:o_gates].astype(BF16)
    wg = w_in[:, o_gla:o_r].astype(BF16)
    ws = jnp.zeros((D, LANES), F32)
    ws = ws.at[:, 0:2 * N_HEADS].set(w_in[:, o_gates:o_gla])
    ws = ws.at[:, 2 * N_HEADS:2 * N_HEADS + GLA_RANK].set(w_in[:, o_r:o_r + GLA_RANK]).astype(BF16)
    bg = jnp.zeros((1, LANES), F32).at[0, 0:2 * N_HEADS].set(b_mlstm_gate)
    wgg = jnp.zeros((LANES, WK), F32).at[2 * N_HEADS:2 * N_HEADS + GLA_RANK, :].set(w_gla_gate)
    wr = jnp.zeros((D, LANES), F32)
    wr = wr.at[:, 0:N_GROUPS].set(w_group).at[:, EXP_LANE0:EXP_LANE0 + N_EXPERTS].set(w_expert)
    br = jnp.zeros((1, LANES), F32)
    br = br.at[0, 0:N_GROUPS].set(b_group).at[0, EXP_LANE0:EXP_LANE0 + N_EXPERTS].set(b_expert)
    return dict(wm=wm, wg=wg, ws=ws, bg=bg, wgg=wgg.astype(BF16), bgg=b_gla_gate[None, :],
                wo=w_out.astype(BF16), wr3=jnp.concatenate(_split3(wr)[0:2], axis=1), br=br)


def _routing_tables(route, counts_row, T):
    eid = route[:, 0:2].astype(I32)
    rank = route[:, 4:6].astype(I32)
    counts = counts_row[EXP_LANE0:EXP_LANE0 + N_EXPERTS].astype(I32)
    pcounts = (counts + MOE_BLOCK - 1) // MOE_BLOCK * MOE_BLOCK
    pend = jnp.cumsum(pcounts)
    pstart = pend - pcounts
    onehot = eid[:, :, None] == jnp.arange(N_EXPERTS, dtype=I32)[None, None, :]
    pos = jnp.sum(jnp.where(onehot, pstart[None, None, :], 0), axis=-1) + rank
    n_blocks = (T * 2 + MOE_BLOCK - 1) // MOE_BLOCK + N_EXPERTS
    blk_start = jnp.arange(n_blocks, dtype=I32) * MOE_BLOCK
    blk_exp = jnp.minimum(jnp.sum((pend[None, :] <= blk_start[:, None]).astype(I32), axis=1),
                          N_EXPERTS - 1)
    nused = (pend[-1] // MOE_BLOCK).astype(I32).reshape(1)
    pads = jnp.concatenate([pstart + counts, pcounts - counts, nused])
    return pos, pads, blk_exp, nused, n_blocks


def kernel(x, norm_mix, w_in, conv_w, conv_b, b_mlstm_gate, w_gla_gate, b_gla_gate, head_norm, w_out, norm_ffn, w_group, b_group, w_expert, b_expert, w_gu, w_down, norm_final):
    B, S, D = x.shape
    T = B * S
    depth = w_in.shape[0]
    W = N_HEADS * DH
    tm = _pick_tile(T, 512)
    td = _pick_tile(T, 512)
    tc = _pick_tile(T, 512)
    lt = _pick_tile(S, 256)
    nb = _pick_tile(B, 4)
    h = x.reshape(T, D)
    pending = None
    for l in range(depth):
        p = _prep_layer(w_in[l], b_mlstm_gate[l], w_gla_gate[l], b_gla_gate[l], w_out[l],
                        w_group[l], b_group[l], w_expert[l], b_expert[l])
        if pending is None:
            main, gla, gates = _inproj(h, norm_mix[l][None, :], p['wm'], p['wg'], p['ws'], tm)
        else:
            h, main, gla, gates = _inproj_combine(*pending, norm_mix[l][None, :], p['wm'], p['wg'],
                                                  p['ws'], tm)
        ym = _mlstm(main, gates, conv_w[l], conv_b[l][None, :], p['bg'], head_norm[l][None, 0:W], B, S,
                    nb)
        yg = _gla(gla, gates, p['wgg'], p['bgg'], head_norm[l][None, W:2 * W], B, S, lt, nb)
        h, xn, route, cnt = _outproj_router(ym, yg, h, p['wo'], norm_ffn[l][None, :], p['wr3'],
                                            p['br'], tm)
        pos, pads, blk_exp, nused, n_blocks = _routing_tables(route, cnt[0], T)
        xs = _dispatch(pos, pads, xn, n_blocks * MOE_BLOCK, td)
        ys = _ffn(blk_exp, nused, xs, w_gu, w_down, l)
        pending = (pos, ys, h, route)
    pos, ys, h, route = pending
    h = _combine(pos, ys, h, route, norm_final[None, :], tc, final=True)
    return h.reshape(B, S, D)
```

```python
import functools

import jax
import jax.numpy as jnp
from jax import lax
from jax.experimental import pallas as pl
from jax.experimental.pallas import tpu as pltpu

F32 = jnp.float32
BF16 = jnp.bfloat16
I32 = jnp.int32

EPS = 1e-6
LANES = 128
SUBLANES = 8
VMEM_LIMIT = 56 * 1024 * 1024

N_HEADS = 4
DH = 128
GLA_DK = 64
GLA_RANK = 16
GLA_GATE_NORM = 16.0
CONV_W = 4
MLSTM_L = 128
GLA_L = 64
GLA_R = 16
N_GROUPS = 4
EPG = 8
N_EXPERTS = N_GROUPS * EPG
MOE_BLOCK = 512
EXP_LANE0 = N_GROUPS

ROW_UNROLL = 8

NEG_INF = float("-inf")


def _cparams(sem):
    return pltpu.CompilerParams(dimension_semantics=sem, vmem_limit_bytes=VMEM_LIMIT)


def _split3(x):
    hi = x.astype(BF16)
    r1 = x - hi.astype(F32)
    mid = r1.astype(BF16)
    lo = (r1 - mid.astype(F32)).astype(BF16)
    return hi, mid, lo


def _cumsum_rows(tri_bf, x):
    hi, mid, lo = _split3(x)
    dot = functools.partial(jnp.dot, preferred_element_type=F32)
    return dot(tri_bf, hi) + dot(tri_bf, mid) + dot(tri_bf, lo)


def _sigmoid(x):
    return 0.5 * jnp.tanh(0.5 * x) + 0.5


def _log_sigmoid(x):
    return jnp.minimum(x, 0.0) - jnp.log(1.0 + jnp.exp(-jnp.abs(x)))


def _inproj_body(x_ref, g_ref, wm_ref, wg_ref, ws_ref, om_ref, og_ref, os_ref):
    x = x_ref[...]
    xn = x * lax.rsqrt(jnp.mean(x * x, axis=-1, keepdims=True) + EPS) * g_ref[...]
    xb = xn.astype(BF16)
    om_ref[...] = jnp.dot(xb, wm_ref[...], preferred_element_type=F32).astype(BF16)
    og_ref[...] = jnp.dot(xb, wg_ref[...], preferred_element_type=F32).astype(BF16)
    os_ref[...] = jnp.dot(xb, ws_ref[...], preferred_element_type=F32)


def _inproj(h2d, gain, wm, wg, ws, tm):
    T, D = h2d.shape
    nm, ng, ns = wm.shape[1], wg.shape[1], ws.shape[1]
    const = lambda i: (0, 0)
    row = lambda i: (i, 0)
    return pl.pallas_call(
        _inproj_body,
        grid=(T // tm,),
        in_specs=[pl.BlockSpec((tm, D), row), pl.BlockSpec((1, D), const),
                  pl.BlockSpec((D, nm), const), pl.BlockSpec((D, ng), const),
                  pl.BlockSpec((D, ns), const)],
        out_specs=[pl.BlockSpec((tm, nm), row), pl.BlockSpec((tm, ng), row),
                   pl.BlockSpec((tm, ns), row)],
        out_shape=[jax.ShapeDtypeStruct((T, nm), BF16), jax.ShapeDtypeStruct((T, ng), BF16),
                   jax.ShapeDtypeStruct((T, ns), F32)],
        compiler_params=_cparams(("parallel",)),
        name="inproj",
    )(h2d, gain, wm, wg, ws)


def _inproj_combine_body(pos_ref, pos1_ref, pos2_ref, ys_hbm, h_ref, route_ref, g_ref, wm_ref, wg_ref,
                         ws_ref, hout_ref, om_ref, og_ref, os_ref, gbuf, sem, *, n_tiles):
    tm = h_ref.shape[0]
    ns = gbuf.shape[0]
    i = pl.program_id(0)
    slot = lax.rem(i, ns)

    def row_copy(p_ref, s, j, k):
        return pltpu.make_async_copy(ys_hbm.at[pl.ds(p_ref[0, 0, 2 * j + k], 1)],
                                     gbuf.at[s, k, pl.ds(j, 1)], sem.at[s])

    def wait_tile(s):
        for k in range(2):
            pltpu.make_async_copy(ys_hbm.at[pl.ds(0, tm)], gbuf.at[s, k], sem.at[s]).wait()

    @pl.when(i == 0)
    def _():
        def issue(jb, carry):
            for u in range(ROW_UNROLL):
                for k in range(2):
                    row_copy(pos_ref, 0, jb * ROW_UNROLL + u, k).start(priority=k)
                    row_copy(pos1_ref, 1, jb * ROW_UNROLL + u, k).start(priority=k)
            return carry

        lax.fori_loop(0, tm // ROW_UNROLL, issue, 0)

    wait_tile(slot)
    x = h_ref[...] + (route_ref[:, 2:3] * gbuf[slot, 0] + route_ref[:, 3:4] * gbuf[slot, 1])
    hout_ref[...] = x

    ahead = lax.rem(i + 2, ns)
    for j in range(tm):
        for k in range(2):
            row_copy(pos2_ref, ahead, j, k).start(priority=k)

    xn = x * lax.rsqrt(jnp.mean(x * x, axis=-1, keepdims=True) + EPS) * g_ref[...]
    xb = xn.astype(BF16)
    om_ref[...] = jnp.dot(xb, wm_ref[...], preferred_element_type=F32).astype(BF16)
    og_ref[...] = jnp.dot(xb, wg_ref[...], preferred_element_type=F32).astype(BF16)
    os_ref[...] = jnp.dot(xb, ws_ref[...], preferred_element_type=F32)

    @pl.when(i == n_tiles - 1)
    def _():
        wait_tile(lax.rem(i + 1, ns))
        wait_tile(ahead)


def _inproj_combine(pos, ys, h2d, route, gain, wm, wg, ws, tm):
    T, D = h2d.shape
    n = T // tm
    nm, ng, ns = wm.shape[1], wg.shape[1], ws.shape[1]
    const = lambda i: (0, 0)
    row = lambda i: (i, 0)
    pos3 = pos.reshape(n, 1, 2 * tm)
    smem = functools.partial(pl.BlockSpec, (1, 1, 2 * tm), memory_space=pltpu.SMEM)
    return pl.pallas_call(
        functools.partial(_inproj_combine_body, n_tiles=n),
        grid=(n,),
        in_specs=[smem(index_map=lambda i: (i, 0, 0)),
                  smem(index_map=lambda i: (jnp.minimum(i + 1, n - 1), 0, 0)),
                  smem(index_map=lambda i: (jnp.minimum(i + 2, n - 1), 0, 0)),
                  pl.BlockSpec(memory_space=pl.ANY),
                  pl.BlockSpec((tm, D), row), pl.BlockSpec((tm, LANES), row), pl.BlockSpec((1, D), const),
                  pl.BlockSpec((D, nm), const), pl.BlockSpec((D, ng), const),
                  pl.BlockSpec((D, ns), const)],
        out_specs=[pl.BlockSpec((tm, D), row), pl.BlockSpec((tm, nm), row), pl.BlockSpec((tm, ng), row),
                   pl.BlockSpec((tm, ns), row)],
        out_shape=[jax.ShapeDtypeStruct((T, D), F32), jax.ShapeDtypeStruct((T, nm), BF16),
                   jax.ShapeDtypeStruct((T, ng), BF16), jax.ShapeDtypeStruct((T, ns), F32)],
        scratch_shapes=[pltpu.VMEM((3, 2, tm, D), F32), pltpu.SemaphoreType.DMA((3,))],
        compiler_params=_cparams(("arbitrary",)),
        name="inproj_combine",
    )(pos3, pos3, pos3, ys, h2d, route, gain, wm, wg, ws)


def _mlstm_body(main_ref, gate_ref, cw_ref, cb_ref, bg_ref, hn_ref, y_ref, *scratch):
    L = MLSTM_L
    W = N_HEADS * DH
    TAIL = 2 * SUBLANES
    nb = main_ref.shape[0]
    ubufs, qkbufs, csts, msts = (scratch[i * nb:(i + 1) * nb] for i in range(4))
    dot = functools.partial(jnp.dot, preferred_element_type=F32)

    @pl.when(pl.program_id(1) == 0)
    def _():
        for bb in range(nb):
            ubufs[bb][0:TAIL, :] = jnp.zeros((TAIL, 2 * W), BF16)
            csts[bb][...] = jnp.zeros(csts[bb].shape, F32)
            msts[bb][...] = jnp.zeros(msts[bb].shape, F32)

    row = lax.broadcasted_iota(I32, (L, L), 0)
    col = lax.broadcasted_iota(I32, (L, L), 1)
    tri = col <= row
    tri_bf = jnp.where(tri, 1.0, 0.0).astype(BF16)
    lane = lax.broadcasted_iota(I32, (L, LANES), 1)
    sh_r = lax.broadcasted_iota(I32, ((CONV_W - 1) * L, L + TAIL), 0)
    sh_c = lax.broadcasted_iota(I32, ((CONV_W - 1) * L, L + TAIL), 1)
    shift = jnp.where(sh_c == (sh_r % L) + TAIL - (CONV_W - 1) + sh_r // L, 1.0, 0.0).astype(BF16)
    rp_r = lax.broadcasted_iota(I32, (LANES, 2 * N_HEADS * LANES), 0)
    rp_c = lax.broadcasted_iota(I32, (LANES, 2 * N_HEADS * LANES), 1)
    rep = jnp.where(rp_r == rp_c // LANES, 1.0, 0.0).astype(BF16)

    gates = []
    for bb in range(nb):
        ubuf = ubufs[bb]
        u = main_ref[bb, :, 0:2 * W]
        ubuf[TAIL:TAIL + L, :] = u
        shifted = dot(shift, ubuf[...])
        ubuf[0:TAIL, :] = ubuf[L:L + TAIL, :]
        acc = cb_ref[...] + u.astype(F32) * cw_ref[CONV_W - 1:CONV_W, :]
        for i in range(CONV_W - 1):
            acc = acc + shifted[i * L:(i + 1) * L, :] * cw_ref[i:i + 1, :]
        qk = acc * _sigmoid(acc)
        qkbufs[bb][:, 0:W] = qk[:, 0:W]
        qkbufs[bb][:, W:2 * W] = qk[:, W:2 * W] * (DH ** -0.5)

        gpre = gate_ref[bb] + bg_ref[...]
        xg = jnp.where(lane < N_HEADS, gpre, _log_sigmoid(gpre))
        gm = jnp.where(lane < N_HEADS, xg, _cumsum_rows(tri_bf, xg))
        hi, mid, lo = _split3(gm)
        gates.append((dot(hi, rep) + dot(mid, rep) + dot(lo, rep), gm.T))

    chains = [_mlstm_head(h, main_ref.at[bb], hn_ref, y_ref.at[bb], qkbufs[bb], csts[bb], msts[bb],
                          gates[bb], tri)
              for h in range(N_HEADS) for bb in range(nb)]
    for _ in range(MLSTM_STAGES):
        for c in chains:
            next(c)


MLSTM_STAGES = 3


def _mlstm_head(h, main_ref, hn_ref, y_ref, qkbuf, cst, mst, gates, tri):
    L = MLSTM_L
    W = N_HEADS * DH
    dot = functools.partial(jnp.dot, preferred_element_type=F32)
    g_rep, g_t = gates
    hs = slice(h * DH, (h + 1) * DH)

    q = qkbuf[:, hs].astype(BF16)
    k = qkbuf[:, W + h * DH:W + (h + 1) * DH]
    v = main_ref[:, 2 * W + h * DH:2 * W + (h + 1) * DH]
    vext = jnp.concatenate([v, jnp.ones((L, DH), BF16)], axis=1)
    cext = cst[h]
    s_raw = lax.dot_general(q, k.astype(BF16), (((1,), (1,)), ((), ())), preferred_element_type=F32)
    qc = dot(q, cext.astype(BF16))
    yield

    ig_rep = g_rep[:, h * LANES:(h + 1) * LANES]
    b_rep = g_rep[:, (N_HEADS + h) * LANES:(N_HEADS + h + 1) * LANES]
    ig_row = g_t[h:h + 1, :]
    b_row = g_t[N_HEADS + h:N_HEADS + h + 1, :]
    m_prev = mst[h]
    logd = jnp.where(tri, b_rep - b_row + ig_row, NEG_INF)
    m_inter = b_rep + m_prev
    m_j = jnp.maximum(m_inter, jnp.max(logd, axis=-1, keepdims=True))
    s = s_raw * jnp.exp(logd - m_j)
    a = jnp.exp(m_inter - m_j)
    sv = dot(s.astype(BF16), vext)
    g = b_rep[L - 1:L, :]
    m_new = jnp.maximum(g + m_prev, jnp.max(g - b_row + ig_row, axis=-1, keepdims=True))
    kw = (k * jnp.exp(g - b_rep + ig_rep - m_new)).astype(BF16)
    decay = jnp.exp(g + m_prev - m_new)
    upd = lax.dot_general(kw, vext, (((0,), (0,)), ((), ())), preferred_element_type=F32)
    yield

    num = sv[:, 0:DH] + a * qc[:, 0:DH]
    den = sv[:, DH:2 * DH] + a * qc[:, DH:2 * DH]
    hh = num / jnp.maximum(jnp.abs(den), jnp.exp(-m_j))
    cst[h] = jnp.concatenate([decay, decay], axis=1) * cext + upd
    mst[h] = m_new
    hn = hh * lax.rsqrt(jnp.mean(hh * hh, axis=-1, keepdims=True) + EPS)
    o_gate = _sigmoid(main_ref[:, 3 * W + h * DH:3 * W + (h + 1) * DH].astype(F32))
    y_ref[:, hs] = (hn * hn_ref[:, hs] * o_gate).astype(BF16)
    yield


def _mlstm(main, gates, conv_w, conv_b, b_gate, head_norm, B, S, nb):
    L = MLSTM_L
    W = N_HEADS * DH
    tok = lambda b, c: (b, c, 0)
    const = lambda b, c: (0, 0)
    return pl.pallas_call(
        _mlstm_body,
        grid=(B // nb, S // L),
        in_specs=[pl.BlockSpec((nb, L, 4 * W), tok), pl.BlockSpec((nb, L, LANES), tok),
                  pl.BlockSpec((CONV_W, 2 * W), const), pl.BlockSpec((1, 2 * W), const),
                  pl.BlockSpec((1, LANES), const), pl.BlockSpec((1, W), const)],
        out_specs=pl.BlockSpec((nb, L, W), tok),
        out_shape=jax.ShapeDtypeStruct((B, S, W), BF16),
        scratch_shapes=([pltpu.VMEM((L + 2 * SUBLANES, 2 * W), BF16)] * nb
                        + [pltpu.VMEM((L, 2 * W), F32)] * nb
                        + [pltpu.VMEM((N_HEADS, DH, 2 * DH), F32)] * nb
                        + [pltpu.VMEM((N_HEADS, 1, LANES), F32)] * nb),
        compiler_params=_cparams(("arbitrary", "arbitrary")),
        name="mlstm",
    )(main.reshape(B, S, 4 * W), gates.reshape(B, S, LANES), conv_w, conv_b, b_gate,
      head_norm).reshape(B * S, W)


GLA_STAGES = 4


def _gla_consts():
    L, R = GLA_L, GLA_R
    row = lax.broadcasted_iota(I32, (L, L), 0)
    col = lax.broadcasted_iota(I32, (L, L), 1)
    return dict(
        tri_bf=jnp.where(col <= row, 1.0, 0.0).astype(BF16),
        lane=lax.broadcasted_iota(I32, (L, LANES), 1),
        lane_r=lax.broadcasted_iota(I32, (R, LANES), 1),
        row_r=lax.broadcasted_iota(I32, (R, LANES), 0),
        wsel=jnp.where(lax.broadcasted_iota(I32, (LANES, 2 * LANES), 0) // GLA_DK
                       == lax.broadcasted_iota(I32, (LANES, 2 * LANES), 1) // LANES,
                       1.0, 0.0).astype(BF16))


def _gla_pair(p, la, rows, c, gla_ref, hn_ref, y_ref, sst):
    L, R = GLA_L, GLA_R
    WK = N_HEADS * GLA_DK
    WV = N_HEADS * DH
    dot = functools.partial(jnp.dot, preferred_element_type=F32)
    nt = (((1,), (1,)), ((), ()))
    tn = (((0,), (0,)), ((), ()))
    lane, lane_r, row_r = c['lane'], c['lane_r'], c['row_r']

    bc = _cumsum_rows(c['tri_bf'], la[:, p * LANES:(p + 1) * LANES])
    yield

    q2 = gla_ref[rows, p * LANES:(p + 1) * LANES].astype(F32) * (GLA_DK ** -0.5)
    k2 = gla_ref[rows, WK + p * LANES:WK + (p + 1) * LANES].astype(F32)
    g_last = bc[L - 1:L, :]
    q_in = q2 * jnp.exp(bc)
    kd = k2 * jnp.exp(g_last - bc)
    decay = jnp.exp(g_last)

    k_off = [None]
    q_off = [None]
    for j in range(1, L // R):
        rj = bc[j * R:j * R + 1, :]
        part = (k2[0:j * R, :] * jnp.exp(rj - bc[0:j * R, :])).astype(BF16)
        k_off.append(jnp.concatenate([part, jnp.zeros((L - j * R, LANES), BF16)], axis=0))
        q_off.append(q2[j * R:(j + 1) * R, :] * jnp.exp(bc[j * R:(j + 1) * R, :] - rj))

    half = SUBLANES
    row_h = row_r[0:half, :]
    zeros_h = jnp.zeros((half, LANES), F32)
    diag = []
    for j in range(L // R):
        qb = q2[j * R:(j + 1) * R, :]
        bq = bc[j * R:(j + 1) * R, :]
        terms = []
        for s in range(R):
            krow = k2[j * R + s:j * R + s + 1, :]
            brow = bc[j * R + s:j * R + s + 1, :]
            if s < half:
                d_top = jnp.where(row_h >= s, bq[0:half, :] - brow, NEG_INF)
                top = qb[0:half, :] * krow * jnp.exp(d_top)
                bot = qb[half:R, :] * krow * jnp.exp(bq[half:R, :] - brow)
            else:
                d_bot = jnp.where(row_h >= s - half, bq[half:R, :] - brow, NEG_INF)
                top = zeros_h
                bot = qb[half:R, :] * krow * jnp.exp(d_bot)
            terms.append(jnp.concatenate([top, bot], axis=0).astype(BF16))
        diag.append(dot(jnp.concatenate(terms, axis=0), c['wsel']))

    heads = []
    for hh in range(2):
        h = 2 * p + hh
        mh = (lane // GLA_DK) == hh
        mq = (lane_r // GLA_DK) == hh
        v = gla_ref[rows, 2 * WK + h * DH:2 * WK + (h + 1) * DH]
        st = sst[h]
        o_inter = lax.dot_general(jnp.where(mh, q_in, 0.0).astype(BF16), st.astype(BF16), nt,
                                  preferred_element_type=F32)
        upd = lax.dot_general(v, jnp.where(mh, kd, 0.0).astype(BF16), tn, preferred_element_type=F32)
        offs = [None] + [lax.dot_general(jnp.where(mq, q_off[j], 0.0).astype(BF16), k_off[j], nt,
                                         preferred_element_type=F32) for j in range(1, L // R)]
        heads.append((h, v, st, o_inter, upd, offs))
    yield

    outs = []
    for hh, (h, v, st, o_inter, upd, offs) in enumerate(heads):
        sst[h] = st * decay + upd
        a_rows = []
        for j in range(L // R):
            res = diag[j][:, hh * LANES:(hh + 1) * LANES]
            blk = jnp.zeros((R, LANES), F32)
            for s in range(R):
                blk = jnp.where(lane_r == j * R + s, res[s * R:(s + 1) * R, :], blk)
            blk = blk[:, 0:L]
            if j > 0:
                blk = blk + offs[j]
            a_rows.append(blk)
        a_mat = jnp.concatenate(a_rows, axis=0).astype(BF16)
        outs.append((h, o_inter, dot(a_mat, v)))
    yield

    for h, o_inter, o_intra in outs:
        o = o_inter + o_intra
        on = o * lax.rsqrt(jnp.mean(o * o, axis=-1, keepdims=True) + EPS)
        z = gla_ref[rows, 2 * WK + WV + h * DH:2 * WK + WV + (h + 1) * DH].astype(F32)
        y_ref[rows, h * DH:(h + 1) * DH] = (
            on * hn_ref[:, h * DH:(h + 1) * DH] * (z * _sigmoid(z))).astype(BF16)
    yield


def _gla_body(gla_ref, gate_ref, wgg_ref, bgg_ref, hn_ref, y_ref, *ssts, n_chunks):
    nb = gla_ref.shape[0]

    @pl.when(pl.program_id(1) == 0)
    def _():
        for bb in range(nb):
            ssts[bb][...] = jnp.zeros(ssts[bb].shape, F32)

    consts = _gla_consts()

    def step(i, carry):
        rows = pl.ds(pl.multiple_of(i * GLA_L, GLA_L), GLA_L)
        chains = []
        for bb in range(nb):
            la = _log_sigmoid(jnp.dot(gate_ref[bb, rows, :].astype(BF16), wgg_ref[...],
                                      preferred_element_type=F32) + bgg_ref[...]) / GLA_GATE_NORM
            chains += [_gla_pair(p, la, rows, consts, gla_ref.at[bb], hn_ref, y_ref.at[bb], ssts[bb])
                       for p in range(N_HEADS // 2)]
        for _ in range(GLA_STAGES):
            for ch in chains:
                next(ch)
        return carry

    lax.fori_loop(0, n_chunks, step, 0)


def _gla(gla, gates, wgg, bgg, head_norm, B, S, lt, nb):
    WK = N_HEADS * GLA_DK
    WV = N_HEADS * DH
    tok = lambda b, c: (b, c, 0)
    const = lambda b, c: (0, 0)
    return pl.pallas_call(
        functools.partial(_gla_body, n_chunks=lt // GLA_L),
        grid=(B // nb, S // lt),
        in_specs=[pl.BlockSpec((nb, lt, 2 * WK + 2 * WV), tok), pl.BlockSpec((nb, lt, LANES), tok),
                  pl.BlockSpec((LANES, WK), const), pl.BlockSpec((1, WK), const),
                  pl.BlockSpec((1, WV), const)],
        out_specs=pl.BlockSpec((nb, lt, WV), tok),
        out_shape=jax.ShapeDtypeStruct((B, S, WV), BF16),
        scratch_shapes=[pltpu.VMEM((N_HEADS, DH, LANES), F32)] * nb,
        compiler_params=_cparams(("arbitrary", "arbitrary")),
        name="gla",
    )(gla.reshape(B, S, 2 * WK + 2 * WV), gates.reshape(B, S, LANES), wgg, bgg,
      head_norm).reshape(B * S, WV)


def _outproj_router_body(ym_ref, yg_ref, h_ref, wo_ref, g_ref, wr_ref, br_ref,
                         hout_ref, xn_ref, route_ref, cnt_ref, carry):
    tm = h_ref.shape[0]
    W = ym_ref.shape[1]
    dot = functools.partial(jnp.dot, preferred_element_type=F32)

    @pl.when(pl.program_id(0) == 0)
    def _():
        carry[...] = jnp.zeros(carry.shape, F32)

    hnew = h_ref[...] + dot(ym_ref[...], wo_ref[0:W, :]) + dot(yg_ref[...], wo_ref[W:2 * W, :])
    hout_ref[...] = hnew
    xn = hnew * lax.rsqrt(jnp.mean(hnew * hnew, axis=-1, keepdims=True) + EPS) * g_ref[...]
    xn_ref[...] = xn

    xh = xn.astype(BF16)
    xm = (xn - xh.astype(F32)).astype(BF16)
    l2 = dot(xh, wr_ref[...])
    logits = l2[:, 0:LANES] + (l2[:, LANES:2 * LANES] + dot(xm, wr_ref[:, 0:LANES])) + br_ref[...]

    lane = lax.broadcasted_iota(I32, (tm, LANES), 1)
    lane_f = lane.astype(F32)
    big = float(LANES)
    gl = jnp.where(lane < N_GROUPS, logits, NEG_INF)
    gmax = jnp.max(gl, axis=-1, keepdims=True)
    gsel = jnp.min(jnp.where(gl == gmax, lane_f, big), axis=-1, keepdims=True)
    g_gate = 1.0 / jnp.sum(jnp.where(lane < N_GROUPS, jnp.exp(logits - gmax), 0.0),
                           axis=-1, keepdims=True)
    in_grp = ((lane >= EXP_LANE0) & (lane < EXP_LANE0 + N_EXPERTS)
              & (((lane - EXP_LANE0) // EPG).astype(F32) == gsel))
    el = jnp.where(in_grp, logits, NEG_INF)
    v1 = jnp.max(el, axis=-1, keepdims=True)
    i1 = jnp.min(jnp.where(el == v1, lane_f, big), axis=-1, keepdims=True)
    el2 = jnp.where(lane_f == i1, NEG_INF, el)
    v2 = jnp.max(el2, axis=-1, keepdims=True)
    i2 = jnp.min(jnp.where(el2 == v2, lane_f, big), axis=-1, keepdims=True)
    e2 = jnp.exp(v2 - v1)
    w1 = g_gate / (1.0 + e2)
    w2 = g_gate * e2 / (1.0 + e2)

    oh1 = lane_f == i1
    oh2 = lane_f == i2
    oh = jnp.where(oh1, 1.0, 0.0) + jnp.where(oh2, 1.0, 0.0)
    r_i = lax.broadcasted_iota(I32, (tm, tm), 0)
    c_i = lax.broadcasted_iota(I32, (tm, tm), 1)
    strict = jnp.where(c_i < r_i, 1.0, 0.0).astype(BF16)
    before = dot(strict, oh.astype(BF16)) + carry[0:1, :]
    rank1 = jnp.sum(jnp.where(oh1, before, 0.0), axis=-1, keepdims=True)
    rank2 = jnp.sum(jnp.where(oh2, before, 0.0), axis=-1, keepdims=True)
    total = carry[0:1, :] + jnp.sum(oh, axis=0, keepdims=True)
    carry[...] = jnp.broadcast_to(total, carry.shape)
    cnt_ref[...] = jnp.broadcast_to(total, cnt_ref.shape)

    rec = jnp.where(lane == 0, i1 - EXP_LANE0, 0.0)
    rec = jnp.where(lane == 1, i2 - EXP_LANE0, rec)
    rec = jnp.where(lane == 2, w1, rec)
    rec = jnp.where(lane == 3, w2, rec)
    rec = jnp.where(lane == 4, rank1, rec)
    rec = jnp.where(lane == 5, rank2, rec)
    route_ref[...] = rec


def _outproj_router(ym, yg, h2d, wo, gain, wr3, br, tm):
    T, D = h2d.shape
    W = ym.shape[1]
    row = lambda i: (i, 0)
    const = lambda i: (0, 0)
    return pl.pallas_call(
        _outproj_router_body,
        grid=(T // tm,),
        in_specs=[pl.BlockSpec((tm, W), row), pl.BlockSpec((tm, W), row), pl.BlockSpec((tm, D), row),
                  pl.BlockSpec((2 * W, D), const), pl.BlockSpec((1, D), const),
                  pl.BlockSpec((D, 2 * LANES), const), pl.BlockSpec((1, LANES), const)],
        out_specs=[pl.BlockSpec((tm, D), row), pl.BlockSpec((tm, D), row),
                   pl.BlockSpec((tm, LANES), row), pl.BlockSpec((SUBLANES, LANES), const)],
        out_shape=[jax.ShapeDtypeStruct((T, D), F32), jax.ShapeDtypeStruct((T, D), F32),
                   jax.ShapeDtypeStruct((T, LANES), F32),
                   jax.ShapeDtypeStruct((SUBLANES, LANES), F32)],
        scratch_shapes=[pltpu.VMEM((SUBLANES, LANES), F32)],
        compiler_params=_cparams(("arbitrary",)),
        name="outproj_router",
    )(ym, yg, h2d, wo, gain, wr3, br)


def _dispatch_body(pos_ref, pad_ref, x_hbm, xs_hbm, xbuf, zbuf, load_sem, scat_sem, pad_sem,
                   *, td, n_tiles):
    i = pl.program_id(0)
    ns = xbuf.shape[0]
    slot = lax.rem(i, ns)

    def load(tile, s):
        rows = pl.ds(pl.multiple_of(tile * td, td), td)
        return pltpu.make_async_copy(x_hbm.at[rows], xbuf.at[s], load_sem.at[s])

    def wait_scatter(s):
        for _ in range(2):
            pltpu.make_async_copy(xbuf.at[s], xs_hbm.at[pl.ds(0, td)], scat_sem.at[s]).wait()

    def for_each_pad_copy(fn):
        def per_expert(e, carry):
            first = pad_ref[0, 0, e]
            count = pad_ref[0, 0, N_EXPERTS + e]
            lead = jnp.minimum(count, lax.rem(SUBLANES - lax.rem(first, SUBLANES), SUBLANES))

            def per_row(r, c):
                fn(pltpu.make_async_copy(zbuf.at[pl.ds(0, 1)], xs_hbm.at[pl.ds(first + r, 1)], pad_sem))
                return c

            def per_group(g, c):
                rows = pl.ds(pl.multiple_of(first + lead + g * SUBLANES, SUBLANES), SUBLANES)
                fn(pltpu.make_async_copy(zbuf.at[pl.ds(0, SUBLANES)], xs_hbm.at[rows], pad_sem))
                return c

            carry = lax.fori_loop(0, lead, per_row, carry)
            return lax.fori_loop(0, (count - lead) // SUBLANES, per_group, carry)

        lax.fori_loop(0, N_EXPERTS, per_expert, 0)

        def per_block(bk, carry):
            rows = pl.ds(pl.multiple_of(bk * MOE_BLOCK, MOE_BLOCK), MOE_BLOCK)
            fn(pltpu.make_async_copy(zbuf, xs_hbm.at[rows], pad_sem))
            return carry

        lax.fori_loop(pad_ref[0, 0, 2 * N_EXPERTS], xs_hbm.shape[0] // MOE_BLOCK, per_block, 0)

    @pl.when(i == 0)
    def _():
        load(0, 0).start()
        zbuf[...] = jnp.zeros(zbuf.shape, F32)
        for_each_pad_copy(lambda cp: cp.start())

    @pl.when(i + 1 < n_tiles)
    def _():
        load(i + 1, lax.rem(i + 1, ns)).start()

    load(i, slot).wait()

    for j in range(td):
        for k in range(2):
            pltpu.make_async_copy(xbuf.at[slot, pl.ds(j, 1)],
                                  xs_hbm.at[pl.ds(pos_ref[0, 0, 2 * j + k], 1)],
                                  scat_sem.at[slot]).start(priority=k)

    @pl.when(i > 0)
    def _():
        wait_scatter(lax.rem(i + ns - 1, ns))

    @pl.when(i == n_tiles - 1)
    def _():
        wait_scatter(slot)
        for_each_pad_copy(lambda cp: cp.wait())


def _dispatch(pos, pads, xn, n_rows, td):
    T, D = xn.shape
    n = T // td
    return pl.pallas_call(
        functools.partial(_dispatch_body, td=td, n_tiles=n),
        grid=(n,),
        in_specs=[pl.BlockSpec((1, 1, 2 * td), lambda i: (i, 0, 0), memory_space=pltpu.SMEM),
                  pl.BlockSpec((1, 1, 2 * N_EXPERTS + 1), lambda i: (0, 0, 0),
                               memory_space=pltpu.SMEM),
                  pl.BlockSpec(memory_space=pl.ANY)],
        out_specs=pl.BlockSpec(memory_space=pl.ANY),
        out_shape=jax.ShapeDtypeStruct((n_rows, D), F32),
        scratch_shapes=[pltpu.VMEM((3, td, D), F32), pltpu.VMEM((MOE_BLOCK, D), F32),
                        pltpu.SemaphoreType.DMA((3,)), pltpu.SemaphoreType.DMA((3,)),
                        pltpu.SemaphoreType.DMA(())],
        compiler_params=_cparams(("arbitrary",)),
        name="dispatch",
    )(pos.reshape(n, 1, 2 * td), pads.reshape(1, 1, 2 * N_EXPERTS + 1), xn)


def _ffn_body(blk_exp_ref, nused_ref, xs_ref, wgu_ref, wd_ref, ys_ref, wgu_bf, wd_bf):
    b = pl.program_id(0)
    de = wd_ref.shape[2]
    used = b < nused_ref[0]
    new_expert = (b == 0) | (blk_exp_ref[b] != blk_exp_ref[jnp.maximum(b - 1, 0)])

    @pl.when(used & new_expert)
    def _():
        wgu_bf[...] = wgu_ref[0, 0].astype(BF16)
        wd_bf[...] = wd_ref[0, 0].astype(BF16)

    @pl.when(used)
    def _():
        gu = jnp.dot(xs_ref[...].astype(BF16), wgu_bf[...], preferred_element_type=F32)
        gate, up = gu[:, 0:de], gu[:, de:2 * de]
        act = (gate * _sigmoid(gate) * up).astype(BF16)
        ys_ref[...] = jnp.dot(act, wd_bf[...], preferred_element_type=F32)

    @pl.when(jnp.logical_not(used))
    def _():
        ys_ref[...] = jnp.zeros(ys_ref.shape, F32)


def _ffn(blk_exp, nused, xs, wgu, wd, layer):
    NR, D = xs.shape
    nb = NR // MOE_BLOCK
    de = wd.shape[2]
    grid_spec = pltpu.PrefetchScalarGridSpec(
        num_scalar_prefetch=2,
        grid=(nb,),
        in_specs=[pl.BlockSpec((MOE_BLOCK, D), lambda b, be, nu: (jnp.minimum(b, nu[0] - 1), 0)),
                  pl.BlockSpec((1, 1, D, 2 * de), lambda b, be, nu: (layer, be[b], 0, 0)),
                  pl.BlockSpec((1, 1, de, D), lambda b, be, nu: (layer, be[b], 0, 0))],
        out_specs=pl.BlockSpec((MOE_BLOCK, D), lambda b, be, nu: (b, 0)),
        scratch_shapes=[pltpu.VMEM((D, 2 * de), BF16), pltpu.VMEM((de, D), BF16)],
    )
    return pl.pallas_call(
        _ffn_body,
        grid_spec=grid_spec,
        out_shape=jax.ShapeDtypeStruct((NR, D), F32),
        compiler_params=_cparams(("arbitrary",)),
        name="expert_ffn",
    )(blk_exp, nused, xs, wgu, wd)


def _combine_body(pos_ref, posn_ref, ys_hbm, h_ref, route_ref, gfin_ref, out_ref, gbuf, sem,
                  *, tc, n_tiles, final):
    i = pl.program_id(0)
    slot = lax.rem(i, 2)

    def issue_tile(p_ref, s):
        def issue(jb, carry):
            for u in range(ROW_UNROLL):
                j = jb * ROW_UNROLL + u
                for k in range(2):
                    pltpu.make_async_copy(ys_hbm.at[pl.ds(p_ref[0, 0, 2 * j + k], 1)],
                                          gbuf.at[s, k, pl.ds(j, 1)], sem.at[s]).start(priority=k)
            return carry

        lax.fori_loop(0, tc // ROW_UNROLL, issue, 0)

    @pl.when(i == 0)
    def _():
        issue_tile(pos_ref, 0)

    @pl.when(i + 1 < n_tiles)
    def _():
        for j in range(tc):
            for k in range(2):
                pltpu.make_async_copy(ys_hbm.at[pl.ds(posn_ref[0, 0, 2 * j + k], 1)],
                                      gbuf.at[1 - slot, k, pl.ds(j, 1)],
                                      sem.at[1 - slot]).start(priority=k)

    for k in range(2):
        pltpu.make_async_copy(ys_hbm.at[pl.ds(0, tc)], gbuf.at[slot, k], sem.at[slot]).wait()

    w1 = route_ref[:, 2:3]
    w2 = route_ref[:, 3:4]
    out = h_ref[...] + (w1 * gbuf[slot, 0] + w2 * gbuf[slot, 1])
    if final:
        out = out * lax.rsqrt(jnp.mean(out * out, axis=-1, keepdims=True) + EPS) * gfin_ref[...]
    out_ref[...] = out


def _combine(pos, ys, h2d, route, gfin, tc, final):
    T, D = h2d.shape
    n = T // tc
    pos3 = pos.reshape(n, 1, 2 * tc)
    row = lambda i: (i, 0)
    smem = functools.partial(pl.BlockSpec, (1, 1, 2 * tc), memory_space=pltpu.SMEM)
    return pl.pallas_call(
        functools.partial(_combine_body, tc=tc, n_tiles=n, final=final),
        grid=(n,),
        in_specs=[smem(index_map=lambda i: (i, 0, 0)),
                  smem(index_map=lambda i: (jnp.minimum(i + 1, n - 1), 0, 0)),
                  pl.BlockSpec(memory_space=pl.ANY),
                  pl.BlockSpec((tc, D), row), pl.BlockSpec((tc, LANES), row),
                  pl.BlockSpec((1, D), lambda i: (0, 0))],
        out_specs=pl.BlockSpec((tc, D), row),
        out_shape=jax.ShapeDtypeStruct((T, D), F32),
        scratch_shapes=[pltpu.VMEM((2, 2, tc, D), F32), pltpu.SemaphoreType.DMA((2,))],
        compiler_params=_cparams(("arbitrary",)),
        name="combine",
    )(pos3, pos3, ys, h2d, route, gfin)


def _pick_tile(n, pref):
    t = min(pref, n)
    while n % t:
        t //= 2
    return t


def _prep_layer(w_in, b_mlstm_gate, w_gla_gate, b_gla_gate, w_out, w_group, b_group, w_expert,
                b_expert):
    W = N_HEADS * DH
    WK = N_HEADS * GLA_DK
    D = w_in.shape[0]
    o_gates = 4 * W
    o_gla = o_gates + 2 * N_HEADS
    o_r = o_gla + 2 * WK + 2 * W
    wm = w_in[:, 0:o_gates].astype(BF16)
    wg = w_in[:, o_gla:o_r].astype(BF16)
    ws = jnp.zeros((D, LANES), F32)
    ws = ws.at[:, 0:2 * N_HEADS].set(w_in[:, o_gates:o_gla])
    ws = ws.at[:, 2 * N_HEADS:2 * N_HEADS + GLA_RANK].set(w_in[:, o_r:o_r + GLA_RANK]).astype(BF16)
    bg = jnp.zeros((1, LANES), F32).at[0, 0:2 * N_HEADS].set(b_mlstm_gate)
    wgg = jnp.zeros((LANES, WK), F32).at[2 * N_HEADS:2 * N_HEADS + GLA_RANK, :].set(w_gla_gate)
    wr = jnp.zeros((D, LANES), F32)
    wr = wr.at[:, 0:N_GROUPS].set(w_group).at[:, EXP_LANE0:EXP_LANE0 + N_EXPERTS].set(w_expert)
    br = jnp.zeros((1, LANES), F32)
    br = br.at[0, 0:N_GROUPS].set(b_group).at[0, EXP_LANE0:EXP_LANE0 + N_EXPERTS].set(b_expert)
    return dict(wm=wm, wg=wg, ws=ws, bg=bg, wgg=wgg.astype(BF16), bgg=b_gla_gate[None, :],
                wo=w_out.astype(BF16), wr3=jnp.concatenate(_split3(wr)[0:2], axis=1), br=br)


def _routing_tables(route, counts_row, T):
    eid = route[:, 0:2].astype(I32)
    rank = route[:, 4:6].astype(I32)
    counts = counts_row[EXP_LANE0:EXP_LANE0 + N_EXPERTS].astype(I32)
    pcounts = (counts + MOE_BLOCK - 1) // MOE_BLOCK * MOE_BLOCK
    pend = jnp.cumsum(pcounts)
    pstart = pend - pcounts
    onehot = eid[:, :, None] == jnp.arange(N_EXPERTS, dtype=I32)[None, None, :]
    pos = jnp.sum(jnp.where(onehot, pstart[None, None, :], 0), axis=-1) + rank
    n_blocks = (T * 2 + MOE_BLOCK - 1) // MOE_BLOCK + N_EXPERTS
    blk_start = jnp.arange(n_blocks, dtype=I32) * MOE_BLOCK
    blk_exp = jnp.minimum(jnp.sum((pend[None, :] <= blk_start[:, None]).astype(I32), axis=1),
                          N_EXPERTS - 1)
    nused = (pend[-1] // MOE_BLOCK).astype(I32).reshape(1)
    pads = jnp.concatenate([pstart + counts, pcounts - counts, nused])
    return pos, pads, blk_exp, nused, n_blocks


def kernel(x, norm_mix, w_in, conv_w, conv_b, b_mlstm_gate, w_gla_gate, b_gla_gate, head_norm, w_out, norm_ffn, w_group, b_group, w_expert, b_expert, w_gu, w_down, norm_final):
    B, S, D = x.shape
    T = B * S
    depth = w_in.shape[0]
    W = N_HEADS * DH
    tm = _pick_tile(T, 512)
    td = _pick_tile(T, 512)
    tc = _pick_tile(T, 512)
    lt = _pick_tile(S, 256)
    nb = _pick_tile(B, 4)
    nb_gla = _pick_tile(B, 8)
    h = x.reshape(T, D)
    pending = None
    for l in range(depth):
        p = _prep_layer(w_in[l], b_mlstm_gate[l], w_gla_gate[l], b_gla_gate[l], w_out[l],
                        w_group[l], b_group[l], w_expert[l], b_expert[l])
        if pending is None:
            main, gla, gates = _inproj(h, norm_mix[l][None, :], p['wm'], p['wg'], p['ws'], tm)
        else:
            h, main, gla, gates = _inproj_combine(*pending, norm_mix[l][None, :], p['wm'], p['wg'],
                                                  p['ws'], tm)
        ym = _mlstm(main, gates, conv_w[l], conv_b[l][None, :], p['bg'], head_norm[l][None, 0:W], B, S,
                    nb)
        yg = _gla(gla, gates, p['wgg'], p['bgg'], head_norm[l][None, W:2 * W], B, S, lt, nb_gla)
        h, xn, route, cnt = _outproj_router(ym, yg, h, p['wo'], norm_ffn[l][None, :], p['wr3'],
                                            p['br'], tm)
        pos, pads, blk_exp, nused, n_blocks = _routing_tables(route, cnt[0], T)
        xs = _dispatch(pos, pads, xn, n_blocks * MOE_BLOCK, td)
        ys = _ffn(blk_exp, nused, xs, w_gu, w_down, l)
        pending = (pos, ys, h, route)
    pos, ys, h, route = pending
    h = _combine(pos, ys, h, route, norm_final[None, :], tc, final=True)
    return h.reshape(B, S, D)
```

```python
import functools

import jax
import jax.numpy as jnp
from jax import lax
from jax.experimental import pallas as pl
from jax.experimental.pallas import tpu as pltpu

F32 = jnp.float32
BF16 = jnp.bfloat16
I32 = jnp.int32

EPS = 1e-6
LANES = 128
SUBLANES = 8
VMEM_LIMIT = 56 * 1024 * 1024

N_HEADS = 4
DH = 128
GLA_DK = 64
GLA_RANK = 16
GLA_GATE_NORM = 16.0
CONV_W = 4
MLSTM_L = 128
GLA_L = 64
GLA_R = 16
N_GROUPS = 4
EPG = 8
N_EXPERTS = N_GROUPS * EPG
MOE_BLOCK = 512
EXP_LANE0 = N_GROUPS

ROW_UNROLL = 8

NEG_INF = float("-inf")


def _cparams(sem):
    return pltpu.CompilerParams(dimension_semantics=sem, vmem_limit_bytes=VMEM_LIMIT)


def _split3(x):
    hi = x.astype(BF16)
    r1 = x - hi.astype(F32)
    mid = r1.astype(BF16)
    lo = (r1 - mid.astype(F32)).astype(BF16)
    return hi, mid, lo


def _cumsum_rows(tri_bf, x):
    hi, mid, lo = _split3(x)
    dot = functools.partial(jnp.dot, preferred_element_type=F32)
    return dot(tri_bf, hi) + dot(tri_bf, mid) + dot(tri_bf, lo)


def _sigmoid(x):
    return 0.5 * jnp.tanh(0.5 * x) + 0.5


def _log_sigmoid(x):
    return jnp.minimum(x, 0.0) - jnp.log(1.0 + jnp.exp(-jnp.abs(x)))


def _inproj_body(x_ref, g_ref, wm_ref, wg_ref, ws_ref, om_ref, og_ref, os_ref):
    x = x_ref[...]
    xn = x * lax.rsqrt(jnp.mean(x * x, axis=-1, keepdims=True) + EPS) * g_ref[...]
    xb = xn.astype(BF16)
    om_ref[...] = jnp.dot(xb, wm_ref[...], preferred_element_type=F32).astype(BF16)
    og_ref[...] = jnp.dot(xb, wg_ref[...], preferred_element_type=F32).astype(BF16)
    os_ref[...] = jnp.dot(xb, ws_ref[...], preferred_element_type=F32)


def _gate_out(T, S, tm, ns):
    spb = S // tm
    return (pl.BlockSpec((None, tm, ns), lambda i: (i // spb, i % spb, 0)),
            jax.ShapeDtypeStruct((T // S, S, ns), F32))


def _inproj(h2d, gain, wm, wg, ws, tm, S):
    T, D = h2d.shape
    nm, ng, ns = wm.shape[1], wg.shape[1], ws.shape[1]
    const = lambda i: (0, 0)
    row = lambda i: (i, 0)
    gate_spec, gate_shape = _gate_out(T, S, tm, ns)
    return pl.pallas_call(
        _inproj_body,
        grid=(T // tm,),
        in_specs=[pl.BlockSpec((tm, D), row), pl.BlockSpec((1, D), const),
                  pl.BlockSpec((D, nm), const), pl.BlockSpec((D, ng), const),
                  pl.BlockSpec((D, ns), const)],
        out_specs=[pl.BlockSpec((tm, nm), row), pl.BlockSpec((tm, ng), row), gate_spec],
        out_shape=[jax.ShapeDtypeStruct((T, nm), BF16), jax.ShapeDtypeStruct((T, ng), BF16),
                   gate_shape],
        compiler_params=_cparams(("parallel",)),
        name="inproj",
    )(h2d, gain, wm, wg, ws)


def _inproj_combine_body(pos_ref, pos1_ref, pos2_ref, ys_hbm, h_ref, route_ref, g_ref, wm_ref, wg_ref,
                         ws_ref, hout_ref, om_ref, og_ref, os_ref, gbuf, sem, *, n_tiles):
    tm = h_ref.shape[0]
    ns = gbuf.shape[0]
    i = pl.program_id(0)
    slot = lax.rem(i, ns)

    def row_copy(p_ref, s, j, k):
        return pltpu.make_async_copy(ys_hbm.at[pl.ds(p_ref[0, 0, 2 * j + k], 1)],
                                     gbuf.at[s, k, pl.ds(j, 1)], sem.at[s])

    def wait_tile(s):
        for k in range(2):
            pltpu.make_async_copy(ys_hbm.at[pl.ds(0, tm)], gbuf.at[s, k], sem.at[s]).wait()

    @pl.when(i == 0)
    def _():
        def issue(jb, carry):
            for u in range(ROW_UNROLL):
                for k in range(2):
                    row_copy(pos_ref, 0, jb * ROW_UNROLL + u, k).start(priority=k)
                    row_copy(pos1_ref, 1, jb * ROW_UNROLL + u, k).start(priority=k)
            return carry

        lax.fori_loop(0, tm // ROW_UNROLL, issue, 0)

    wait_tile(slot)
    x = h_ref[...] + (route_ref[:, 2:3] * gbuf[slot, 0] + route_ref[:, 3:4] * gbuf[slot, 1])
    hout_ref[...] = x

    ahead = lax.rem(i + 2, ns)
    for j in range(tm):
        for k in range(2):
            row_copy(pos2_ref, ahead, j, k).start(priority=k)

    xn = x * lax.rsqrt(jnp.mean(x * x, axis=-1, keepdims=True) + EPS) * g_ref[...]
    xb = xn.astype(BF16)
    om_ref[...] = jnp.dot(xb, wm_ref[...], preferred_element_type=F32).astype(BF16)
    og_ref[...] = jnp.dot(xb, wg_ref[...], preferred_element_type=F32).astype(BF16)
    os_ref[...] = jnp.dot(xb, ws_ref[...], preferred_element_type=F32)

    @pl.when(i == n_tiles - 1)
    def _():
        wait_tile(lax.rem(i + 1, ns))
        wait_tile(ahead)


def _inproj_combine(pos, ys, h2d, route, gain, wm, wg, ws, tm, S):
    T, D = h2d.shape
    n = T // tm
    nm, ng, ns = wm.shape[1], wg.shape[1], ws.shape[1]
    const = lambda i: (0, 0)
    row = lambda i: (i, 0)
    gate_spec, gate_shape = _gate_out(T, S, tm, ns)
    pos3 = pos.reshape(n, 1, 2 * tm)
    smem = functools.partial(pl.BlockSpec, (1, 1, 2 * tm), memory_space=pltpu.SMEM)
    return pl.pallas_call(
        functools.partial(_inproj_combine_body, n_tiles=n),
        grid=(n,),
        in_specs=[smem(index_map=lambda i: (i, 0, 0)),
                  smem(index_map=lambda i: (jnp.minimum(i + 1, n - 1), 0, 0)),
                  smem(index_map=lambda i: (jnp.minimum(i + 2, n - 1), 0, 0)),
                  pl.BlockSpec(memory_space=pl.ANY),
                  pl.BlockSpec((tm, D), row), pl.BlockSpec((tm, LANES), row), pl.BlockSpec((1, D), const),
                  pl.BlockSpec((D, nm), const), pl.BlockSpec((D, ng), const),
                  pl.BlockSpec((D, ns), const)],
        out_specs=[pl.BlockSpec((tm, D), row), pl.BlockSpec((tm, nm), row), pl.BlockSpec((tm, ng), row),
                   gate_spec],
        out_shape=[jax.ShapeDtypeStruct((T, D), F32), jax.ShapeDtypeStruct((T, nm), BF16),
                   jax.ShapeDtypeStruct((T, ng), BF16), gate_shape],
        scratch_shapes=[pltpu.VMEM((3, 2, tm, D), F32), pltpu.SemaphoreType.DMA((3,))],
        compiler_params=_cparams(("arbitrary",)),
        name="inproj_combine",
    )(pos3, pos3, pos3, ys, h2d, route, gain, wm, wg, ws)


def _mlstm_body(main_ref, gate_ref, cw_ref, cb_ref, bg_ref, hn_ref, y_ref, *scratch):
    L = MLSTM_L
    W = N_HEADS * DH
    TAIL = 2 * SUBLANES
    nb = main_ref.shape[0]
    ubufs, qkbufs, csts, msts = (scratch[i * nb:(i + 1) * nb] for i in range(4))
    dot = functools.partial(jnp.dot, preferred_element_type=F32)

    @pl.when(pl.program_id(1) == 0)
    def _():
        for bb in range(nb):
            ubufs[bb][0:TAIL, :] = jnp.zeros((TAIL, 2 * W), BF16)
            csts[bb][...] = jnp.zeros(csts[bb].shape, F32)
            msts[bb][...] = jnp.zeros(msts[bb].shape, F32)

    row = lax.broadcasted_iota(I32, (L, L), 0)
    col = lax.broadcasted_iota(I32, (L, L), 1)
    tri = col <= row
    tri_bf = jnp.where(tri, 1.0, 0.0).astype(BF16)
    lane = lax.broadcasted_iota(I32, (L, LANES), 1)
    sh_r = lax.broadcasted_iota(I32, ((CONV_W - 1) * L, L + TAIL), 0)
    sh_c = lax.broadcasted_iota(I32, ((CONV_W - 1) * L, L + TAIL), 1)
    shift = jnp.where(sh_c == (sh_r % L) + TAIL - (CONV_W - 1) + sh_r // L, 1.0, 0.0).astype(BF16)
    rp_r = lax.broadcasted_iota(I32, (LANES, 2 * N_HEADS * LANES), 0)
    rp_c = lax.broadcasted_iota(I32, (LANES, 2 * N_HEADS * LANES), 1)
    rep = jnp.where(rp_r == rp_c // LANES, 1.0, 0.0).astype(BF16)

    gates = []
    for bb in range(nb):
        ubuf = ubufs[bb]
        u = main_ref[bb, :, 0:2 * W]
        ubuf[TAIL:TAIL + L, :] = u
        shifted = dot(shift, ubuf[...])
        ubuf[0:TAIL, :] = ubuf[L:L + TAIL, :]
        acc = cb_ref[...] + u.astype(F32) * cw_ref[CONV_W - 1:CONV_W, :]
        for i in range(CONV_W - 1):
            acc = acc + shifted[i * L:(i + 1) * L, :] * cw_ref[i:i + 1, :]
        qk = acc * _sigmoid(acc)
        qkbufs[bb][:, 0:W] = qk[:, 0:W]
        qkbufs[bb][:, W:2 * W] = qk[:, W:2 * W] * (DH ** -0.5)

        gpre = gate_ref[bb] + bg_ref[...]
        xg = jnp.where(lane < N_HEADS, gpre, _log_sigmoid(gpre))
        gm = jnp.where(lane < N_HEADS, xg, _cumsum_rows(tri_bf, xg))
        hi, mid, lo = _split3(gm)
        gates.append((dot(hi, rep) + dot(mid, rep) + dot(lo, rep), gm.T))

    chains = [_mlstm_head(h, main_ref.at[bb], hn_ref, y_ref.at[bb], qkbufs[bb], csts[bb], msts[bb],
                          gates[bb], tri)
              for h in range(N_HEADS) for bb in range(nb)]
    for _ in range(MLSTM_STAGES):
        for c in chains:
            next(c)


MLSTM_STAGES = 3


def _mlstm_head(h, main_ref, hn_ref, y_ref, qkbuf, cst, mst, gates, tri):
    L = MLSTM_L
    W = N_HEADS * DH
    dot = functools.partial(jnp.dot, preferred_element_type=F32)
    g_rep, g_t = gates
    hs = slice(h * DH, (h + 1) * DH)

    q = qkbuf[:, hs].astype(BF16)
    k = qkbuf[:, W + h * DH:W + (h + 1) * DH]
    v = main_ref[:, 2 * W + h * DH:2 * W + (h + 1) * DH]
    vext = jnp.concatenate([v, jnp.ones((L, DH), BF16)], axis=1)
    cext = cst[h]
    s_raw = lax.dot_general(q, k.astype(BF16), (((1,), (1,)), ((), ())), preferred_element_type=F32)
    qc = dot(q, cext.astype(BF16))
    yield

    ig_rep = g_rep[:, h * LANES:(h + 1) * LANES]
    b_rep = g_rep[:, (N_HEADS + h) * LANES:(N_HEADS + h + 1) * LANES]
    ig_row = g_t[h:h + 1, :]
    b_row = g_t[N_HEADS + h:N_HEADS + h + 1, :]
    m_prev = mst[h]
    logd = jnp.where(tri, b_rep - b_row + ig_row, NEG_INF)
    m_inter = b_rep + m_prev
    m_j = jnp.maximum(m_inter, jnp.max(logd, axis=-1, keepdims=True))
    s = s_raw * jnp.exp(logd - m_j)
    a = jnp.exp(m_inter - m_j)
    sv = dot(s.astype(BF16), vext)
    g = b_rep[L - 1:L, :]
    m_new = jnp.maximum(g + m_prev, jnp.max(g - b_row + ig_row, axis=-1, keepdims=True))
    kw = (k * jnp.exp(g - b_rep + ig_rep - m_new)).astype(BF16)
    decay = jnp.exp(g + m_prev - m_new)
    upd = lax.dot_general(kw, vext, (((0,), (0,)), ((), ())), preferred_element_type=F32)
    yield

    num = sv[:, 0:DH] + a * qc[:, 0:DH]
    den = sv[:, DH:2 * DH] + a * qc[:, DH:2 * DH]
    hh = num / jnp.maximum(jnp.abs(den), jnp.exp(-m_j))
    cst[h] = jnp.concatenate([decay, decay], axis=1) * cext + upd
    mst[h] = m_new
    hn = hh * lax.rsqrt(jnp.mean(hh * hh, axis=-1, keepdims=True) + EPS)
    o_gate = _sigmoid(main_ref[:, 3 * W + h * DH:3 * W + (h + 1) * DH].astype(F32))
    y_ref[:, hs] = (hn * hn_ref[:, hs] * o_gate).astype(BF16)
    yield


def _mlstm(main, gates, conv_w, conv_b, b_gate, head_norm, B, S, nb):
    L = MLSTM_L
    W = N_HEADS * DH
    tok = lambda b, c: (b, c, 0)
    const = lambda b, c: (0, 0)
    return pl.pallas_call(
        _mlstm_body,
        grid=(B // nb, S // L),
        in_specs=[pl.BlockSpec((nb, L, 4 * W), tok), pl.BlockSpec((nb, L, LANES), tok),
                  pl.BlockSpec((CONV_W, 2 * W), const), pl.BlockSpec((1, 2 * W), const),
                  pl.BlockSpec((1, LANES), const), pl.BlockSpec((1, W), const)],
        out_specs=pl.BlockSpec((nb, L, W), tok),
        out_shape=jax.ShapeDtypeStruct((B, S, W), BF16),
        scratch_shapes=([pltpu.VMEM((L + 2 * SUBLANES, 2 * W), BF16)] * nb
                        + [pltpu.VMEM((L, 2 * W), F32)] * nb
                        + [pltpu.VMEM((N_HEADS, DH, 2 * DH), F32)] * nb
                        + [pltpu.VMEM((N_HEADS, 1, LANES), F32)] * nb),
        compiler_params=_cparams(("arbitrary", "arbitrary")),
        name="mlstm",
    )(main.reshape(B, S, 4 * W), gates, conv_w, conv_b, b_gate,
      head_norm).reshape(B * S, W)


GLA_STAGES = 4


def _gla_consts():
    L, R = GLA_L, GLA_R
    row = lax.broadcasted_iota(I32, (L, L), 0)
    col = lax.broadcasted_iota(I32, (L, L), 1)
    return dict(
        tri_bf=jnp.where(col <= row, 1.0, 0.0).astype(BF16),
        lane=lax.broadcasted_iota(I32, (L, LANES), 1),
        lane_r=lax.broadcasted_iota(I32, (R, LANES), 1),
        row_r=lax.broadcasted_iota(I32, (R, LANES), 0),
        wsel=jnp.where(lax.broadcasted_iota(I32, (LANES, 2 * LANES), 0) // GLA_DK
                       == lax.broadcasted_iota(I32, (LANES, 2 * LANES), 1) // LANES,
                       1.0, 0.0).astype(BF16))


def _gla_pair(p, la, rows, c, gla_ref, hn_ref, y_ref, sst):
    L, R = GLA_L, GLA_R
    WK = N_HEADS * GLA_DK
    WV = N_HEADS * DH
    dot = functools.partial(jnp.dot, preferred_element_type=F32)
    nt = (((1,), (1,)), ((), ()))
    tn = (((0,), (0,)), ((), ()))
    lane, lane_r, row_r = c['lane'], c['lane_r'], c['row_r']

    bc = _cumsum_rows(c['tri_bf'], la[:, p * LANES:(p + 1) * LANES])
    yield

    q2 = gla_ref[rows, p * LANES:(p + 1) * LANES].astype(F32) * (GLA_DK ** -0.5)
    k2 = gla_ref[rows, WK + p * LANES:WK + (p + 1) * LANES].astype(F32)
    g_last = bc[L - 1:L, :]
    q_in = q2 * jnp.exp(bc)
    kd = k2 * jnp.exp(g_last - bc)
    decay = jnp.exp(g_last)

    k_off = [None]
    q_off = [None]
    for j in range(1, L // R):
        rj = bc[j * R:j * R + 1, :]
        part = (k2[0:j * R, :] * jnp.exp(rj - bc[0:j * R, :])).astype(BF16)
        k_off.append(jnp.concatenate([part, jnp.zeros((L - j * R, LANES), BF16)], axis=0))
        q_off.append(q2[j * R:(j + 1) * R, :] * jnp.exp(bc[j * R:(j + 1) * R, :] - rj))

    half = SUBLANES
    row_h = row_r[0:half, :]
    zeros_h = jnp.zeros((half, LANES), F32)
    diag = []
    for j in range(L // R):
        qb = q2[j * R:(j + 1) * R, :]
        bq = bc[j * R:(j + 1) * R, :]
        terms = []
        for s in range(R):
            krow = k2[j * R + s:j * R + s + 1, :]
            brow = bc[j * R + s:j * R + s + 1, :]
            if s < half:
                d_top = jnp.where(row_h >= s, bq[0:half, :] - brow, NEG_INF)
                top = qb[0:half, :] * krow * jnp.exp(d_top)
                bot = qb[half:R, :] * krow * jnp.exp(bq[half:R, :] - brow)
            else:
                d_bot = jnp.where(row_h >= s - half, bq[half:R, :] - brow, NEG_INF)
                top = zeros_h
                bot = qb[half:R, :] * krow * jnp.exp(d_bot)
            terms.append(jnp.concatenate([top, bot], axis=0).astype(BF16))
        diag.append(dot(jnp.concatenate(terms, axis=0), c['wsel']))

    heads = []
    for hh in range(2):
        h = 2 * p + hh
        mh = (lane // GLA_DK) == hh
        mq = (lane_r // GLA_DK) == hh
        v = gla_ref[rows, 2 * WK + h * DH:2 * WK + (h + 1) * DH]
        st = sst[h]
        o_inter = lax.dot_general(jnp.where(mh, q_in, 0.0).astype(BF16), st.astype(BF16), nt,
                                  preferred_element_type=F32)
        upd = lax.dot_general(v, jnp.where(mh, kd, 0.0).astype(BF16), tn, preferred_element_type=F32)
        offs = [None] + [lax.dot_general(jnp.where(mq, q_off[j], 0.0).astype(BF16), k_off[j], nt,
                                         preferred_element_type=F32) for j in range(1, L // R)]
        heads.append((h, v, st, o_inter, upd, offs))
    yield

    outs = []
    for hh, (h, v, st, o_inter, upd, offs) in enumerate(heads):
        sst[h] = st * decay + upd
        a_rows = []
        for j in range(L // R):
            res = diag[j][:, hh * LANES:(hh + 1) * LANES]
            blk = jnp.zeros((R, LANES), F32)
            for s in range(R):
                blk = jnp.where(lane_r == j * R + s, res[s * R:(s + 1) * R, :], blk)
            blk = blk[:, 0:L]
            if j > 0:
                blk = blk + offs[j]
            a_rows.append(blk)
        a_mat = jnp.concatenate(a_rows, axis=0).astype(BF16)
        outs.append((h, o_inter, dot(a_mat, v)))
    yield

    for h, o_inter, o_intra in outs:
        o = o_inter + o_intra
        on = o * lax.rsqrt(jnp.mean(o * o, axis=-1, keepdims=True) + EPS)
        z = gla_ref[rows, 2 * WK + WV + h * DH:2 * WK + WV + (h + 1) * DH].astype(F32)
        y_ref[rows, h * DH:(h + 1) * DH] = (
            on * hn_ref[:, h * DH:(h + 1) * DH] * (z * _sigmoid(z))).astype(BF16)
    yield


def _gla_body(gla_ref, gate_ref, wgg_ref, bgg_ref, hn_ref, y_ref, *ssts, n_chunks):
    nb = gla_ref.shape[0]

    @pl.when(pl.program_id(1) == 0)
    def _():
        for bb in range(nb):
            ssts[bb][...] = jnp.zeros(ssts[bb].shape, F32)

    consts = _gla_consts()

    def step(i, carry):
        rows = pl.ds(pl.multiple_of(i * GLA_L, GLA_L), GLA_L)
        chains = []
        for bb in range(nb):
            la = _log_sigmoid(jnp.dot(gate_ref[bb, rows, :].astype(BF16), wgg_ref[...],
                                      preferred_element_type=F32) + bgg_ref[...]) / GLA_GATE_NORM
            chains += [_gla_pair(p, la, rows, consts, gla_ref.at[bb], hn_ref, y_ref.at[bb], ssts[bb])
                       for p in range(N_HEADS // 2)]
        for _ in range(GLA_STAGES):
            for ch in chains:
                next(ch)
        return carry

    lax.fori_loop(0, n_chunks, step, 0)


def _gla(gla, gates, wgg, bgg, head_norm, B, S, lt, nb):
    WK = N_HEADS * GLA_DK
    WV = N_HEADS * DH
    tok = lambda b, c: (b, c, 0)
    const = lambda b, c: (0, 0)
    return pl.pallas_call(
        functools.partial(_gla_body, n_chunks=lt // GLA_L),
        grid=(B // nb, S // lt),
        in_specs=[pl.BlockSpec((nb, lt, 2 * WK + 2 * WV), tok), pl.BlockSpec((nb, lt, LANES), tok),
                  pl.BlockSpec((LANES, WK), const), pl.BlockSpec((1, WK), const),
                  pl.BlockSpec((1, WV), const)],
        out_specs=pl.BlockSpec((nb, lt, WV), tok),
        out_shape=jax.ShapeDtypeStruct((B, S, WV), BF16),
        scratch_shapes=[pltpu.VMEM((N_HEADS, DH, LANES), F32)] * nb,
        compiler_params=_cparams(("arbitrary", "arbitrary")),
        name="gla",
    )(gla.reshape(B, S, 2 * WK + 2 * WV), gates, wgg, bgg,
      head_norm).reshape(B * S, WV)


def _outproj_router_body(ym_ref, yg_ref, h_ref, wo_ref, g_ref, wr_ref, br_ref,
                         hout_ref, xn_ref, route_ref, cnt_ref, carry):
    tm = h_ref.shape[0]
    W = ym_ref.shape[1]
    dot = functools.partial(jnp.dot, preferred_element_type=F32)

    @pl.when(pl.program_id(0) == 0)
    def _():
        carry[...] = jnp.zeros(carry.shape, F32)

    hnew = h_ref[...] + dot(ym_ref[...], wo_ref[0:W, :]) + dot(yg_ref[...], wo_ref[W:2 * W, :])
    hout_ref[...] = hnew
    xn = hnew * lax.rsqrt(jnp.mean(hnew * hnew, axis=-1, keepdims=True) + EPS) * g_ref[...]
    xn_ref[...] = xn

    xh = xn.astype(BF16)
    xm = (xn - xh.astype(F32)).astype(BF16)
    l2 = dot(xh, wr_ref[...])
    logits = l2[:, 0:LANES] + (l2[:, LANES:2 * LANES] + dot(xm, wr_ref[:, 0:LANES])) + br_ref[...]

    lane = lax.broadcasted_iota(I32, (tm, LANES), 1)
    lane_f = lane.astype(F32)
    big = float(LANES)
    gl = jnp.where(lane < N_GROUPS, logits, NEG_INF)
    gmax = jnp.max(gl, axis=-1, keepdims=True)
    gsel = jnp.min(jnp.where(gl == gmax, lane_f, big), axis=-1, keepdims=True)
    g_gate = 1.0 / jnp.sum(jnp.where(lane < N_GROUPS, jnp.exp(logits - gmax), 0.0),
                           axis=-1, keepdims=True)
    in_grp = ((lane >= EXP_LANE0) & (lane < EXP_LANE0 + N_EXPERTS)
              & (((lane - EXP_LANE0) // EPG).astype(F32) == gsel))
    el = jnp.where(in_grp, logits, NEG_INF)
    v1 = jnp.max(el, axis=-1, keepdims=True)
    i1 = jnp.min(jnp.where(el == v1, lane_f, big), axis=-1, keepdims=True)
    el2 = jnp.where(lane_f == i1, NEG_INF, el)
    v2 = jnp.max(el2, axis=-1, keepdims=True)
    i2 = jnp.min(jnp.where(el2 == v2, lane_f, big), axis=-1, keepdims=True)
    e2 = jnp.exp(v2 - v1)
    w1 = g_gate / (1.0 + e2)
    w2 = g_gate * e2 / (1.0 + e2)

    oh1 = lane_f == i1
    oh2 = lane_f == i2
    oh = jnp.where(oh1, 1.0, 0.0) + jnp.where(oh2, 1.0, 0.0)
    r_i = lax.broadcasted_iota(I32, (tm, tm), 0)
    c_i = lax.broadcasted_iota(I32, (tm, tm), 1)
    strict = jnp.where(c_i < r_i, 1.0, 0.0).astype(BF16)
    before = dot(strict, oh.astype(BF16)) + carry[0:1, :]
    rank1 = jnp.sum(jnp.where(oh1, before, 0.0), axis=-1, keepdims=True)
    rank2 = jnp.sum(jnp.where(oh2, before, 0.0), axis=-1, keepdims=True)
    total = carry[0:1, :] + jnp.sum(oh, axis=0, keepdims=True)
    carry[...] = jnp.broadcast_to(total, carry.shape)
    cnt_ref[...] = jnp.broadcast_to(total, cnt_ref.shape)

    rec = jnp.where(lane == 0, i1 - EXP_LANE0, 0.0)
    rec = jnp.where(lane == 1, i2 - EXP_LANE0, rec)
    rec = jnp.where(lane == 2, w1, rec)
    rec = jnp.where(lane == 3, w2, rec)
    rec = jnp.where(lane == 4, rank1, rec)
    rec = jnp.where(lane == 5, rank2, rec)
    route_ref[...] = rec


def _outproj_router(ym, yg, h2d, wo, gain, wr3, br, tm):
    T, D = h2d.shape
    W = ym.shape[1]
    row = lambda i: (i, 0)
    const = lambda i: (0, 0)
    return pl.pallas_call(
        _outproj_router_body,
        grid=(T // tm,),
        in_specs=[pl.BlockSpec((tm, W), row), pl.BlockSpec((tm, W), row), pl.BlockSpec((tm, D), row),
                  pl.BlockSpec((2 * W, D), const), pl.BlockSpec((1, D), const),
                  pl.BlockSpec((D, 2 * LANES), const), pl.BlockSpec((1, LANES), const)],
        out_specs=[pl.BlockSpec((tm, D), row), pl.BlockSpec((tm, D), row),
                   pl.BlockSpec((tm, LANES), row), pl.BlockSpec((SUBLANES, LANES), const)],
        out_shape=[jax.ShapeDtypeStruct((T, D), F32), jax.ShapeDtypeStruct((T, D), F32),
                   jax.ShapeDtypeStruct((T, LANES), F32),
                   jax.ShapeDtypeStruct((SUBLANES, LANES), F32)],
        scratch_shapes=[pltpu.VMEM((SUBLANES, LANES), F32)],
        compiler_params=_cparams(("arbitrary",)),
        name="outproj_router",
    )(ym, yg, h2d, wo, gain, wr3, br)


def _dispatch_body(pos_ref, pad_ref, x_hbm, xs_hbm, xbuf, zbuf, load_sem, scat_sem, pad_sem,
                   *, td, n_tiles):
    i = pl.program_id(0)
    ns = xbuf.shape[0]
    slot = lax.rem(i, ns)

    def load(tile, s):
        rows = pl.ds(pl.multiple_of(tile * td, td), td)
        return pltpu.make_async_copy(x_hbm.at[rows], xbuf.at[s], load_sem.at[s])

    def wait_scatter(s):
        for _ in range(2):
            pltpu.make_async_copy(xbuf.at[s], xs_hbm.at[pl.ds(0, td)], scat_sem.at[s]).wait()

    def for_each_pad_copy(fn):
        def per_expert(e, carry):
            first = pad_ref[0, 0, e]
            count = pad_ref[0, 0, N_EXPERTS + e]
            lead = jnp.minimum(count, lax.rem(SUBLANES - lax.rem(first, SUBLANES), SUBLANES))

            def per_row(r, c):
                fn(pltpu.make_async_copy(zbuf.at[pl.ds(0, 1)], xs_hbm.at[pl.ds(first + r, 1)], pad_sem))
                return c

            def per_group(g, c):
                rows = pl.ds(pl.multiple_of(first + lead + g * SUBLANES, SUBLANES), SUBLANES)
                fn(pltpu.make_async_copy(zbuf.at[pl.ds(0, SUBLANES)], xs_hbm.at[rows], pad_sem))
                return c

            carry = lax.fori_loop(0, lead, per_row, carry)
            return lax.fori_loop(0, (count - lead) // SUBLANES, per_group, carry)

        lax.fori_loop(0, N_EXPERTS, per_expert, 0)

        def per_block(bk, carry):
            rows = pl.ds(pl.multiple_of(bk * MOE_BLOCK, MOE_BLOCK), MOE_BLOCK)
            fn(pltpu.make_async_copy(zbuf, xs_hbm.at[rows], pad_sem))
            return carry

        lax.fori_loop(pad_ref[0, 0, 2 * N_EXPERTS], xs_hbm.shape[0] // MOE_BLOCK, per_block, 0)

    @pl.when(i == 0)
    def _():
        load(0, 0).start()
        zbuf[...] = jnp.zeros(zbuf.shape, F32)
        for_each_pad_copy(lambda cp: cp.start())

    @pl.when(i + 1 < n_tiles)
    def _():
        load(i + 1, lax.rem(i + 1, ns)).start()

    load(i, slot).wait()

    for j in range(td):
        for k in range(2):
            pltpu.make_async_copy(xbuf.at[slot, pl.ds(j, 1)],
                                  xs_hbm.at[pl.ds(pos_ref[0, 0, 2 * j + k], 1)],
                                  scat_sem.at[slot]).start(priority=k)

    @pl.when(i > 0)
    def _():
        wait_scatter(lax.rem(i + ns - 1, ns))

    @pl.when(i == n_tiles - 1)
    def _():
        wait_scatter(slot)
        for_each_pad_copy(lambda cp: cp.wait())


def _dispatch(pos, pads, xn, n_rows, td):
    T, D = xn.shape
    n = T // td
    return pl.pallas_call(
        functools.partial(_dispatch_body, td=td, n_tiles=n),
        grid=(n,),
        in_specs=[pl.BlockSpec((1, 1, 2 * td), lambda i: (i, 0, 0), memory_space=pltpu.SMEM),
                  pl.BlockSpec((1, 1, 2 * N_EXPERTS + 1), lambda i: (0, 0, 0),
                               memory_space=pltpu.SMEM),
                  pl.BlockSpec(memory_space=pl.ANY)],
        out_specs=pl.BlockSpec(memory_space=pl.ANY),
        out_shape=jax.ShapeDtypeStruct((n_rows, D), F32),
        scratch_shapes=[pltpu.VMEM((3, td, D), F32), pltpu.VMEM((MOE_BLOCK, D), F32),
                        pltpu.SemaphoreType.DMA((3,)), pltpu.SemaphoreType.DMA((3,)),
                        pltpu.SemaphoreType.DMA(())],
        compiler_params=_cparams(("arbitrary",)),
        name="dispatch",
    )(pos.reshape(n, 1, 2 * td), pads.reshape(1, 1, 2 * N_EXPERTS + 1), xn)


def _ffn_body(blk_exp_ref, nused_ref, xs_ref, wgu_ref, wd_ref, ys_ref, wgu_bf, wd_bf):
    b = pl.program_id(0)
    de = wd_ref.shape[2]
    used = b < nused_ref[0]
    new_expert = (b == 0) | (blk_exp_ref[b] != blk_exp_ref[jnp.maximum(b - 1, 0)])

    @pl.when(used & new_expert)
    def _():
        wgu_bf[...] = wgu_ref[0, 0].astype(BF16)
        wd_bf[...] = wd_ref[0, 0].astype(BF16)

    @pl.when(used)
    def _():
        gu = jnp.dot(xs_ref[...].astype(BF16), wgu_bf[...], preferred_element_type=F32)
        gate, up = gu[:, 0:de], gu[:, de:2 * de]
        act = (gate * _sigmoid(gate) * up).astype(BF16)
        ys_ref[...] = jnp.dot(act, wd_bf[...], preferred_element_type=F32)

    @pl.when(jnp.logical_not(used))
    def _():
        ys_ref[...] = jnp.zeros(ys_ref.shape, F32)


def _ffn(blk_exp, nused, xs, wgu, wd, layer):
    NR, D = xs.shape
    nb = NR // MOE_BLOCK
    de = wd.shape[2]
    grid_spec = pltpu.PrefetchScalarGridSpec(
        num_scalar_prefetch=2,
        grid=(nb,),
        in_specs=[pl.BlockSpec((MOE_BLOCK, D), lambda b, be, nu: (jnp.minimum(b, nu[0] - 1), 0)),
                  pl.BlockSpec((1, 1, D, 2 * de), lambda b, be, nu: (layer, be[b], 0, 0)),
                  pl.BlockSpec((1, 1, de, D), lambda b, be, nu: (layer, be[b], 0, 0))],
        out_specs=pl.BlockSpec((MOE_BLOCK, D), lambda b, be, nu: (b, 0)),
        scratch_shapes=[pltpu.VMEM((D, 2 * de), BF16), pltpu.VMEM((de, D), BF16)],
    )
    return pl.pallas_call(
        _ffn_body,
        grid_spec=grid_spec,
        out_shape=jax.ShapeDtypeStruct((NR, D), F32),
        compiler_params=_cparams(("arbitrary",)),
        name="expert_ffn",
    )(blk_exp, nused, xs, wgu, wd)


def _combine_body(pos_ref, posn_ref, ys_hbm, h_ref, route_ref, gfin_ref, out_ref, gbuf, sem,
                  *, tc, n_tiles, final):
    i = pl.program_id(0)
    slot = lax.rem(i, 2)

    def issue_tile(p_ref, s):
        def issue(jb, carry):
            for u in range(ROW_UNROLL):
                j = jb * ROW_UNROLL + u
                for k in range(2):
                    pltpu.make_async_copy(ys_hbm.at[pl.ds(p_ref[0, 0, 2 * j + k], 1)],
                                          gbuf.at[s, k, pl.ds(j, 1)], sem.at[s]).start(priority=k)
            return carry

        lax.fori_loop(0, tc // ROW_UNROLL, issue, 0)

    @pl.when(i == 0)
    def _():
        issue_tile(pos_ref, 0)

    @pl.when(i + 1 < n_tiles)
    def _():
        for j in range(tc):
            for k in range(2):
                pltpu.make_async_copy(ys_hbm.at[pl.ds(posn_ref[0, 0, 2 * j + k], 1)],
                                      gbuf.at[1 - slot, k, pl.ds(j, 1)],
                                      sem.at[1 - slot]).start(priority=k)

    for k in range(2):
        pltpu.make_async_copy(ys_hbm.at[pl.ds(0, tc)], gbuf.at[slot, k], sem.at[slot]).wait()

    w1 = route_ref[:, 2:3]
    w2 = route_ref[:, 3:4]
    out = h_ref[...] + (w1 * gbuf[slot, 0] + w2 * gbuf[slot, 1])
    if final:
        out = out * lax.rsqrt(jnp.mean(out * out, axis=-1, keepdims=True) + EPS) * gfin_ref[...]
    out_ref[...] = out


def _combine(pos, ys, h2d, route, gfin, tc, final):
    T, D = h2d.shape
    n = T // tc
    pos3 = pos.reshape(n, 1, 2 * tc)
    row = lambda i: (i, 0)
    smem = functools.partial(pl.BlockSpec, (1, 1, 2 * tc), memory_space=pltpu.SMEM)
    return pl.pallas_call(
        functools.partial(_combine_body, tc=tc, n_tiles=n, final=final),
        grid=(n,),
        in_specs=[smem(index_map=lambda i: (i, 0, 0)),
                  smem(index_map=lambda i: (jnp.minimum(i + 1, n - 1), 0, 0)),
                  pl.BlockSpec(memory_space=pl.ANY),
                  pl.BlockSpec((tc, D), row), pl.BlockSpec((tc, LANES), row),
                  pl.BlockSpec((1, D), lambda i: (0, 0))],
        out_specs=pl.BlockSpec((tc, D), row),
        out_shape=jax.ShapeDtypeStruct((T, D), F32),
        scratch_shapes=[pltpu.VMEM((2, 2, tc, D), F32), pltpu.SemaphoreType.DMA((2,))],
        compiler_params=_cparams(("arbitrary",)),
        name="combine",
    )(pos3, pos3, ys, h2d, route, gfin)


def _pick_tile(n, pref):
    t = min(pref, n)
    while n % t:
        t //= 2
    return t


def _prep_layer(w_in, b_mlstm_gate, w_gla_gate, b_gla_gate, w_out, w_group, b_group, w_expert,
                b_expert):
    W = N_HEADS * DH
    WK = N_HEADS * GLA_DK
    D = w_in.shape[0]
    o_gates = 4 * W
    o_gla = o_gates + 2 * N_HEADS
    o_r = o_gla + 2 * WK + 2 * W
    wm = w_in[:, 0:o_gates].astype(BF16)
    wg = w_in[:, o_gla:o_r].astype(BF16)
    ws = jnp.zeros((D, LANES), F32)
    ws = ws.at[:, 0:2 * N_HEADS].set(w_in[:, o_gates:o_gla])
    ws = ws.at[:, 2 * N_HEADS:2 * N_HEADS + GLA_RANK].set(w_in[:, o_r:o_r + GLA_RANK]).astype(BF16)
    bg = jnp.zeros((1, LANES), F32).at[0, 0:2 * N_HEADS].set(b_mlstm_gate)
    wgg = jnp.zeros((LANES, WK), F32).at[2 * N_HEADS:2 * N_HEADS + GLA_RANK, :].set(w_gla_gate)
    wr = jnp.zeros((D, LANES), F32)
    wr = wr.at[:, 0:N_GROUPS].set(w_group).at[:, EXP_LANE0:EXP_LANE0 + N_EXPERTS].set(w_expert)
    br = jnp.zeros((1, LANES), F32)
    br = br.at[0, 0:N_GROUPS].set(b_group).at[0, EXP_LANE0:EXP_LANE0 + N_EXPERTS].set(b_expert)
    return dict(wm=wm, wg=wg, ws=ws, bg=bg, wgg=wgg.astype(BF16), bgg=b_gla_gate[None, :],
                wo=w_out.astype(BF16), wr3=jnp.concatenate(_split3(wr)[0:2], axis=1), br=br)


def _routing_tables(route, counts_row, T):
    eid = route[:, 0:2].astype(I32)
    rank = route[:, 4:6].astype(I32)
    counts = counts_row[EXP_LANE0:EXP_LANE0 + N_EXPERTS].astype(I32)
    pcounts = (counts + MOE_BLOCK - 1) // MOE_BLOCK * MOE_BLOCK
    pend = jnp.cumsum(pcounts)
    pstart = pend - pcounts
    onehot = eid[:, :, None] == jnp.arange(N_EXPERTS, dtype=I32)[None, None, :]
    pos = jnp.sum(jnp.where(onehot, pstart[None, None, :], 0), axis=-1) + rank
    n_blocks = (T * 2 + MOE_BLOCK - 1) // MOE_BLOCK + N_EXPERTS
    blk_start = jnp.arange(n_blocks, dtype=I32) * MOE_BLOCK
    blk_exp = jnp.minimum(jnp.sum((pend[None, :] <= blk_start[:, None]).astype(I32), axis=1),
                          N_EXPERTS - 1)
    nused = (pend[-1] // MOE_BLOCK).astype(I32).reshape(1)
    pads = jnp.concatenate([pstart + counts, pcounts - counts, nused])
    return pos, pads, blk_exp, nused, n_blocks


def kernel(x, norm_mix, w_in, conv_w, conv_b, b_mlstm_gate, w_gla_gate, b_gla_gate, head_norm, w_out, norm_ffn, w_group, b_group, w_expert, b_expert, w_gu, w_down, norm_final):
    B, S, D = x.shape
    T = B * S
    depth = w_in.shape[0]
    W = N_HEADS * DH
    tm = _pick_tile(S, 512)
    td = _pick_tile(T, 512)
    tc = _pick_tile(T, 512)
    lt = _pick_tile(S, 256)
    nb = _pick_tile(B, 4)
    nb_gla = _pick_tile(B, 8)
    h = x.reshape(T, D)
    pending = None
    for l in range(depth):
        p = _prep_layer(w_in[l], b_mlstm_gate[l], w_gla_gate[l], b_gla_gate[l], w_out[l],
                        w_group[l], b_group[l], w_expert[l], b_expert[l])
        if pending is None:
            main, gla, gates = _inproj(h, norm_mix[l][None, :], p['wm'], p['wg'], p['ws'], tm, S)
        else:
            h, main, gla, gates = _inproj_combine(*pending, norm_mix[l][None, :], p['wm'], p['wg'],
                                                  p['ws'], tm, S)
        ym = _mlstm(main, gates, conv_w[l], conv_b[l][None, :], p['bg'], head_norm[l][None, 0:W], B, S,
                    nb)
        yg = _gla(gla, gates, p['wgg'], p['bgg'], head_norm[l][None, W:2 * W], B, S, lt, nb_gla)
        h, xn, route, cnt = _outproj_router(ym, yg, h, p['wo'], norm_ffn[l][None, :], p['wr3'],
                                            p['br'], tm)
        pos, pads, blk_exp, nused, n_blocks = _routing_tables(route, cnt[0], T)
        xs = _dispatch(pos, pads, xn, n_blocks * MOE_BLOCK, td)
        ys = _ffn(blk_exp, nused, xs, w_gu, w_down, l)
        pending = (pos, ys, h, route)
    pos, ys, h, route = pending
    h = _combine(pos, ys, h, route, norm_final[None, :], tc, final=True)
    return h.reshape(B, S, D)
```

```python
import functools

import jax
import jax.numpy as jnp
from jax import lax
from jax.experimental import pallas as pl
from jax.experimental.pallas import tpu as pltpu

F32 = jnp.float32
BF16 = jnp.bfloat16
I32 = jnp.int32

EPS = 1e-6
LANES = 128
SUBLANES = 8
VMEM_LIMIT = 56 * 1024 * 1024

N_HEADS = 4
DH = 128
GLA_DK = 64
GLA_RANK = 16
GLA_GATE_NORM = 16.0
CONV_W = 4
MLSTM_L = 128
GLA_L = 64
GLA_R = 16
N_GROUPS = 4
EPG = 8
N_EXPERTS = N_GROUPS * EPG
MOE_BLOCK = 512
EXP_LANE0 = N_GROUPS

ROW_UNROLL = 8

NEG_INF = float("-inf")


def _cparams(sem):
    return pltpu.CompilerParams(dimension_semantics=sem, vmem_limit_bytes=VMEM_LIMIT)


def _split3(x):
    hi = x.astype(BF16)
    r1 = x - hi.astype(F32)
    mid = r1.astype(BF16)
    lo = (r1 - mid.astype(F32)).astype(BF16)
    return hi, mid, lo


def _cumsum_rows(tri_bf, x):
    hi, mid, lo = _split3(x)
    dot = functools.partial(jnp.dot, preferred_element_type=F32)
    return dot(tri_bf, hi) + dot(tri_bf, mid) + dot(tri_bf, lo)


def _sigmoid(x):
    return 0.5 * jnp.tanh(0.5 * x) + 0.5


def _log_sigmoid(x):
    return jnp.minimum(x, 0.0) - jnp.log(1.0 + jnp.exp(-jnp.abs(x)))


def _inproj_body(x_ref, g_ref, wm_ref, wg_ref, ws_ref, om_ref, og_ref, os_ref):
    x = x_ref[...]
    xn = x * lax.rsqrt(jnp.mean(x * x, axis=-1, keepdims=True) + EPS) * g_ref[...]
    xb = xn.astype(BF16)
    om_ref[...] = jnp.dot(xb, wm_ref[...], preferred_element_type=F32).astype(BF16)
    og_ref[...] = jnp.dot(xb, wg_ref[...], preferred_element_type=F32).astype(BF16)
    os_ref[...] = jnp.dot(xb, ws_ref[...], preferred_element_type=F32)


def _inproj(h2d, gain, wm, wg, ws, tm):
    T, D = h2d.shape
    nm, ng, ns = wm.shape[1], wg.shape[1], ws.shape[1]
    const = lambda i: (0, 0)
    row = lambda i: (i, 0)
    return pl.pallas_call(
        _inproj_body,
        grid=(T // tm,),
        in_specs=[pl.BlockSpec((tm, D), row), pl.BlockSpec((1, D), const),
                  pl.BlockSpec((D, nm), const), pl.BlockSpec((D, ng), const),
                  pl.BlockSpec((D, ns), const)],
        out_specs=[pl.BlockSpec((tm, nm), row), pl.BlockSpec((tm, ng), row),
                   pl.BlockSpec((tm, ns), row)],
        out_shape=[jax.ShapeDtypeStruct((T, nm), BF16), jax.ShapeDtypeStruct((T, ng), BF16),
                   jax.ShapeDtypeStruct((T, ns), F32)],
        compiler_params=_cparams(("parallel",)),
        name="inproj",
    )(h2d, gain, wm, wg, ws)


def _inproj_combine_body(pos_ref, pos1_ref, pos2_ref, ys_hbm, h_ref, route_ref, g_ref, wm_ref, wg_ref,
                         ws_ref, hout_ref, om_ref, og_ref, os_ref, gbuf, sem, *, n_tiles):
    tm = h_ref.shape[0]
    ns = gbuf.shape[0]
    i = pl.program_id(0)
    slot = lax.rem(i, ns)

    def row_copy(p_ref, s, j, k):
        return pltpu.make_async_copy(ys_hbm.at[pl.ds(p_ref[0, 0, 2 * j + k], 1)],
                                     gbuf.at[s, k, pl.ds(j, 1)], sem.at[s])

    def wait_tile(s):
        for k in range(2):
            pltpu.make_async_copy(ys_hbm.at[pl.ds(0, tm)], gbuf.at[s, k], sem.at[s]).wait()

    @pl.when(i == 0)
    def _():
        def issue(jb, carry):
            for u in range(ROW_UNROLL):
                for k in range(2):
                    row_copy(pos_ref, 0, jb * ROW_UNROLL + u, k).start(priority=k)
                    row_copy(pos1_ref, 1, jb * ROW_UNROLL + u, k).start(priority=k)
            return carry

        lax.fori_loop(0, tm // ROW_UNROLL, issue, 0)

    wait_tile(slot)
    x = h_ref[...] + (route_ref[:, 2:3] * gbuf[slot, 0] + route_ref[:, 3:4] * gbuf[slot, 1])
    hout_ref[...] = x

    ahead = lax.rem(i + 2, ns)
    for j in range(tm):
        for k in range(2):
            row_copy(pos2_ref, ahead, j, k).start(priority=k)

    xn = x * lax.rsqrt(jnp.mean(x * x, axis=-1, keepdims=True) + EPS) * g_ref[...]
    xb = xn.astype(BF16)
    om_ref[...] = jnp.dot(xb, wm_ref[...], preferred_element_type=F32).astype(BF16)
    og_ref[...] = jnp.dot(xb, wg_ref[...], preferred_element_type=F32).astype(BF16)
    os_ref[...] = jnp.dot(xb, ws_ref[...], preferred_element_type=F32)

    @pl.when(i == n_tiles - 1)
    def _():
        wait_tile(lax.rem(i + 1, ns))
        wait_tile(ahead)


def _inproj_combine(pos, ys, h2d, route, gain, wm, wg, ws, tm):
    T, D = h2d.shape
    n = T // tm
    nm, ng, ns = wm.shape[1], wg.shape[1], ws.shape[1]
    const = lambda i: (0, 0)
    row = lambda i: (i, 0)
    pos3 = pos.reshape(n, 1, 2 * tm)
    smem = functools.partial(pl.BlockSpec, (1, 1, 2 * tm), memory_space=pltpu.SMEM)
    return pl.pallas_call(
        functools.partial(_inproj_combine_body, n_tiles=n),
        grid=(n,),
        in_specs=[smem(index_map=lambda i: (i, 0, 0)),
                  smem(index_map=lambda i: (jnp.minimum(i + 1, n - 1), 0, 0)),
                  smem(index_map=lambda i: (jnp.minimum(i + 2, n - 1), 0, 0)),
                  pl.BlockSpec(memory_space=pl.ANY),
                  pl.BlockSpec((tm, D), row), pl.BlockSpec((tm, LANES), row), pl.BlockSpec((1, D), const),
                  pl.BlockSpec((D, nm), const), pl.BlockSpec((D, ng), const),
                  pl.BlockSpec((D, ns), const)],
        out_specs=[pl.BlockSpec((tm, D), row), pl.BlockSpec((tm, nm), row), pl.BlockSpec((tm, ng), row),
                   pl.BlockSpec((tm, ns), row)],
        out_shape=[jax.ShapeDtypeStruct((T, D), F32), jax.ShapeDtypeStruct((T, nm), BF16),
                   jax.ShapeDtypeStruct((T, ng), BF16), jax.ShapeDtypeStruct((T, ns), F32)],
        scratch_shapes=[pltpu.VMEM((3, 2, tm, D), F32), pltpu.SemaphoreType.DMA((3,))],
        compiler_params=_cparams(("arbitrary",)),
        name="inproj_combine",
    )(pos3, pos3, pos3, ys, h2d, route, gain, wm, wg, ws)


def _mlstm_body(main_ref, gate_ref, cw_ref, cb_ref, bg_ref, hn_ref, y_ref, *scratch):
    L = MLSTM_L
    W = N_HEADS * DH
    TAIL = 2 * SUBLANES
    nb = main_ref.shape[0]
    ubufs, qkbufs, csts, msts = (scratch[i * nb:(i + 1) * nb] for i in range(4))
    dot = functools.partial(jnp.dot, preferred_element_type=F32)

    @pl.when(pl.program_id(1) == 0)
    def _():
        for bb in range(nb):
            ubufs[bb][0:TAIL, :] = jnp.zeros((TAIL, 2 * W), BF16)
            csts[bb][...] = jnp.zeros(csts[bb].shape, F32)
            msts[bb][...] = jnp.zeros(msts[bb].shape, F32)

    row = lax.broadcasted_iota(I32, (L, L), 0)
    col = lax.broadcasted_iota(I32, (L, L), 1)
    tri = col <= row
    tri_bf = jnp.where(tri, 1.0, 0.0).astype(BF16)
    lane = lax.broadcasted_iota(I32, (L, LANES), 1)
    sh_r = lax.broadcasted_iota(I32, ((CONV_W - 1) * L, L + TAIL), 0)
    sh_c = lax.broadcasted_iota(I32, ((CONV_W - 1) * L, L + TAIL), 1)
    shift = jnp.where(sh_c == (sh_r % L) + TAIL - (CONV_W - 1) + sh_r // L, 1.0, 0.0).astype(BF16)
    rp_r = lax.broadcasted_iota(I32, (LANES, 2 * N_HEADS * LANES), 0)
    rp_c = lax.broadcasted_iota(I32, (LANES, 2 * N_HEADS * LANES), 1)
    rep = jnp.where(rp_r == rp_c // LANES, 1.0, 0.0).astype(BF16)

    gates = []
    for bb in range(nb):
        ubuf = ubufs[bb]
        u = main_ref[bb, :, 0:2 * W]
        ubuf[TAIL:TAIL + L, :] = u
        shifted = dot(shift, ubuf[...])
        ubuf[0:TAIL, :] = ubuf[L:L + TAIL, :]
        acc = cb_ref[...] + u.astype(F32) * cw_ref[CONV_W - 1:CONV_W, :]
        for i in range(CONV_W - 1):
            acc = acc + shifted[i * L:(i + 1) * L, :] * cw_ref[i:i + 1, :]
        qk = acc * _sigmoid(acc)
        qkbufs[bb][:, 0:W] = qk[:, 0:W]
        qkbufs[bb][:, W:2 * W] = qk[:, W:2 * W] * (DH ** -0.5)

        gpre = gate_ref[bb] + bg_ref[...]
        xg = jnp.where(lane < N_HEADS, gpre, _log_sigmoid(gpre))
        gm = jnp.where(lane < N_HEADS, xg, _cumsum_rows(tri_bf, xg))
        hi, mid, lo = _split3(gm)
        gates.append((dot(hi, rep) + dot(mid, rep) + dot(lo, rep), gm.T))

    chains = [_mlstm_head(h, main_ref.at[bb], hn_ref, y_ref.at[bb], qkbufs[bb], csts[bb], msts[bb],
                          gates[bb], tri)
              for h in range(N_HEADS) for bb in range(nb)]
    for _ in range(MLSTM_STAGES):
        for c in chains:
            next(c)


MLSTM_STAGES = 3


def _mlstm_head(h, main_ref, hn_ref, y_ref, qkbuf, cst, mst, gates, tri):
    L = MLSTM_L
    W = N_HEADS * DH
    dot = functools.partial(jnp.dot, preferred_element_type=F32)
    g_rep, g_t = gates
    hs = slice(h * DH, (h + 1) * DH)

    q = qkbuf[:, hs].astype(BF16)
    k = qkbuf[:, W + h * DH:W + (h + 1) * DH]
    v = main_ref[:, 2 * W + h * DH:2 * W + (h + 1) * DH]
    vext = jnp.concatenate([v, jnp.ones((L, DH), BF16)], axis=1)
    cext = cst[h]
    s_raw = lax.dot_general(q, k.astype(BF16), (((1,), (1,)), ((), ())), preferred_element_type=F32)
    qc = dot(q, cext.astype(BF16))
    yield

    ig_rep = g_rep[:, h * LANES:(h + 1) * LANES]
    b_rep = g_rep[:, (N_HEADS + h) * LANES:(N_HEADS + h + 1) * LANES]
    ig_row = g_t[h:h + 1, :]
    b_row = g_t[N_HEADS + h:N_HEADS + h + 1, :]
    m_prev = mst[h]
    logd = jnp.where(tri, b_rep - b_row + ig_row, NEG_INF)
    m_inter = b_rep + m_prev
    m_j = jnp.maximum(m_inter, jnp.max(logd, axis=-1, keepdims=True))
    s = s_raw * jnp.exp(logd - m_j)
    a = jnp.exp(m_inter - m_j)
    sv = dot(s.astype(BF16), vext)
    g = b_rep[L - 1:L, :]
    m_new = jnp.maximum(g + m_prev, jnp.max(g - b_row + ig_row, axis=-1, keepdims=True))
    kw = (k * jnp.exp(g - b_rep + ig_rep - m_new)).astype(BF16)
    decay = jnp.exp(g + m_prev - m_new)
    upd = lax.dot_general(kw, vext, (((0,), (0,)), ((), ())), preferred_element_type=F32)
    yield

    num = sv[:, 0:DH] + a * qc[:, 0:DH]
    den = sv[:, DH:2 * DH] + a * qc[:, DH:2 * DH]
    hh = num / jnp.maximum(jnp.abs(den), jnp.exp(-m_j))
    cst[h] = jnp.concatenate([decay, decay], axis=1) * cext + upd
    mst[h] = m_new
    hn = hh * lax.rsqrt(jnp.mean(hh * hh, axis=-1, keepdims=True) + EPS)
    o_gate = _sigmoid(main_ref[:, 3 * W + h * DH:3 * W + (h + 1) * DH].astype(F32))
    y_ref[:, hs] = (hn * hn_ref[:, hs] * o_gate).astype(BF16)
    yield


def _mlstm(main, gates, conv_w, conv_b, b_gate, head_norm, B, S, nb):
    L = MLSTM_L
    W = N_HEADS * DH
    tok = lambda b, c: (b, c, 0)
    const = lambda b, c: (0, 0)
    return pl.pallas_call(
        _mlstm_body,
        grid=(B // nb, S // L),
        in_specs=[pl.BlockSpec((nb, L, 4 * W), tok), pl.BlockSpec((nb, L, LANES), tok),
                  pl.BlockSpec((CONV_W, 2 * W), const), pl.BlockSpec((1, 2 * W), const),
                  pl.BlockSpec((1, LANES), const), pl.BlockSpec((1, W), const)],
        out_specs=pl.BlockSpec((nb, L, W), tok),
        out_shape=jax.ShapeDtypeStruct((B, S, W), BF16),
        scratch_shapes=([pltpu.VMEM((L + 2 * SUBLANES, 2 * W), BF16)] * nb
                        + [pltpu.VMEM((L, 2 * W), F32)] * nb
                        + [pltpu.VMEM((N_HEADS, DH, 2 * DH), F32)] * nb
                        + [pltpu.VMEM((N_HEADS, 1, LANES), F32)] * nb),
        compiler_params=_cparams(("arbitrary", "arbitrary")),
        name="mlstm",
    )(main.reshape(B, S, 4 * W), gates.reshape(B, S, LANES), conv_w, conv_b, b_gate,
      head_norm).reshape(B * S, W)


GLA_STAGES = 4


def _gla_consts():
    L, R = GLA_L, GLA_R
    row = lax.broadcasted_iota(I32, (L, L), 0)
    col = lax.broadcasted_iota(I32, (L, L), 1)
    return dict(
        tri_bf=jnp.where(col <= row, 1.0, 0.0).astype(BF16),
        lane=lax.broadcasted_iota(I32, (L, LANES), 1),
        lane_r=lax.broadcasted_iota(I32, (R, LANES), 1),
        row_r=lax.broadcasted_iota(I32, (R, LANES), 0),
        wsel=jnp.where(lax.broadcasted_iota(I32, (LANES, 2 * LANES), 0) // GLA_DK
                       == lax.broadcasted_iota(I32, (LANES, 2 * LANES), 1) // LANES,
                       1.0, 0.0).astype(BF16))


def _gla_pair(p, la, rows, c, gla_ref, hn_ref, y_ref, sst):
    L, R = GLA_L, GLA_R
    WK = N_HEADS * GLA_DK
    WV = N_HEADS * DH
    dot = functools.partial(jnp.dot, preferred_element_type=F32)
    nt = (((1,), (1,)), ((), ()))
    tn = (((0,), (0,)), ((), ()))
    lane, lane_r, row_r = c['lane'], c['lane_r'], c['row_r']

    bc = _cumsum_rows(c['tri_bf'], la[:, p * LANES:(p + 1) * LANES])
    yield

    q2 = gla_ref[rows, p * LANES:(p + 1) * LANES].astype(F32) * (GLA_DK ** -0.5)
    k2 = gla_ref[rows, WK + p * LANES:WK + (p + 1) * LANES].astype(F32)
    g_last = bc[L - 1:L, :]
    q_in = q2 * jnp.exp(bc)
    kd = k2 * jnp.exp(g_last - bc)
    decay = jnp.exp(g_last)

    k_off = [None]
    q_off = [None]
    for j in range(1, L // R):
        rj = bc[j * R:j * R + 1, :]
        part = (k2[0:j * R, :] * jnp.exp(rj - bc[0:j * R, :])).astype(BF16)
        k_off.append(jnp.concatenate([part, jnp.zeros((L - j * R, LANES), BF16)], axis=0))
        q_off.append(q2[j * R:(j + 1) * R, :] * jnp.exp(bc[j * R:(j + 1) * R, :] - rj))

    half = SUBLANES
    row_h = row_r[0:half, :]
    zeros_h = jnp.zeros((half, LANES), F32)
    diag = []
    for j in range(L // R):
        qb = q2[j * R:(j + 1) * R, :]
        bq = bc[j * R:(j + 1) * R, :]
        terms = []
        for s in range(R):
            krow = k2[j * R + s:j * R + s + 1, :]
            brow = bc[j * R + s:j * R + s + 1, :]
            if s < half:
                d_top = jnp.where(row_h >= s, bq[0:half, :] - brow, NEG_INF)
                top = qb[0:half, :] * krow * jnp.exp(d_top)
                bot = qb[half:R, :] * krow * jnp.exp(bq[half:R, :] - brow)
            else:
                d_bot = jnp.where(row_h >= s - half, bq[half:R, :] - brow, NEG_INF)
                top = zeros_h
                bot = qb[half:R, :] * krow * jnp.exp(d_bot)
            terms.append(jnp.concatenate([top, bot], axis=0).astype(BF16))
        diag.append(dot(jnp.concatenate(terms, axis=0), c['wsel']))

    heads = []
    for hh in range(2):
        h = 2 * p + hh
        mh = (lane // GLA_DK) == hh
        mq = (lane_r // GLA_DK) == hh
        v = gla_ref[rows, 2 * WK + h * DH:2 * WK + (h + 1) * DH]
        st = sst[h]
        o_inter = lax.dot_general(jnp.where(mh, q_in, 0.0).astype(BF16), st.astype(BF16), nt,
                                  preferred_element_type=F32)
        upd = lax.dot_general(v, jnp.where(mh, kd, 0.0).astype(BF16), tn, preferred_element_type=F32)
        offs = [None] + [lax.dot_general(jnp.where(mq, q_off[j], 0.0).astype(BF16), k_off[j], nt,
                                         preferred_element_type=F32) for j in range(1, L // R)]
        heads.append((h, v, st, o_inter, upd, offs))
    yield

    outs = []
    for hh, (h, v, st, o_inter, upd, offs) in enumerate(heads):
        sst[h] = st * decay + upd
        a_rows = []
        for j in range(L // R):
            res = diag[j][:, hh * LANES:(hh + 1) * LANES]
            blk = jnp.zeros((R, LANES), F32)
            for s in range(R):
                blk = jnp.where(lane_r == j * R + s, res[s * R:(s + 1) * R, :], blk)
            blk = blk[:, 0:L]
            if j > 0:
                blk = blk + offs[j]
            a_rows.append(blk)
        a_mat = jnp.concatenate(a_rows, axis=0).astype(BF16)
        outs.append((h, o_inter, dot(a_mat, v)))
    yield

    for h, o_inter, o_intra in outs:
        o = o_inter + o_intra
        on = o * lax.rsqrt(jnp.mean(o * o, axis=-1, keepdims=True) + EPS)
        z = gla_ref[rows, 2 * WK + WV + h * DH:2 * WK + WV + (h + 1) * DH].astype(F32)
        y_ref[rows, h * DH:(h + 1) * DH] = (
            on * hn_ref[:, h * DH:(h + 1) * DH] * (z * _sigmoid(z))).astype(BF16)
    yield


def _gla_body(gla_ref, gate_ref, wgg_ref, bgg_ref, hn_ref, y_ref, *ssts, n_chunks):
    nb = gla_ref.shape[0]

    @pl.when(pl.program_id(1) == 0)
    def _():
        for bb in range(nb):
            ssts[bb][...] = jnp.zeros(ssts[bb].shape, F32)

    consts = _gla_consts()

    def step(i, carry):
        rows = pl.ds(pl.multiple_of(i * GLA_L, GLA_L), GLA_L)
        chains = []
        for bb in range(nb):
            la = _log_sigmoid(jnp.dot(gate_ref[bb, rows, :].astype(BF16), wgg_ref[...],
                                      preferred_element_type=F32) + bgg_ref[...]) / GLA_GATE_NORM
            chains += [_gla_pair(p, la, rows, consts, gla_ref.at[bb], hn_ref, y_ref.at[bb], ssts[bb])
                       for p in range(N_HEADS // 2)]
        for _ in range(GLA_STAGES):
            for ch in chains:
                next(ch)
        return carry

    lax.fori_loop(0, n_chunks, step, 0)


def _gla(gla, gates, wgg, bgg, head_norm, B, S, lt, nb):
    WK = N_HEADS * GLA_DK
    WV = N_HEADS * DH
    tok = lambda b, c: (b, c, 0)
    const = lambda b, c: (0, 0)
    return pl.pallas_call(
        functools.partial(_gla_body, n_chunks=lt // GLA_L),
        grid=(B // nb, S // lt),
        in_specs=[pl.BlockSpec((nb, lt, 2 * WK + 2 * WV), tok), pl.BlockSpec((nb, lt, LANES), tok),
                  pl.BlockSpec((LANES, WK), const), pl.BlockSpec((1, WK), const),
                  pl.BlockSpec((1, WV), const)],
        out_specs=pl.BlockSpec((nb, lt, WV), tok),
        out_shape=jax.ShapeDtypeStruct((B, S, WV), BF16),
        scratch_shapes=[pltpu.VMEM((N_HEADS, DH, LANES), F32)] * nb,
        compiler_params=_cparams(("arbitrary", "arbitrary")),
        name="gla",
    )(gla.reshape(B, S, 2 * WK + 2 * WV), gates.reshape(B, S, LANES), wgg, bgg,
      head_norm).reshape(B * S, WV)


def _outproj_router_body(ym_ref, yg_ref, h_ref, wo_ref, g_ref, wr_ref, br_ref,
                         hout_ref, xn_ref, route_ref, cnt_ref, carry):
    tm = h_ref.shape[0]
    W = ym_ref.shape[1]
    dot = functools.partial(jnp.dot, preferred_element_type=F32)

    @pl.when(pl.program_id(0) == 0)
    def _():
        carry[...] = jnp.zeros(carry.shape, F32)

    hnew = h_ref[...] + dot(ym_ref[...], wo_ref[0:W, :]) + dot(yg_ref[...], wo_ref[W:2 * W, :])
    hout_ref[...] = hnew
    xn = hnew * lax.rsqrt(jnp.mean(hnew * hnew, axis=-1, keepdims=True) + EPS) * g_ref[...]
    xn_ref[...] = xn

    xh = xn.astype(BF16)
    xm = (xn - xh.astype(F32)).astype(BF16)
    l2 = dot(xh, wr_ref[...])
    logits = l2[:, 0:LANES] + (l2[:, LANES:2 * LANES] + dot(xm, wr_ref[:, 0:LANES])) + br_ref[...]

    lane = lax.broadcasted_iota(I32, (tm, LANES), 1)
    lane_f = lane.astype(F32)
    big = float(LANES)
    gl = jnp.where(lane < N_GROUPS, logits, NEG_INF)
    gmax = jnp.max(gl, axis=-1, keepdims=True)
    gsel = jnp.min(jnp.where(gl == gmax, lane_f, big), axis=-1, keepdims=True)
    g_gate = 1.0 / jnp.sum(jnp.where(lane < N_GROUPS, jnp.exp(logits - gmax), 0.0),
                           axis=-1, keepdims=True)
    in_grp = ((lane >= EXP_LANE0) & (lane < EXP_LANE0 + N_EXPERTS)
              & (((lane - EXP_LANE0) // EPG).astype(F32) == gsel))
    el = jnp.where(in_grp, logits, NEG_INF)
    v1 = jnp.max(el, axis=-1, keepdims=True)
    i1 = jnp.min(jnp.where(el == v1, lane_f, big), axis=-1, keepdims=True)
    el2 = jnp.where(lane_f == i1, NEG_INF, el)
    v2 = jnp.max(el2, axis=-1, keepdims=True)
    i2 = jnp.min(jnp.where(el2 == v2, lane_f, big), axis=-1, keepdims=True)
    e2 = jnp.exp(v2 - v1)
    w1 = g_gate / (1.0 + e2)
    w2 = g_gate * e2 / (1.0 + e2)

    oh1 = lane_f == i1
    oh2 = lane_f == i2
    oh = jnp.where(oh1, 1.0, 0.0) + jnp.where(oh2, 1.0, 0.0)
    r_i = lax.broadcasted_iota(I32, (tm, tm), 0)
    c_i = lax.broadcasted_iota(I32, (tm, tm), 1)
    strict = jnp.where(c_i < r_i, 1.0, 0.0).astype(BF16)
    before = dot(strict, oh.astype(BF16)) + carry[0:1, :]
    rank1 = jnp.sum(jnp.where(oh1, before, 0.0), axis=-1, keepdims=True)
    rank2 = jnp.sum(jnp.where(oh2, before, 0.0), axis=-1, keepdims=True)
    total = carry[0:1, :] + jnp.sum(oh, axis=0, keepdims=True)
    carry[...] = jnp.broadcast_to(total, carry.shape)
    cnt_ref[...] = jnp.broadcast_to(total, cnt_ref.shape)

    rec = jnp.where(lane == 0, i1 - EXP_LANE0, 0.0)
    rec = jnp.where(lane == 1, i2 - EXP_LANE0, rec)
    rec = jnp.where(lane == 2, w1, rec)
    rec = jnp.where(lane == 3, w2, rec)
    rec = jnp.where(lane == 4, rank1, rec)
    rec = jnp.where(lane == 5, rank2, rec)
    route_ref[...] = rec


def _outproj_router(ym, yg, h2d, wo, gain, wr3, br, tm):
    T, D = h2d.shape
    W = ym.shape[1]
    row = lambda i: (i, 0)
    const = lambda i: (0, 0)
    return pl.pallas_call(
        _outproj_router_body,
        grid=(T // tm,),
        in_specs=[pl.BlockSpec((tm, W), row), pl.BlockSpec((tm, W), row), pl.BlockSpec((tm, D), row),
                  pl.BlockSpec((2 * W, D), const), pl.BlockSpec((1, D), const),
                  pl.BlockSpec((D, 2 * LANES), const), pl.BlockSpec((1, LANES), const)],
        out_specs=[pl.BlockSpec((tm, D), row), pl.BlockSpec((tm, D), row),
                   pl.BlockSpec((tm, LANES), row), pl.BlockSpec((SUBLANES, LANES), const)],
        out_shape=[jax.ShapeDtypeStruct((T, D), F32), jax.ShapeDtypeStruct((T, D), F32),
                   jax.ShapeDtypeStruct((T, LANES), F32),
                   jax.ShapeDtypeStruct((SUBLANES, LANES), F32)],
        scratch_shapes=[pltpu.VMEM((SUBLANES, LANES), F32)],
        compiler_params=_cparams(("arbitrary",)),
        name="outproj_router",
    )(ym, yg, h2d, wo, gain, wr3, br)


def _dispatch_body(pos_ref, pad_ref, x_hbm, xs_hbm, xbuf, zbuf, load_sem, scat_sem, pad_sem,
                   *, td, n_tiles):
    i = pl.program_id(0)
    ns = xbuf.shape[0]
    slot = lax.rem(i, ns)

    def load(tile, s):
        rows = pl.ds(pl.multiple_of(tile * td, td), td)
        return pltpu.make_async_copy(x_hbm.at[rows], xbuf.at[s], load_sem.at[s])

    def wait_scatter(s):
        for _ in range(2):
            pltpu.make_async_copy(xbuf.at[s], xs_hbm.at[pl.ds(0, td)], scat_sem.at[s]).wait()

    def for_each_pad_copy(fn):
        def per_expert(e, carry):
            first = pad_ref[0, 0, e]
            count = pad_ref[0, 0, N_EXPERTS + e]
            lead = jnp.minimum(count, lax.rem(SUBLANES - lax.rem(first, SUBLANES), SUBLANES))

            def per_row(r, c):
                fn(pltpu.make_async_copy(zbuf.at[pl.ds(0, 1)], xs_hbm.at[pl.ds(first + r, 1)], pad_sem))
                return c

            def per_group(g, c):
                rows = pl.ds(pl.multiple_of(first + lead + g * SUBLANES, SUBLANES), SUBLANES)
                fn(pltpu.make_async_copy(zbuf.at[pl.ds(0, SUBLANES)], xs_hbm.at[rows], pad_sem))
                return c

            carry = lax.fori_loop(0, lead, per_row, carry)
            return lax.fori_loop(0, (count - lead) // SUBLANES, per_group, carry)

        lax.fori_loop(0, N_EXPERTS, per_expert, 0)

        def per_block(bk, carry):
            rows = pl.ds(pl.multiple_of(bk * MOE_BLOCK, MOE_BLOCK), MOE_BLOCK)
            fn(pltpu.make_async_copy(zbuf, xs_hbm.at[rows], pad_sem))
            return carry

        lax.fori_loop(pad_ref[0, 0, 2 * N_EXPERTS], xs_hbm.shape[0] // MOE_BLOCK, per_block, 0)

    @pl.when(i == 0)
    def _():
        load(0, 0).start()
        zbuf[...] = jnp.zeros(zbuf.shape, F32)
        for_each_pad_copy(lambda cp: cp.start())

    @pl.when(i + 1 < n_tiles)
    def _():
        load(i + 1, lax.rem(i + 1, ns)).start()

    load(i, slot).wait()

    for j in range(td):
        for k in range(2):
            pltpu.make_async_copy(xbuf.at[slot, pl.ds(j, 1)],
                                  xs_hbm.at[pl.ds(pos_ref[0, 0, 2 * j + k], 1)],
                                  scat_sem.at[slot]).start(priority=k)

    @pl.when(i > 0)
    def _():
        wait_scatter(lax.rem(i + ns - 1, ns))

    @pl.when(i == n_tiles - 1)
    def _():
        wait_scatter(slot)
        for_each_pad_copy(lambda cp: cp.wait())


def _dispatch(pos, pads, xn, n_rows, td):
    T, D = xn.shape
    n = T // td
    return pl.pallas_call(
        functools.partial(_dispatch_body, td=td, n_tiles=n),
        grid=(n,),
        in_specs=[pl.BlockSpec((1, 1, 2 * td), lambda i: (i, 0, 0), memory_space=pltpu.SMEM),
                  pl.BlockSpec((1, 1, 2 * N_EXPERTS + 1), lambda i: (0, 0, 0),
                               memory_space=pltpu.SMEM),
                  pl.BlockSpec(memory_space=pl.ANY)],
        out_specs=pl.BlockSpec(memory_space=pl.ANY),
        out_shape=jax.ShapeDtypeStruct((n_rows, D), F32),
        scratch_shapes=[pltpu.VMEM((3, td, D), F32), pltpu.VMEM((MOE_BLOCK, D), F32),
                        pltpu.SemaphoreType.DMA((3,)), pltpu.SemaphoreType.DMA((3,)),
                        pltpu.SemaphoreType.DMA(())],
        compiler_params=_cparams(("arbitrary",)),
        name="dispatch",
    )(pos.reshape(n, 1, 2 * td), pads.reshape(1, 1, 2 * N_EXPERTS + 1), xn)


def _ffn_body(blk_exp_ref, nused_ref, xs_ref, wgu_ref, wd_ref, ys_ref, wgu_bf, wd_bf):
    b = pl.program_id(0)
    de = wd_ref.shape[2]
    used = b < nused_ref[0]
    new_expert = (b == 0) | (blk_exp_ref[b] != blk_exp_ref[jnp.maximum(b - 1, 0)])

    @pl.when(used & new_expert)
    def _():
        wgu_bf[...] = wgu_ref[0, 0].astype(BF16)
        wd_bf[...] = wd_ref[0, 0].astype(BF16)

    @pl.when(used)
    def _():
        gu = jnp.dot(xs_ref[...].astype(BF16), wgu_bf[...], preferred_element_type=F32)
        gate, up = gu[:, 0:de], gu[:, de:2 * de]
        act = (gate * _sigmoid(gate) * up).astype(BF16)
        ys_ref[...] = jnp.dot(act, wd_bf[...], preferred_element_type=F32)

    @pl.when(jnp.logical_not(used))
    def _():
        ys_ref[...] = jnp.zeros(ys_ref.shape, F32)


def _ffn(blk_exp, nused, xs, wgu, wd, layer):
    NR, D = xs.shape
    nb = NR // MOE_BLOCK
    de = wd.shape[2]
    grid_spec = pltpu.PrefetchScalarGridSpec(
        num_scalar_prefetch=2,
        grid=(nb,),
        in_specs=[pl.BlockSpec((MOE_BLOCK, D), lambda b, be, nu: (jnp.minimum(b, nu[0] - 1), 0)),
                  pl.BlockSpec((1, 1, D, 2 * de), lambda b, be, nu: (layer, be[b], 0, 0)),
                  pl.BlockSpec((1, 1, de, D), lambda b, be, nu: (layer, be[b], 0, 0))],
        out_specs=pl.BlockSpec((MOE_BLOCK, D), lambda b, be, nu: (b, 0)),
        scratch_shapes=[pltpu.VMEM((D, 2 * de), BF16), pltpu.VMEM((de, D), BF16)],
    )
    return pl.pallas_call(
        _ffn_body,
        grid_spec=grid_spec,
        out_shape=jax.ShapeDtypeStruct((NR, D), F32),
        compiler_params=_cparams(("arbitrary",)),
        name="expert_ffn",
    )(blk_exp, nused, xs, wgu, wd)


def _combine_body(pos_ref, posn_ref, ys_hbm, h_ref, route_ref, gfin_ref, out_ref, gbuf, sem,
                  *, tc, n_tiles, final):
    i = pl.program_id(0)
    slot = lax.rem(i, 2)

    def issue_tile(p_ref, s):
        def issue(jb, carry):
            for u in range(ROW_UNROLL):
                j = jb * ROW_UNROLL + u
                for k in range(2):
                    pltpu.make_async_copy(ys_hbm.at[pl.ds(p_ref[0, 0, 2 * j + k], 1)],
                                          gbuf.at[s, k, pl.ds(j, 1)], sem.at[s]).start(priority=k)
            return carry

        lax.fori_loop(0, tc // ROW_UNROLL, issue, 0)

    @pl.when(i == 0)
    def _():
        issue_tile(pos_ref, 0)

    @pl.when(i + 1 < n_tiles)
    def _():
        for j in range(tc):
            for k in range(2):
                pltpu.make_async_copy(ys_hbm.at[pl.ds(posn_ref[0, 0, 2 * j + k], 1)],
                                      gbuf.at[1 - slot, k, pl.ds(j, 1)],
                                      sem.at[1 - slot]).start(priority=k)

    for k in range(2):
        pltpu.make_async_copy(ys_hbm.at[pl.ds(0, tc)], gbuf.at[slot, k], sem.at[slot]).wait()

    w1 = route_ref[:, 2:3]
    w2 = route_ref[:, 3:4]
    out = h_ref[...] + (w1 * gbuf[slot, 0] + w2 * gbuf[slot, 1])
    if final:
        out = out * lax.rsqrt(jnp.mean(out * out, axis=-1, keepdims=True) + EPS) * gfin_ref[...]
    out_ref[...] = out


def _combine(pos, ys, h2d, route, gfin, tc, final):
    T, D = h2d.shape
    n = T // tc
    pos3 = pos.reshape(n, 1, 2 * tc)
    row = lambda i: (i, 0)
    smem = functools.partial(pl.BlockSpec, (1, 1, 2 * tc), memory_space=pltpu.SMEM)
    return pl.pallas_call(
        functools.partial(_combine_body, tc=tc, n_tiles=n, final=final),
        grid=(n,),
        in_specs=[smem(index_map=lambda i: (i, 0, 0)),
                  smem(index_map=lambda i: (jnp.minimum(i + 1, n - 1), 0, 0)),
                  pl.BlockSpec(memory_space=pl.ANY),
                  pl.BlockSpec((tc, D), row), pl.BlockSpec((tc, LANES), row),
                  pl.BlockSpec((1, D), lambda i: (0, 0))],
        out_specs=pl.BlockSpec((tc, D), row),
        out_shape=jax.ShapeDtypeStruct((T, D), F32),
        scratch_shapes=[pltpu.VMEM((2, 2, tc, D), F32), pltpu.SemaphoreType.DMA((2,))],
        compiler_params=_cparams(("arbitrary",)),
        name="combine",
    )(pos3, pos3, ys, h2d, route, gfin)


def _pick_tile(n, pref):
    t = min(pref, n)
    while n % t:
        t //= 2
    return t


def _prep_layer(w_in, b_mlstm_gate, w_gla_gate, b_gla_gate, w_out, w_group, b_group, w_expert,
                b_expert):
    W = N_HEADS * DH
    WK = N_HEADS * GLA_DK
    D = w_in.shape[0]
    o_gates = 4 * W
    o_gla = o_gates + 2 * N_HEADS
    o_r = o_gla + 2 * WK + 2 * W
    wm = w_in[:, 0:o_gates].astype(BF16)
    wg = w_in[:, o_gla:o_r].astype(BF16)
    ws = jnp.zeros((D, LANES), F32)
    ws = ws.at[:, 0:2 * N_HEADS].set(w_in[:, o_gates:o_gla])
    ws = ws.at[:, 2 * N_HEADS:2 * N_HEADS + GLA_RANK].set(w_in[:, o_r:o_r + GLA_RANK]).astype(BF16)
    bg = jnp.zeros((1, LANES), F32).at[0, 0:2 * N_HEADS].set(b_mlstm_gate)
    wgg = jnp.zeros((LANES, WK), F32).at[2 * N_HEADS:2 * N_HEADS + GLA_RANK, :].set(w_gla_gate)
    wr = jnp.zeros((D, LANES), F32)
    wr = wr.at[:, 0:N_GROUPS].set(w_group).at[:, EXP_LANE0:EXP_LANE0 + N_EXPERTS].set(w_expert)
    br = jnp.zeros((1, LANES), F32)
    br = br.at[0, 0:N_GROUPS].set(b_group).at[0, EXP_LANE0:EXP_LANE0 + N_EXPERTS].set(b_expert)
    return dict(wm=wm, wg=wg, ws=ws, bg=bg, wgg=wgg.astype(BF16), bgg=b_gla_gate[None, :],
                wo=w_out.astype(BF16), wr3=jnp.concatenate(_split3(wr)[0:2], axis=1), br=br)


def _routing_tables(route, counts_row, T):
    eid = route[:, 0:2].astype(I32)
    rank = route[:, 4:6].astype(I32)
    counts = counts_row[EXP_LANE0:EXP_LANE0 + N_EXPERTS].astype(I32)
    pcounts = (counts + MOE_BLOCK - 1) // MOE_BLOCK * MOE_BLOCK
    pend = jnp.cumsum(pcounts)
    pstart = pend - pcounts
    onehot = eid[:, :, None] == jnp.arange(N_EXPERTS, dtype=I32)[None, None, :]
    pos = jnp.sum(jnp.where(onehot, pstart[None, None, :], 0), axis=-1) + rank
    n_blocks = (T * 2 + MOE_BLOCK - 1) // MOE_BLOCK + N_EXPERTS
    blk_start = jnp.arange(n_blocks, dtype=I32) * MOE_BLOCK
    blk_exp = jnp.minimum(jnp.sum((pend[None, :] <= blk_start[:, None]).astype(I32), axis=1),
                          N_EXPERTS - 1)
    nused = (pend[-1] // MOE_BLOCK).astype(I32).reshape(1)
    pads = jnp.concatenate([pstart + counts, pcounts - counts, nused])
    return pos, pads, blk_exp, nused, n_blocks


def kernel(x, norm_mix, w_in, conv_w, conv_b, b_mlstm_gate, w_gla_gate, b_gla_gate, head_norm, w_out, norm_ffn, w_group, b_group, w_expert, b_expert, w_gu, w_down, norm_final):
    B, S, D = x.shape
    T = B * S
    depth = w_in.shape[0]
    W = N_HEADS * DH
    tm = _pick_tile(T, 512)
    td = _pick_tile(T, 1024)
    tc = _pick_tile(T, 1024)
    lt = _pick_tile(S, 256)
    nb = _pick_tile(B, 4)
    nb_gla = _pick_tile(B, 8)
    h = x.reshape(T, D)
    pending = None
    for l in range(depth):
        p = _prep_layer(w_in[l], b_mlstm_gate[l], w_gla_gate[l], b_gla_gate[l], w_out[l],
                        w_group[l], b_group[l], w_expert[l], b_expert[l])
        if pending is None:
            main, gla, gates = _inproj(h, norm_mix[l][None, :], p['wm'], p['wg'], p['ws'], tm)
        else:
            h, main, gla, gates = _inproj_combine(*pending, norm_mix[l][None, :], p['wm'], p['wg'],
                                                  p['ws'], tm)
        ym = _mlstm(main, gates, conv_w[l], conv_b[l][None, :], p['bg'], head_norm[l][None, 0:W], B, S,
                    nb)
        yg = _gla(gla, gates, p['wgg'], p['bgg'], head_norm[l][None, W:2 * W], B, S, lt, nb_gla)
        h, xn, route, cnt = _outproj_router(ym, yg, h, p['wo'], norm_ffn[l][None, :], p['wr3'],
                                            p['br'], tm)
        pos, pads, blk_exp, nused, n_blocks = _routing_tables(route, cnt[0], T)
        xs = _dispatch(pos, pads, xn, n_blocks * MOE_BLOCK, td)
        ys = _ffn(blk_exp, nused, xs, w_gu, w_down, l)
        pending = (pos, ys, h, route)
    pos, ys, h, route = pending
    h = _combine(pos, ys, h, route, norm_final[None, :], tc, final=True)
    return h.reshape(B, S, D)
```
